```python
import jax, jax.numpy as jnp
from jax import lax
import numpy as np

D_MODEL = 2048
BATCH = 4
SEQ = 4096
DEPTH = 1

Q_BLOCK = 128
HEAD_WIDTH = 128
MLA_HEADS = (D_MODEL // 2) // HEAD_WIDTH
MLA_NOPE_DIM = 128
MLA_ROPE_DIM = 64
MLA_V_DIM = 128
MLA_QK_DIM = MLA_NOPE_DIM + MLA_ROPE_DIM
MLA_Q_RANK = 768
MLA_KV_RANK = 512
MLA_WIDTH = MLA_HEADS * MLA_V_DIM
FOX_HEADS = (D_MODEL // 2) // HEAD_WIDTH
FOX_HEAD_DIM = HEAD_WIDTH
FOX_WIDTH = FOX_HEADS * FOX_HEAD_DIM
D_MIX = MLA_WIDTH + FOX_WIDTH
ROPE_THETA = 10000.0
NORM_EPS = 1e-6
IN_SPLITS = (MLA_Q_RANK, MLA_KV_RANK, MLA_ROPE_DIM, MLA_WIDTH,
             FOX_WIDTH, FOX_WIDTH, FOX_WIDTH, FOX_HEADS, FOX_WIDTH)
D_IN = (MLA_Q_RANK + MLA_KV_RANK + MLA_ROPE_DIM + MLA_WIDTH
        + 4 * FOX_WIDTH + FOX_HEADS)

kernel_name = "hybrid_mla_fox_parallel_heads"


def _rms_norm(x, g):
    xf = x.astype(jnp.float32)
    y = xf * lax.rsqrt(jnp.mean(xf * xf, axis=-1, keepdims=True) + NORM_EPS)
    return (y * g.astype(jnp.float32)).astype(x.dtype)


def _rope_angles(positions, dim):
    inv_freq = ROPE_THETA ** (-jnp.arange(0, dim, 2, dtype=jnp.float32) / dim)
    ang = positions.astype(jnp.float32)[..., None] * inv_freq
    return jnp.cos(ang), jnp.sin(ang)


def _apply_rope(x, cos, sin):
    xf = x.astype(jnp.float32)
    half = xf.shape[-1] // 2
    x1, x2 = xf[..., :half], xf[..., half:]
    out = jnp.concatenate([x1 * cos - x2 * sin, x2 * cos + x1 * sin], axis=-1)
    return out.astype(x.dtype)


def _causal_block_sweep(score_fn, v):
    b, seq = v.shape[0], v.shape[1]
    n_blocks = seq // Q_BLOCK
    key_pos = jnp.arange(seq)

    def one_block(i):
        start = i * Q_BLOCK
        logits = score_fn(start)
        q_pos = start + jnp.arange(Q_BLOCK)
        logits = jnp.where(key_pos[None, :] <= q_pos[:, None], logits, -jnp.inf)
        p = jax.nn.softmax(logits, axis=-1).astype(v.dtype)
        return jnp.einsum('bhqs,bshd->bqhd', p, v)

    out = lax.map(one_block, jnp.arange(n_blocks))
    return out.transpose(1, 0, 2, 3, 4).reshape(b, seq, -1)


def _mla_branch(q_lat, kv_lat, k_rope_raw, g_q, w_uq, g_kv, w_ukv, cos, sin):
    b, s, _ = q_lat.shape
    q = (_rms_norm(q_lat, g_q) @ w_uq).reshape(b, s, MLA_HEADS, MLA_QK_DIM)
    q_nope = q[..., :MLA_NOPE_DIM]
    q_rope = _apply_rope(q[..., MLA_NOPE_DIM:], cos[:, :, None, :], sin[:, :, None, :])
    kv = (_rms_norm(kv_lat, g_kv) @ w_ukv).reshape(b, s, MLA_HEADS, MLA_NOPE_DIM + MLA_V_DIM)
    k_nope, v = kv[..., :MLA_NOPE_DIM], kv[..., MLA_NOPE_DIM:]
    k_rope = _apply_rope(k_rope_raw, cos, sin)
    scale = MLA_QK_DIM ** -0.5

    def score(start):
        qn = lax.dynamic_slice_in_dim(q_nope, start, Q_BLOCK, axis=1)
        qr = lax.dynamic_slice_in_dim(q_rope, start, Q_BLOCK, axis=1)
        s_nope = jnp.einsum('bqhd,bshd->bhqs', qn, k_nope, preferred_element_type=jnp.float32)
        s_rope = jnp.einsum('bqhr,bsr->bhqs', qr, k_rope, preferred_element_type=jnp.float32)
        return (s_nope + s_rope) * scale

    return _causal_block_sweep(score, v)


def _fox_branch(q, k, v, f_logit, b_forget):
    b, s, _ = q.shape
    q = q.reshape(b, s, FOX_HEADS, FOX_HEAD_DIM)
    k = k.reshape(b, s, FOX_HEADS, FOX_HEAD_DIM)
    v = v.reshape(b, s, FOX_HEADS, FOX_HEAD_DIM)
    log_f = jax.nn.log_sigmoid(f_logit.astype(jnp.float32) + b_forget.astype(jnp.float32))
    c = jnp.cumsum(log_f, axis=1).transpose(0, 2, 1)
    scale = FOX_HEAD_DIM ** -0.5

    def score(start):
        qb = lax.dynamic_slice_in_dim(q, start, Q_BLOCK, axis=1)
        cq = lax.dynamic_slice_in_dim(c, start, Q_BLOCK, axis=2)
        logits = jnp.einsum('bqhd,bshd->bhqs', qb, k, preferred_element_type=jnp.float32) * scale
        return logits + cq[:, :, :, None] - c[:, :, None, :]

    return _causal_block_sweep(score, v)


def _hybrid_layer(x, cos, sin, g_pre, w_in, g_q, w_uq, g_kv, w_ukv, b_forget, w_out, g_post):
    h = _rms_norm(x, g_pre)
    proj = h @ w_in
    split_points = np.cumsum(IN_SPLITS)[:-1].tolist()
    (q_lat, kv_lat, k_rope_raw, gate_mla,
     fq, fk, fv, f_logit, gate_fox) = jnp.split(proj, split_points, axis=-1)
    o_mla = _mla_branch(q_lat, kv_lat, k_rope_raw, g_q, w_uq, g_kv, w_ukv, cos, sin) * jax.nn.silu(gate_mla)
    o_fox = _fox_branch(fq, fk, fv, f_logit, b_forget) * jax.nn.silu(gate_fox)
    o = jnp.concatenate([o_mla, o_fox], axis=-1) @ w_out
    return x + _rms_norm(o, g_post)


def setup_inputs(seed: int = 0) -> dict:
    key = jax.random.key(seed)
    ks = jax.random.split(key, 12)
    f32 = jnp.float32
    x = jax.random.normal(ks[0], (BATCH, SEQ, D_MODEL), f32)
    offsets = jax.random.randint(ks[1], (BATCH, 1), 0, 1024, dtype=jnp.int32)
    positions = jnp.arange(SEQ, dtype=jnp.int32)[None, :] + offsets
    g_pre = 1.0 + 0.02 * jax.random.normal(ks[2], (DEPTH, D_MODEL), f32)
    w_in = jax.random.normal(ks[3], (DEPTH, D_MODEL, D_IN), f32) * D_MODEL ** -0.5
    g_q_latent = 1.0 + 0.02 * jax.random.normal(ks[4], (DEPTH, MLA_Q_RANK), f32)
    w_uq = jax.random.normal(ks[5], (DEPTH, MLA_Q_RANK, MLA_HEADS * MLA_QK_DIM), f32) * MLA_Q_RANK ** -0.5
    g_kv_latent = 1.0 + 0.02 * jax.random.normal(ks[6], (DEPTH, MLA_KV_RANK), f32)
    w_ukv = jax.random.normal(ks[7], (DEPTH, MLA_KV_RANK, MLA_HEADS * (MLA_NOPE_DIM + MLA_V_DIM)), f32) * MLA_KV_RANK ** -0.5
    b_forget = 3.0 + 0.1 * jax.random.normal(ks[8], (DEPTH, FOX_HEADS), f32)
    w_out = jax.random.normal(ks[9], (DEPTH, D_MIX, D_MODEL), f32) * D_MIX ** -0.5
    g_post = 1.0 + 0.02 * jax.random.normal(ks[10], (DEPTH, D_MODEL), f32)
    return {"x": x, "positions": positions, "g_pre": g_pre, "w_in": w_in,
            "g_q_latent": g_q_latent, "w_uq": w_uq, "g_kv_latent": g_kv_latent,
            "w_ukv": w_ukv, "b_forget": b_forget, "w_out": w_out, "g_post": g_post}


def reference(x, positions, g_pre, w_in, g_q_latent, w_uq, g_kv_latent, w_ukv, b_forget, w_out, g_post):
    cos, sin = _rope_angles(positions, MLA_ROPE_DIM)
    for l in range(DEPTH):
        x = _hybrid_layer(x, cos, sin, g_pre[l], w_in[l], g_q_latent[l], w_uq[l],
                          g_kv_latent[l], w_ukv[l], b_forget[l], w_out[l], g_post[l])
    return x
```

```python
import functools
import math

import jax
import jax.numpy as jnp
from jax import lax
from jax.experimental import pallas as pl
from jax.experimental.pallas import tpu as pltpu

F32 = jnp.float32
BF16 = jnp.bfloat16

D_MODEL = 2048
N_HEADS = 8
HEAD_DIM = 128
NOPE_DIM = 128
ROPE_DIM = 64
ROPE_HALF = ROPE_DIM // 2
QK_DIM = NOPE_DIM + ROPE_DIM
Q_RANK = 768
KV_RANK = 512
LAT_DIM = Q_RANK + KV_RANK
BRANCH_WIDTH = N_HEADS * HEAD_DIM
D_MIX = 2 * BRANCH_WIDTH
ROPE_THETA = 10000.0
NORM_EPS = 1e-6
LOG2E = math.log2(math.e)

LANES = 128
FEAT = 256
SMALL_W = 2 * LANES
REST_W = 5 * BRANCH_WIDTH

MLA_QSCALE = QK_DIM ** -0.5 * LOG2E
FOX_QSCALE = HEAD_DIM ** -0.5 * LOG2E

PROJ_TM = 512
PREP_TM = 512
ATT_TQ = 512
ATT_TK = 512
OUT_TM = 512

VMEM_LIMIT = 56 * 1024 * 1024


def _rms(x, g):
    return x * lax.rsqrt(jnp.mean(x * x, axis=-1, keepdims=True) + NORM_EPS) * g


def _proj_kernel(x_ref, g_ref, wlat_ref, wsm_ref, wrest_ref,
                 lat_ref, sm_ref, rest_ref, h_scr):
    j = pl.program_id(1)

    @pl.when(j == 0)
    def _():
        hb = _rms(x_ref[...], g_ref[...]).astype(BF16)
        h_scr[...] = hb
        lat_ref[...] = jnp.dot(hb, wlat_ref[...],
                               preferred_element_type=F32).astype(BF16)
        sm_ref[...] = jnp.dot(hb, wsm_ref[...], preferred_element_type=F32)

    acc = jnp.dot(h_scr[...], wrest_ref[...], preferred_element_type=F32)
    scale = jnp.where(j == 1, FOX_QSCALE, 1.0).astype(F32)
    rest_ref[...] = (acc * scale).astype(BF16)


def _proj(x2, g_pre, w_lat, w_small, w_rest):
    t = x2.shape[0]
    n_rest = REST_W // BRANCH_WIDTH
    return pl.pallas_call(
        _proj_kernel,
        grid=(t // PROJ_TM, n_rest),
        in_specs=[
            pl.BlockSpec((PROJ_TM, D_MODEL), lambda i, j: (i, 0)),
            pl.BlockSpec((1, D_MODEL), lambda i, j: (0, 0)),
            pl.BlockSpec((D_MODEL, LAT_DIM), lambda i, j: (0, 0)),
            pl.BlockSpec((D_MODEL, SMALL_W), lambda i, j: (0, 0)),
            pl.BlockSpec((D_MODEL, BRANCH_WIDTH), lambda i, j: (0, j)),
        ],
        out_specs=[
            pl.BlockSpec((PROJ_TM, LAT_DIM), lambda i, j: (i, 0)),
            pl.BlockSpec((PROJ_TM, SMALL_W), lambda i, j: (i, 0)),
            pl.BlockSpec((PROJ_TM, BRANCH_WIDTH), lambda i, j: (i, j)),
        ],
        out_shape=[
            jax.ShapeDtypeStruct((t, LAT_DIM), BF16),
            jax.ShapeDtypeStruct((t, SMALL_W), F32),
            jax.ShapeDtypeStruct((t, REST_W), BF16),
        ],
        scratch_shapes=[pltpu.VMEM((PROJ_TM, D_MODEL), BF16)],
        compiler_params=pltpu.CompilerParams(
            dimension_semantics=("arbitrary", "arbitrary"),
            vmem_limit_bytes=VMEM_LIMIT),
        name="proj",
    )(x2, g_pre, w_lat, w_small, w_rest)


def _split3(x):
    hi = x.astype(BF16).astype(F32)
    r = x - hi
    mid = r.astype(BF16).astype(F32)
    lo = (r - mid).astype(BF16).astype(F32)
    return hi, mid, lo


def _prep_kernel(lat_ref, sm_ref, fq_ref, fk_ref, fv_ref, pos_ref,
                 gq_ref, gkv_ref, wuq_ref, wukv_ref, bf_ref, invf_ref,
                 qp_ref, kp_ref, v_ref, carry_scr):
    tm = lat_ref.shape[0]

    @pl.when(pl.program_id(1) == 0)
    def _():
        carry_scr[...] = jnp.zeros_like(carry_scr)

    lat = lat_ref[...].astype(F32)
    qn = _rms(lat[:, :Q_RANK], gq_ref[...]).astype(BF16)
    kvn = _rms(lat[:, Q_RANK:], gkv_ref[...]).astype(BF16)
    q = jnp.dot(qn, wuq_ref[...], preferred_element_type=F32)
    kv = jnp.dot(kvn, wukv_ref[...], preferred_element_type=F32)
    sm = sm_ref[...]

    lane = lax.broadcasted_iota(jnp.int32, (tm, LANES), 1)
    ang = pos_ref[...].astype(F32) * invf_ref[...]
    cosv = jnp.cos(ang)
    sinv = jnp.sin(ang)
    sin_hi = jnp.where(lane < ROPE_HALF, 0.0,
                       jnp.where(lane < ROPE_DIM, sinv, 0.0))
    sin_lo = jnp.where(lane < ROPE_HALF, -sinv, 0.0)

    def rope(r):
        return (r * cosv
                + pltpu.roll(r, ROPE_HALF, 1) * sin_hi
                + pltpu.roll(r, LANES - ROPE_HALF, 1) * sin_lo)

    k_rope = rope(sm[:, :LANES]).astype(BF16)
    for h in range(N_HEADS):
        c0 = h * FEAT
        qp_ref[:, c0:c0 + LANES] = (q[:, c0:c0 + LANES] * MLA_QSCALE).astype(BF16)
        qp_ref[:, c0 + LANES:c0 + FEAT] = (
            rope(q[:, c0 + LANES:c0 + FEAT]) * MLA_QSCALE).astype(BF16)
        kp_ref[:, c0:c0 + LANES] = kv[:, c0:c0 + LANES].astype(BF16)
        kp_ref[:, c0 + LANES:c0 + FEAT] = k_rope
        v_ref[:, h * HEAD_DIM:(h + 1) * HEAD_DIM] = kv[:, c0 + LANES:c0 + FEAT].astype(BF16)

    fl = sm[:, LANES:] + bf_ref[...]
    logf = jnp.minimum(fl, 0.0) - jnp.log1p(jnp.exp(-jnp.abs(fl)))
    row = lax.broadcasted_iota(jnp.int32, (tm, tm), 0)
    col = lax.broadcasted_iota(jnp.int32, (tm, tm), 1)
    tri = jnp.where(row >= col, 1.0, 0.0).astype(BF16)
    c = carry_scr[...]
    for piece in _split3(logf):
        c = c + jnp.dot(tri, piece.astype(BF16), preferred_element_type=F32)
    carry_scr[...] = c[tm - 1:tm, :]
    c2 = c * LOG2E

    for h in range(N_HEADS):
        c0 = (N_HEADS + h) * FEAT
        hi, mid, lo = _split3(jnp.broadcast_to(c2[:, h:h + 1], (tm, LANES)))
        cq = jnp.where(lane == 0, hi, jnp.where(lane == 1, mid, jnp.where(
            lane == 2, lo, jnp.where(lane < 6, 1.0, 0.0))))
        ck = jnp.where(lane < 3, 1.0, jnp.where(lane == 3, -hi, jnp.where(
            lane == 4, -mid, jnp.where(lane == 5, -lo, 0.0))))
        s0 = h * HEAD_DIM
        qp_ref[:, c0:c0 + LANES] = fq_ref[:, s0:s0 + HEAD_DIM]
        qp_ref[:, c0 + LANES:c0 + FEAT] = cq.astype(BF16)
        kp_ref[:, c0:c0 + LANES] = fk_ref[:, s0:s0 + HEAD_DIM]
        kp_ref[:, c0 + LANES:c0 + FEAT] = ck.astype(BF16)
    v_ref[:, BRANCH_WIDTH:] = fv_ref[...]


def _prep(lat, small, rest, pos, g_q, g_kv, w_uq, w_ukv, b_f, inv_f, batch, seq):
    t = lat.shape[0]
    ns = seq // PREP_TM
    row = lambda b, i: (b * ns + i, 0)
    const = lambda b, i: (0, 0)
    return pl.pallas_call(
        _prep_kernel,
        grid=(batch, ns),
        in_specs=[
            pl.BlockSpec((PREP_TM, LAT_DIM), row),
            pl.BlockSpec((PREP_TM, SMALL_W), row),
            pl.BlockSpec((PREP_TM, BRANCH_WIDTH), lambda b, i: (b * ns + i, 1)),
            pl.BlockSpec((PREP_TM, BRANCH_WIDTH), lambda b, i: (b * ns + i, 2)),
            pl.BlockSpec((PREP_TM, BRANCH_WIDTH), lambda b, i: (b * ns + i, 3)),
            pl.BlockSpec((PREP_TM, 1), row),
            pl.BlockSpec((1, Q_RANK), const),
            pl.BlockSpec((1, KV_RANK), const),
            pl.BlockSpec((Q_RANK, N_HEADS * FEAT), const),
            pl.BlockSpec((KV_RANK, N_HEADS * FEAT), const),
            pl.BlockSpec((1, LANES), const),
            pl.BlockSpec((1, LANES), const),
        ],
        out_specs=[
            pl.BlockSpec((PREP_TM, 2 * N_HEADS * FEAT), row),
            pl.BlockSpec((PREP_TM, 2 * N_HEADS * FEAT), row),
            pl.BlockSpec((PREP_TM, D_MIX), row),
        ],
        out_shape=[
            jax.ShapeDtypeStruct((t, 2 * N_HEADS * FEAT), BF16),
            jax.ShapeDtypeStruct((t, 2 * N_HEADS * FEAT), BF16),
            jax.ShapeDtypeStruct((t, D_MIX), BF16),
        ],
        scratch_shapes=[pltpu.VMEM((1, LANES), F32)],
        compiler_params=pltpu.CompilerParams(
            dimension_semantics=("arbitrary", "arbitrary"),
            vmem_limit_bytes=VMEM_LIMIT),
        name="prep",
    )(lat, small, rest, rest, rest, pos, g_q, g_kv, w_uq, w_ukv, b_f, inv_f)


def _attn_kernel(q_ref, k_ref, v_ref, o_ref, m_scr, l_scr, acc_scr):
    i = pl.program_id(2)
    tq = q_ref.shape[0]
    tk = ATT_TK
    q = q_ref[...]
    m_scr[...] = jnp.full(m_scr.shape, -jnp.inf, F32)
    l_scr[...] = jnp.zeros(l_scr.shape, F32)
    acc_scr[...] = jnp.zeros(acc_scr.shape, F32)

    def step(j, masked):
        off = pl.multiple_of(j * tk, tk)
        k = k_ref[pl.ds(off, tk), :]
        v = v_ref[pl.ds(off, tk), :]
        s = lax.dot_general(q, k, (((1,), (1,)), ((), ())),
                            preferred_element_type=F32)
        if masked:
            row = lax.broadcasted_iota(jnp.int32, (tq, tk), 0)
            col = lax.broadcasted_iota(jnp.int32, (tq, tk), 1)
            s = jnp.where(col <= row, s, -jnp.inf)
        m_old = m_scr[...]
        m_new = jnp.maximum(m_old, jnp.max(s, axis=1, keepdims=True))
        alpha = jnp.exp2(m_old - m_new)
        p = jnp.exp2(s - m_new)
        l_scr[...] = alpha * l_scr[...] + jnp.sum(p, axis=1, keepdims=True)
        acc_scr[...] = alpha * acc_scr[...] + jnp.dot(
            p.astype(BF16), v, preferred_element_type=F32)
        m_scr[...] = m_new

    def body(j, carry):
        step(j, False)
        return carry

    lax.fori_loop(0, i, body, 0)
    step(i, True)
    o_ref[...] = (acc_scr[...] / l_scr[...]).astype(BF16)


def _attn(qp, kp, v, batch, seq):
    t = qp.shape[0]
    nq = seq // ATT_TQ
    nh = 2 * N_HEADS
    return pl.pallas_call(
        _attn_kernel,
        grid=(batch, nh, nq),
        in_specs=[
            pl.BlockSpec((ATT_TQ, FEAT), lambda b, h, i: (b * nq + i, h)),
            pl.BlockSpec((seq, FEAT), lambda b, h, i: (b, h)),
            pl.BlockSpec((seq, HEAD_DIM), lambda b, h, i: (b, h)),
        ],
        out_specs=pl.BlockSpec((ATT_TQ, HEAD_DIM), lambda b, h, i: (b * nq + i, h)),
        out_shape=jax.ShapeDtypeStruct((t, D_MIX), BF16),
        scratch_shapes=[
            pltpu.VMEM((ATT_TQ, 1), F32),
            pltpu.VMEM((ATT_TQ, 1), F32),
            pltpu.VMEM((ATT_TQ, HEAD_DIM), F32),
        ],
        compiler_params=pltpu.CompilerParams(
            dimension_semantics=("arbitrary", "arbitrary", "arbitrary"),
            vmem_limit_bytes=VMEM_LIMIT),
        name="attn",
    )(qp, kp, v)


def _silu(g):
    return g / (1.0 + jnp.exp(-g))


def _out_kernel(o_ref, gm_ref, gf_ref, x_ref, w_ref, g_ref, y_ref):
    o = o_ref[...].astype(F32)
    om = (o[:, :BRANCH_WIDTH] * _silu(gm_ref[...].astype(F32))).astype(BF16)
    of = (o[:, BRANCH_WIDTH:] * _silu(gf_ref[...].astype(F32))).astype(BF16)
    y = (jnp.dot(om, w_ref[:BRANCH_WIDTH, :], preferred_element_type=F32)
         + jnp.dot(of, w_ref[BRANCH_WIDTH:, :], preferred_element_type=F32))
    y_ref[...] = x_ref[...] + _rms(y, g_ref[...])


def _out(o, rest, x2, w_out, g_post):
    t = o.shape[0]
    return pl.pallas_call(
        _out_kernel,
        grid=(t // OUT_TM,),
        in_specs=[
            pl.BlockSpec((OUT_TM, D_MIX), lambda i: (i, 0)),
            pl.BlockSpec((OUT_TM, BRANCH_WIDTH), lambda i: (i, 0)),
            pl.BlockSpec((OUT_TM, BRANCH_WIDTH), lambda i: (i, 4)),
            pl.BlockSpec((OUT_TM, D_MODEL), lambda i: (i, 0)),
            pl.BlockSpec((D_MIX, D_MODEL), lambda i: (0, 0)),
            pl.BlockSpec((1, D_MODEL), lambda i: (0, 0)),
        ],
        out_specs=pl.BlockSpec((OUT_TM, D_MODEL), lambda i: (i, 0)),
        out_shape=jax.ShapeDtypeStruct((t, D_MODEL), F32),
        compiler_params=pltpu.CompilerParams(
            dimension_semantics=("arbitrary",),
            vmem_limit_bytes=VMEM_LIMIT),
        name="out",
    )(o, rest, rest, x2, w_out, g_post)


def _layer(x2, pos, inv_f, g_pre, w_in, g_q, w_uq, g_kv, w_ukv, b_forget,
           w_out, g_post, batch, seq):
    o_kr = LAT_DIM
    o_gm = o_kr + ROPE_DIM
    o_fq = o_gm + BRANCH_WIDTH
    o_fl = o_fq + 3 * BRANCH_WIDTH
    o_gf = o_fl + N_HEADS
    w_lat = w_in[:, :LAT_DIM].astype(BF16)
    w_small = jnp.zeros((D_MODEL, SMALL_W), F32)
    w_small = w_small.at[:, :ROPE_DIM].set(w_in[:, o_kr:o_gm])
    w_small = w_small.at[:, LANES:LANES + N_HEADS].set(w_in[:, o_fl:o_gf])
    w_small = w_small.astype(BF16)
    w_rest = jnp.concatenate([w_in[:, o_gm:o_fl], w_in[:, o_gf:]], axis=1).astype(BF16)
    w_uq_p = jnp.pad(w_uq.reshape(Q_RANK, N_HEADS, QK_DIM),
                     ((0, 0), (0, 0), (0, FEAT - QK_DIM))).reshape(
                         Q_RANK, N_HEADS * FEAT).astype(BF16)
    w_ukv_b = w_ukv.astype(BF16)
    w_out_b = w_out.astype(BF16)
    b_f = jnp.zeros((1, LANES), F32).at[0, :N_HEADS].set(b_forget)

    lat, small, rest = _proj(x2, g_pre[None, :], w_lat, w_small, w_rest)
    qp, kp, v = _prep(lat, small, rest, pos, g_q[None, :], g_kv[None, :],
                      w_uq_p, w_ukv_b, b_f, inv_f, batch, seq)
    o = _attn(qp, kp, v, batch, seq)
    return _out(o, rest, x2, w_out_b, g_post[None, :])


def kernel(x, positions, g_pre, w_in, g_q_latent, w_uq, g_kv_latent, w_ukv,
           b_forget, w_out, g_post):
    batch, seq, d = x.shape
    depth = g_pre.shape[0]
    x2 = x.reshape(batch * seq, d)
    pos = positions.reshape(batch * seq, 1)
    freqs = ROPE_THETA ** (-jnp.arange(0, ROPE_DIM, 2, dtype=F32) / ROPE_DIM)
    inv_f = jnp.zeros((1, LANES), F32)
    inv_f = inv_f.at[0, :ROPE_HALF].set(freqs).at[0, ROPE_HALF:ROPE_DIM].set(freqs)
    for l in range(depth):
        x2 = _layer(x2, pos, inv_f, g_pre[l], w_in[l], g_q_latent[l], w_uq[l],
                    g_kv_latent[l], w_ukv[l], b_forget[l], w_out[l], g_post[l],
                    batch, seq)
    return x2.reshape(batch, seq, d)
```

```python
import functools
import math

import jax
import jax.numpy as jnp
from jax import lax
from jax.experimental import pallas as pl
from jax.experimental.pallas import tpu as pltpu

F32 = jnp.float32
BF16 = jnp.bfloat16

D_MODEL = 2048
N_HEADS = 8
HEAD_DIM = 128
NOPE_DIM = 128
ROPE_DIM = 64
ROPE_HALF = ROPE_DIM // 2
QK_DIM = NOPE_DIM + ROPE_DIM
Q_RANK = 768
KV_RANK = 512
LAT_DIM = Q_RANK + KV_RANK
BRANCH_WIDTH = N_HEADS * HEAD_DIM
D_MIX = 2 * BRANCH_WIDTH
ROPE_THETA = 10000.0
NORM_EPS = 1e-6
LOG2E = math.log2(math.e)

LANES = 128
FEAT = 256
SMALL_W = 2 * LANES
REST_W = 5 * BRANCH_WIDTH

MLA_QSCALE = QK_DIM ** -0.5 * LOG2E
FOX_QSCALE = HEAD_DIM ** -0.5 * LOG2E

PROJ_TM = 512
PREP_TM = 512
ATT_T = 512
OUT_TM = 512

VMEM_LIMIT = 56 * 1024 * 1024


def _rms(x, g):
    return x * lax.rsqrt(jnp.mean(x * x, axis=-1, keepdims=True) + NORM_EPS) * g


def _proj_kernel(x_ref, g_ref, wlat_ref, wsm_ref, wrest_ref,
                 lat_ref, sm_ref, rest_ref, h_scr):
    j = pl.program_id(1)

    @pl.when(j == 0)
    def _():
        hb = _rms(x_ref[...], g_ref[...]).astype(BF16)
        h_scr[...] = hb
        lat_ref[...] = jnp.dot(hb, wlat_ref[...],
                               preferred_element_type=F32).astype(BF16)
        sm_ref[...] = jnp.dot(hb, wsm_ref[...], preferred_element_type=F32)

    acc = jnp.dot(h_scr[...], wrest_ref[...], preferred_element_type=F32)
    scale = jnp.where(j == 1, FOX_QSCALE, 1.0).astype(F32)
    rest_ref[...] = (acc * scale).astype(BF16)


def _proj(x2, g_pre, w_lat, w_small, w_rest):
    t = x2.shape[0]
    n_rest = REST_W // BRANCH_WIDTH
    return pl.pallas_call(
        _proj_kernel,
        grid=(t // PROJ_TM, n_rest),
        in_specs=[
            pl.BlockSpec((PROJ_TM, D_MODEL), lambda i, j: (i, 0)),
            pl.BlockSpec((1, D_MODEL), lambda i, j: (0, 0)),
            pl.BlockSpec((D_MODEL, LAT_DIM), lambda i, j: (0, 0)),
            pl.BlockSpec((D_MODEL, SMALL_W), lambda i, j: (0, 0)),
            pl.BlockSpec((D_MODEL, BRANCH_WIDTH), lambda i, j: (0, j)),
        ],
        out_specs=[
            pl.BlockSpec((PROJ_TM, LAT_DIM), lambda i, j: (i, 0)),
            pl.BlockSpec((PROJ_TM, SMALL_W), lambda i, j: (i, 0)),
            pl.BlockSpec((PROJ_TM, BRANCH_WIDTH), lambda i, j: (i, j)),
        ],
        out_shape=[
            jax.ShapeDtypeStruct((t, LAT_DIM), BF16),
            jax.ShapeDtypeStruct((t, SMALL_W), F32),
            jax.ShapeDtypeStruct((t, REST_W), BF16),
        ],
        scratch_shapes=[pltpu.VMEM((PROJ_TM, D_MODEL), BF16)],
        compiler_params=pltpu.CompilerParams(
            dimension_semantics=("arbitrary", "arbitrary"),
            vmem_limit_bytes=VMEM_LIMIT),
        name="proj",
    )(x2, g_pre, w_lat, w_small, w_rest)


def _split3(x):
    hi = x.astype(BF16).astype(F32)
    r = x - hi
    mid = r.astype(BF16).astype(F32)
    lo = (r - mid).astype(BF16).astype(F32)
    return hi, mid, lo


def _prep_kernel(lat_ref, sm_ref, fq_ref, fk_ref, fv_ref, pos_ref,
                 gq_ref, gkv_ref, wuq_ref, wukv_ref, bf_ref, invf_ref,
                 qp_ref, kp_ref, v_ref, carry_scr):
    tm = lat_ref.shape[0]

    @pl.when(pl.program_id(1) == 0)
    def _():
        carry_scr[...] = jnp.zeros_like(carry_scr)

    lat = lat_ref[...].astype(F32)
    qn = _rms(lat[:, :Q_RANK], gq_ref[...]).astype(BF16)
    kvn = _rms(lat[:, Q_RANK:], gkv_ref[...]).astype(BF16)
    q = jnp.dot(qn, wuq_ref[...], preferred_element_type=F32)
    kv = jnp.dot(kvn, wukv_ref[...], preferred_element_type=F32)
    sm = sm_ref[...]

    lane = lax.broadcasted_iota(jnp.int32, (tm, LANES), 1)
    ang = pos_ref[...].astype(F32) * invf_ref[...]
    cosv = jnp.cos(ang)
    sinv = jnp.sin(ang)
    sin_hi = jnp.where(lane < ROPE_HALF, 0.0,
                       jnp.where(lane < ROPE_DIM, sinv, 0.0))
    sin_lo = jnp.where(lane < ROPE_HALF, -sinv, 0.0)

    def rope(r):
        return (r * cosv
                + pltpu.roll(r, ROPE_HALF, 1) * sin_hi
                + pltpu.roll(r, LANES - ROPE_HALF, 1) * sin_lo)

    k_rope = rope(sm[:, :LANES]).astype(BF16)
    for h in range(N_HEADS):
        c0 = h * FEAT
        qp_ref[:, c0:c0 + LANES] = (q[:, c0:c0 + LANES] * MLA_QSCALE).astype(BF16)
        qp_ref[:, c0 + LANES:c0 + FEAT] = (
            rope(q[:, c0 + LANES:c0 + FEAT]) * MLA_QSCALE).astype(BF16)
        kp_ref[:, c0:c0 + LANES] = kv[:, c0:c0 + LANES].astype(BF16)
        kp_ref[:, c0 + LANES:c0 + FEAT] = k_rope
        v_ref[:, h * HEAD_DIM:(h + 1) * HEAD_DIM] = kv[:, c0 + LANES:c0 + FEAT].astype(BF16)

    fl = sm[:, LANES:] + bf_ref[...]
    logf = jnp.minimum(fl, 0.0) - jnp.log1p(jnp.exp(-jnp.abs(fl)))
    row = lax.broadcasted_iota(jnp.int32, (tm, tm), 0)
    col = lax.broadcasted_iota(jnp.int32, (tm, tm), 1)
    tri = jnp.where(row >= col, 1.0, 0.0).astype(BF16)
    c = carry_scr[...]
    for piece in _split3(logf):
        c = c + jnp.dot(tri, piece.astype(BF16), preferred_element_type=F32)
    carry_scr[...] = c[tm - 1:tm, :]
    c2 = c * LOG2E

    for h in range(N_HEADS):
        c0 = (N_HEADS + h) * FEAT
        hi, mid, lo = _split3(jnp.broadcast_to(c2[:, h:h + 1], (tm, LANES)))
        cq = jnp.where(lane == 0, hi, jnp.where(lane == 1, mid, jnp.where(
            lane == 2, lo, jnp.where(lane < 6, 1.0, 0.0))))
        ck = jnp.where(lane < 3, 1.0, jnp.where(lane == 3, -hi, jnp.where(
            lane == 4, -mid, jnp.where(lane == 5, -lo, 0.0))))
        s0 = h * HEAD_DIM
        qp_ref[:, c0:c0 + LANES] = fq_ref[:, s0:s0 + HEAD_DIM]
        qp_ref[:, c0 + LANES:c0 + FEAT] = cq.astype(BF16)
        kp_ref[:, c0:c0 + LANES] = fk_ref[:, s0:s0 + HEAD_DIM]
        kp_ref[:, c0 + LANES:c0 + FEAT] = ck.astype(BF16)
    v_ref[:, BRANCH_WIDTH:] = fv_ref[...]


def _prep(lat, small, rest, pos, g_q, g_kv, w_uq, w_ukv, b_f, inv_f, batch, seq):
    t = lat.shape[0]
    ns = seq // PREP_TM
    row = lambda b, i: (b * ns + i, 0)
    const = lambda b, i: (0, 0)
    return pl.pallas_call(
        _prep_kernel,
        grid=(batch, ns),
        in_specs=[
            pl.BlockSpec((PREP_TM, LAT_DIM), row),
            pl.BlockSpec((PREP_TM, SMALL_W), row),
            pl.BlockSpec((PREP_TM, BRANCH_WIDTH), lambda b, i: (b * ns + i, 1)),
            pl.BlockSpec((PREP_TM, BRANCH_WIDTH), lambda b, i: (b * ns + i, 2)),
            pl.BlockSpec((PREP_TM, BRANCH_WIDTH), lambda b, i: (b * ns + i, 3)),
            pl.BlockSpec((PREP_TM, 1), row),
            pl.BlockSpec((1, Q_RANK), const),
            pl.BlockSpec((1, KV_RANK), const),
            pl.BlockSpec((Q_RANK, N_HEADS * FEAT), const),
            pl.BlockSpec((KV_RANK, N_HEADS * FEAT), const),
            pl.BlockSpec((1, LANES), const),
            pl.BlockSpec((1, LANES), const),
        ],
        out_specs=[
            pl.BlockSpec((PREP_TM, 2 * N_HEADS * FEAT), row),
            pl.BlockSpec((PREP_TM, 2 * N_HEADS * FEAT), row),
            pl.BlockSpec((PREP_TM, D_MIX), row),
        ],
        out_shape=[
            jax.ShapeDtypeStruct((t, 2 * N_HEADS * FEAT), BF16),
            jax.ShapeDtypeStruct((t, 2 * N_HEADS * FEAT), BF16),
            jax.ShapeDtypeStruct((t, D_MIX), BF16),
        ],
        scratch_shapes=[pltpu.VMEM((1, LANES), F32)],
        compiler_params=pltpu.CompilerParams(
            dimension_semantics=("arbitrary", "arbitrary"),
            vmem_limit_bytes=VMEM_LIMIT),
        name="prep",
    )(lat, small, rest, rest, rest, pos, g_q, g_kv, w_uq, w_ukv, b_f, inv_f)


def _attn_kernel(qt_ref, k_ref, vt_ref, o_ref,
                 s0, s1, p0, p1, cm0, cm1, al0, al1, lf0, lf1,
                 m_scr, l_scr, acc_scr):
    t = ATT_T
    nq = qt_ref.shape[0]
    nsteps = nq * (nq + 1) // 2
    s_buf, p_buf, cm_buf = (s0, s1), (p0, p1), (cm0, cm1)
    al_buf, lf_buf = (al0, al1), (lf0, lf1)

    def advance(ij):
        i, j = ij
        wrap = j >= i
        return jnp.where(wrap, i + 1, i), jnp.where(wrap, 0, j + 1)

    def stage_a(ij, slot, zero_row):
        i, j = ij
        i = jnp.minimum(i, nq - 1)
        k = k_ref[pl.ds(pl.multiple_of(j * t, t), t), :]
        s = jnp.dot(k, qt_ref[i], preferred_element_type=F32)
        s = s + zero_row
        s_buf[slot][...] = s
        cm_buf[slot][...] = jnp.max(s, axis=0, keepdims=True)

    def stage_b(ij, slot, masked):
        i, j = ij
        s = s_buf[slot][...]
        if masked:
            key = lax.broadcasted_iota(jnp.int32, (t, t), 0)
            qry = lax.broadcasted_iota(jnp.int32, (t, t), 1)
            s = jnp.where(key <= qry, s, -jnp.inf)
            cm = jnp.max(s, axis=0, keepdims=True)
        else:
            cm = cm_buf[slot][...]
        m_old = jnp.where(j == 0, -jnp.inf, m_scr[...])
        m_new = jnp.maximum(m_old, cm)
        alpha = jnp.exp2(m_old - m_new)
        p = jnp.exp2(s - m_new)
        l_new = alpha * l_scr[...] + jnp.sum(p, axis=0, keepdims=True)
        m_scr[...] = m_new
        l_scr[...] = l_new
        p_buf[slot][...] = p.astype(BF16)
        al_buf[slot][...] = alpha
        lf_buf[slot][...] = l_new
        bits = lax.shift_right_logical(
            lax.shift_right_logical(pltpu.bitcast(l_new, jnp.int32), 16), 16)
        return bits.astype(F32)

    def stage_c(ij, slot):
        i, j = ij
        pv = jnp.dot(vt_ref[j], p_buf[slot][...], preferred_element_type=F32)
        acc_scr[...] = al_buf[slot][...] * acc_scr[...] + pv

        @pl.when(j == i)
        def _():
            o = acc_scr[...] / lf_buf[slot][...]
            o_ref[pl.ds(pl.multiple_of(i * t, t), t), :] = o.T.astype(BF16)

    def iteration(slot_a, ija, ijb, ijc):
        slot_b = 1 - slot_a
        ib, jb = ijb

        @pl.when(jb == ib)
        def _():
            stage_a(ija, slot_a, stage_b(ijb, slot_b, True))
            stage_c(ijc, slot_a)

        @pl.when(jb != ib)
        def _():
            stage_a(ija, slot_a, stage_b(ijb, slot_b, False))
            stage_c(ijc, slot_a)

    zero = jnp.int32(0)
    m_scr[...] = jnp.full(m_scr.shape, -jnp.inf, F32)
    l_scr[...] = jnp.zeros(l_scr.shape, F32)
    acc_scr[...] = jnp.zeros(acc_scr.shape, F32)
    p1[...] = jnp.zeros(p1.shape, BF16)
    al1[...] = jnp.zeros(al1.shape, F32)
    stage_a((zero, zero), 0, jnp.zeros((1, t), F32))

    def pair(_, carry):
        ija, ijb, ijc = carry
        iteration(1, ija, ijb, ijc)
        ija2 = advance(ija)
        iteration(0, ija2, ija, ijb)
        return advance(ija2), ija2, ija

    step1 = advance((zero, zero))
    dummy = (jnp.int32(1), zero)
    carry = lax.fori_loop(0, nsteps // 2, pair, (step1, (zero, zero), dummy))
    stage_c(carry[2], 1)


def _attn(qt, kp, vt, batch, seq):
    t = kp.shape[0]
    nq = seq // ATT_T
    assert (nq * (nq + 1) // 2) % 2 == 0
    nh = 2 * N_HEADS
    row = pltpu.VMEM((1, ATT_T), F32)
    return pl.pallas_call(
        _attn_kernel,
        grid=(batch, nh),
        in_specs=[
            pl.BlockSpec((nq, FEAT, ATT_T), lambda b, h: (b, h, 0)),
            pl.BlockSpec((seq, FEAT), lambda b, h: (b, h)),
            pl.BlockSpec((nq, HEAD_DIM, ATT_T), lambda b, h: (b, h, 0)),
        ],
        out_specs=pl.BlockSpec((seq, HEAD_DIM), lambda b, h: (b, h)),
        out_shape=jax.ShapeDtypeStruct((t, D_MIX), BF16),
        scratch_shapes=[
            pltpu.VMEM((ATT_T, ATT_T), F32), pltpu.VMEM((ATT_T, ATT_T), F32),
            pltpu.VMEM((ATT_T, ATT_T), BF16), pltpu.VMEM((ATT_T, ATT_T), BF16),
            row, row, row, row, row, row,
            row, row, pltpu.VMEM((HEAD_DIM, ATT_T), F32),
        ],
        compiler_params=pltpu.CompilerParams(
            dimension_semantics=("arbitrary", "arbitrary"),
            vmem_limit_bytes=VMEM_LIMIT),
        name="attn",
    )(qt, kp, vt)


def _silu(g):
    return g / (1.0 + jnp.exp(-g))


def _out_kernel(o_ref, gm_ref, gf_ref, x_ref, w_ref, g_ref, y_ref):
    o = o_ref[...].astype(F32)
    om = (o[:, :BRANCH_WIDTH] * _silu(gm_ref[...].astype(F32))).astype(BF16)
    of = (o[:, BRANCH_WIDTH:] * _silu(gf_ref[...].astype(F32))).astype(BF16)
    y = (jnp.dot(om, w_ref[:BRANCH_WIDTH, :], preferred_element_type=F32)
         + jnp.dot(of, w_ref[BRANCH_WIDTH:, :], preferred_element_type=F32))
    y_ref[...] = x_ref[...] + _rms(y, g_ref[...])


def _out(o, rest, x2, w_out, g_post):
    t = o.shape[0]
    return pl.pallas_call(
        _out_kernel,
        grid=(t // OUT_TM,),
        in_specs=[
            pl.BlockSpec((OUT_TM, D_MIX), lambda i: (i, 0)),
            pl.BlockSpec((OUT_TM, BRANCH_WIDTH), lambda i: (i, 0)),
            pl.BlockSpec((OUT_TM, BRANCH_WIDTH), lambda i: (i, 4)),
            pl.BlockSpec((OUT_TM, D_MODEL), lambda i: (i, 0)),
            pl.BlockSpec((D_MIX, D_MODEL), lambda i: (0, 0)),
            pl.BlockSpec((1, D_MODEL), lambda i: (0, 0)),
        ],
        out_specs=pl.BlockSpec((OUT_TM, D_MODEL), lambda i: (i, 0)),
        out_shape=jax.ShapeDtypeStruct((t, D_MODEL), F32),
        compiler_params=pltpu.CompilerParams(
            dimension_semantics=("arbitrary",),
            vmem_limit_bytes=VMEM_LIMIT),
        name="out",
    )(o, rest, rest, x2, w_out, g_post)


def _layer(x2, pos, inv_f, g_pre, w_in, g_q, w_uq, g_kv, w_ukv, b_forget,
           w_out, g_post, batch, seq):
    o_kr = LAT_DIM
    o_gm = o_kr + ROPE_DIM
    o_fq = o_gm + BRANCH_WIDTH
    o_fl = o_fq + 3 * BRANCH_WIDTH
    o_gf = o_fl + N_HEADS
    w_lat = w_in[:, :LAT_DIM].astype(BF16)
    w_small = jnp.zeros((D_MODEL, SMALL_W), F32)
    w_small = w_small.at[:, :ROPE_DIM].set(w_in[:, o_kr:o_gm])
    w_small = w_small.at[:, LANES:LANES + N_HEADS].set(w_in[:, o_fl:o_gf])
    w_small = w_small.astype(BF16)
    w_rest = jnp.concatenate([w_in[:, o_gm:o_fl], w_in[:, o_gf:]], axis=1).astype(BF16)
    w_uq_p = jnp.pad(w_uq.reshape(Q_RANK, N_HEADS, QK_DIM),
                     ((0, 0), (0, 0), (0, FEAT - QK_DIM))).reshape(
                         Q_RANK, N_HEADS * FEAT).astype(BF16)
    w_ukv_b = w_ukv.astype(BF16)
    w_out_b = w_out.astype(BF16)
    b_f = jnp.zeros((1, LANES), F32).at[0, :N_HEADS].set(b_forget)

    lat, small, rest = _proj(x2, g_pre[None, :], w_lat, w_small, w_rest)
    qp, kp, v = _prep(lat, small, rest, pos, g_q[None, :], g_kv[None, :],
                      w_uq_p, w_ukv_b, b_f, inv_f, batch, seq)
    nt = batch * (seq // ATT_T)
    qt = qp.reshape(nt, ATT_T, 2 * N_HEADS * FEAT).transpose(0, 2, 1)
    vt = v.reshape(nt, ATT_T, D_MIX).transpose(0, 2, 1)
    o = _attn(qt, kp, vt, batch, seq)
    return _out(o, rest, x2, w_out_b, g_post[None, :])


def kernel(x, positions, g_pre, w_in, g_q_latent, w_uq, g_kv_latent, w_ukv,
           b_forget, w_out, g_post):
    batch, seq, d = x.shape
    depth = g_pre.shape[0]
    x2 = x.reshape(batch * seq, d)
    pos = positions.reshape(batch * seq, 1)
    freqs = ROPE_THETA ** (-jnp.arange(0, ROPE_DIM, 2, dtype=F32) / ROPE_DIM)
    inv_f = jnp.zeros((1, LANES), F32)
    inv_f = inv_f.at[0, :ROPE_HALF].set(freqs).at[0, ROPE_HALF:ROPE_DIM].set(freqs)
    for l in range(depth):
        x2 = _layer(x2, pos, inv_f, g_pre[l], w_in[l], g_q_latent[l], w_uq[l],
                    g_kv_latent[l], w_ukv[l], b_forget[l], w_out[l], g_post[l],
                    batch, seq)
    return x2.reshape(batch, seq, d)
```

```python
import functools
import math

import jax
import jax.numpy as jnp
from jax import lax
from jax.experimental import pallas as pl
from jax.experimental.pallas import tpu as pltpu

F32 = jnp.float32
BF16 = jnp.bfloat16

D_MODEL = 2048
N_HEADS = 8
HEAD_DIM = 128
NOPE_DIM = 128
ROPE_DIM = 64
ROPE_HALF = ROPE_DIM // 2
QK_DIM = NOPE_DIM + ROPE_DIM
Q_RANK = 768
KV_RANK = 512
LAT_DIM = Q_RANK + KV_RANK
BRANCH_WIDTH = N_HEADS * HEAD_DIM
D_MIX = 2 * BRANCH_WIDTH
ROPE_THETA = 10000.0
NORM_EPS = 1e-6
LOG2E = math.log2(math.e)

LANES = 128
FEAT = 256
SMALL_W = 2 * LANES
REST_W = 5 * BRANCH_WIDTH

MLA_QSCALE = QK_DIM ** -0.5 * LOG2E
FOX_QSCALE = HEAD_DIM ** -0.5 * LOG2E

PROJ_TM = 512
PREP_TM = 512
ATT_T = 1024
OUT_TM = 512

VMEM_LIMIT = 56 * 1024 * 1024


def _rms(x, g):
    return x * lax.rsqrt(jnp.mean(x * x, axis=-1, keepdims=True) + NORM_EPS) * g


def _proj_kernel(x_ref, g_ref, wlat_ref, wsm_ref, wrest_ref,
                 lat_ref, sm_ref, rest_ref, h_scr):
    j = pl.program_id(1)

    @pl.when(j == 0)
    def _():
        hb = _rms(x_ref[...], g_ref[...]).astype(BF16)
        h_scr[...] = hb
        lat_ref[...] = jnp.dot(hb, wlat_ref[...],
                               preferred_element_type=F32).astype(BF16)
        sm_ref[...] = jnp.dot(hb, wsm_ref[...], preferred_element_type=F32)

    acc = jnp.dot(h_scr[...], wrest_ref[...], preferred_element_type=F32)
    scale = jnp.where(j == 1, FOX_QSCALE, 1.0).astype(F32)
    rest_ref[...] = (acc * scale).astype(BF16)


def _proj(x2, g_pre, w_lat, w_small, w_rest):
    t = x2.shape[0]
    n_rest = REST_W // BRANCH_WIDTH
    return pl.pallas_call(
        _proj_kernel,
        grid=(t // PROJ_TM, n_rest),
        in_specs=[
            pl.BlockSpec((PROJ_TM, D_MODEL), lambda i, j: (i, 0)),
            pl.BlockSpec((1, D_MODEL), lambda i, j: (0, 0)),
            pl.BlockSpec((D_MODEL, LAT_DIM), lambda i, j: (0, 0)),
            pl.BlockSpec((D_MODEL, SMALL_W), lambda i, j: (0, 0)),
            pl.BlockSpec((D_MODEL, BRANCH_WIDTH), lambda i, j: (0, j)),
        ],
        out_specs=[
            pl.BlockSpec((PROJ_TM, LAT_DIM), lambda i, j: (i, 0)),
            pl.BlockSpec((PROJ_TM, SMALL_W), lambda i, j: (i, 0)),
            pl.BlockSpec((PROJ_TM, BRANCH_WIDTH), lambda i, j: (i, j)),
        ],
        out_shape=[
            jax.ShapeDtypeStruct((t, LAT_DIM), BF16),
            jax.ShapeDtypeStruct((t, SMALL_W), F32),
            jax.ShapeDtypeStruct((t, REST_W), BF16),
        ],
        scratch_shapes=[pltpu.VMEM((PROJ_TM, D_MODEL), BF16)],
        compiler_params=pltpu.CompilerParams(
            dimension_semantics=("arbitrary", "arbitrary"),
            vmem_limit_bytes=VMEM_LIMIT),
        name="proj",
    )(x2, g_pre, w_lat, w_small, w_rest)


def _split3(x):
    hi = x.astype(BF16).astype(F32)
    r = x - hi
    mid = r.astype(BF16).astype(F32)
    lo = (r - mid).astype(BF16).astype(F32)
    return hi, mid, lo


def _prep_kernel(lat_ref, sm_ref, fq_ref, fk_ref, fv_ref, pos_ref,
                 gq_ref, gkv_ref, wuq_ref, wukv_ref, bf_ref, invf_ref,
                 qt_ref, kp_ref, vt_ref, carry_scr):
    tm = lat_ref.shape[0]

    def put_t(ref, r0, x):
        ref[0, r0:r0 + x.shape[1], :] = x.astype(F32).T.astype(BF16)

    @pl.when(pl.program_id(1) == 0)
    def _():
        carry_scr[...] = jnp.zeros_like(carry_scr)

    lat = lat_ref[...].astype(F32)
    qn = _rms(lat[:, :Q_RANK], gq_ref[...]).astype(BF16)
    kvn = _rms(lat[:, Q_RANK:], gkv_ref[...]).astype(BF16)
    q = jnp.dot(qn, wuq_ref[...], preferred_element_type=F32)
    kv = jnp.dot(kvn, wukv_ref[...], preferred_element_type=F32)
    sm = sm_ref[...]

    lane = lax.broadcasted_iota(jnp.int32, (tm, LANES), 1)
    ang = pos_ref[...].astype(F32) * invf_ref[...]
    cosv = jnp.cos(ang)
    sinv = jnp.sin(ang)
    sin_hi = jnp.where(lane < ROPE_HALF, 0.0,
                       jnp.where(lane < ROPE_DIM, sinv, 0.0))
    sin_lo = jnp.where(lane < ROPE_HALF, -sinv, 0.0)

    def rope(r):
        return (r * cosv
                + pltpu.roll(r, ROPE_HALF, 1) * sin_hi
                + pltpu.roll(r, LANES - ROPE_HALF, 1) * sin_lo)

    k_rope = rope(sm[:, :LANES]).astype(BF16)
    for h in range(N_HEADS):
        c0 = h * FEAT
        put_t(qt_ref, c0, q[:, c0:c0 + LANES] * MLA_QSCALE)
        put_t(qt_ref, c0 + LANES, rope(q[:, c0 + LANES:c0 + FEAT]) * MLA_QSCALE)
        kp_ref[:, c0:c0 + LANES] = kv[:, c0:c0 + LANES].astype(BF16)
        kp_ref[:, c0 + LANES:c0 + FEAT] = k_rope
        put_t(vt_ref, h * HEAD_DIM, kv[:, c0 + LANES:c0 + FEAT])

    fl = sm[:, LANES:] + bf_ref[...]
    logf = jnp.minimum(fl, 0.0) - jnp.log1p(jnp.exp(-jnp.abs(fl)))
    row = lax.broadcasted_iota(jnp.int32, (tm, tm), 0)
    col = lax.broadcasted_iota(jnp.int32, (tm, tm), 1)
    tri = jnp.where(row >= col, 1.0, 0.0).astype(BF16)
    c = carry_scr[...]
    for piece in _split3(logf):
        c = c + jnp.dot(tri, piece.astype(BF16), preferred_element_type=F32)
    carry_scr[...] = c[tm - 1:tm, :]
    c2 = c * LOG2E

    for h in range(N_HEADS):
        c0 = (N_HEADS + h) * FEAT
        hi, mid, lo = _split3(jnp.broadcast_to(c2[:, h:h + 1], (tm, LANES)))
        cq = jnp.where(lane == 0, hi, jnp.where(lane == 1, mid, jnp.where(
            lane == 2, lo, jnp.where(lane < 6, 1.0, 0.0))))
        ck = jnp.where(lane < 3, 1.0, jnp.where(lane == 3, -hi, jnp.where(
            lane == 4, -mid, jnp.where(lane == 5, -lo, 0.0))))
        s0 = h * HEAD_DIM
        put_t(qt_ref, c0, fq_ref[:, s0:s0 + HEAD_DIM])
        put_t(qt_ref, c0 + LANES, cq)
        kp_ref[:, c0:c0 + LANES] = fk_ref[:, s0:s0 + HEAD_DIM]
        kp_ref[:, c0 + LANES:c0 + FEAT] = ck.astype(BF16)
        put_t(vt_ref, BRANCH_WIDTH + s0, fv_ref[:, s0:s0 + HEAD_DIM])


def _prep(lat, small, rest, pos, g_q, g_kv, w_uq, w_ukv, b_f, inv_f, batch, seq):
    t = lat.shape[0]
    ns = seq // PREP_TM
    per = ATT_T // PREP_TM
    nt = t // ATT_T
    row = lambda b, i: (b * ns + i, 0)
    const = lambda b, i: (0, 0)
    tile_t = lambda b, i: ((b * ns + i) // per, 0, (b * ns + i) % per)
    return pl.pallas_call(
        _prep_kernel,
        grid=(batch, ns),
        in_specs=[
            pl.BlockSpec((PREP_TM, LAT_DIM), row),
            pl.BlockSpec((PREP_TM, SMALL_W), row),
            pl.BlockSpec((PREP_TM, BRANCH_WIDTH), lambda b, i: (b * ns + i, 1)),
            pl.BlockSpec((PREP_TM, BRANCH_WIDTH), lambda b, i: (b * ns + i, 2)),
            pl.BlockSpec((PREP_TM, BRANCH_WIDTH), lambda b, i: (b * ns + i, 3)),
            pl.BlockSpec((PREP_TM, 1), row),
            pl.BlockSpec((1, Q_RANK), const),
            pl.BlockSpec((1, KV_RANK), const),
            pl.BlockSpec((Q_RANK, N_HEADS * FEAT), const),
            pl.BlockSpec((KV_RANK, N_HEADS * FEAT), const),
            pl.BlockSpec((1, LANES), const),
            pl.BlockSpec((1, LANES), const),
        ],
        out_specs=[
            pl.BlockSpec((1, 2 * N_HEADS * FEAT, PREP_TM), tile_t),
            pl.BlockSpec((PREP_TM, 2 * N_HEADS * FEAT), row),
            pl.BlockSpec((1, D_MIX, PREP_TM), tile_t),
        ],
        out_shape=[
            jax.ShapeDtypeStruct((nt, 2 * N_HEADS * FEAT, ATT_T), BF16),
            jax.ShapeDtypeStruct((t, 2 * N_HEADS * FEAT), BF16),
            jax.ShapeDtypeStruct((nt, D_MIX, ATT_T), BF16),
        ],
        scratch_shapes=[pltpu.VMEM((1, LANES), F32)],
        compiler_params=pltpu.CompilerParams(
            dimension_semantics=("arbitrary", "arbitrary"),
            vmem_limit_bytes=VMEM_LIMIT),
        name="prep",
    )(lat, small, rest, rest, rest, pos, g_q, g_kv, w_uq, w_ukv, b_f, inv_f)


def _attn_kernel(qt_ref, k_ref, vt_ref, o_ref, s0, s1, cm0, cm1, m_scr, l_scr, acc_scr):
    t = ATT_T
    nq = qt_ref.shape[0]
    nsteps = nq * (nq + 1) // 2
    s_buf, cm_buf = (s0, s1), (cm0, cm1)

    def advance(ij):
        i, j = ij
        wrap = j >= i
        return jnp.where(wrap, i + 1, i), jnp.where(wrap, 0, j + 1)

    def stage_a(ij, slot):
        i, j = ij
        i = jnp.minimum(i, nq - 1)
        k = k_ref[pl.ds(pl.multiple_of(j * t, t), t), :]
        s = jnp.dot(k, qt_ref[i], preferred_element_type=F32)
        s_buf[slot][...] = s
        cm_buf[slot][...] = jnp.max(s, axis=0, keepdims=True)

    def stage_bc(ij, slot, diagonal):
        i, j = ij
        s = s_buf[slot][...]
        if diagonal:
            key = lax.broadcasted_iota(jnp.int32, (t, t), 0)
            qry = lax.broadcasted_iota(jnp.int32, (t, t), 1)
            s = jnp.where(key <= qry, s, -jnp.inf)
            cm = jnp.max(s, axis=0, keepdims=True)
        else:
            cm = cm_buf[slot][...]
        m_old = jnp.where(j == 0, -jnp.inf, m_scr[...])
        m_new = jnp.maximum(m_old, cm)
        alpha = jnp.exp2(m_old - m_new)
        p = jnp.exp2(s - m_new)
        l_new = alpha * l_scr[...] + jnp.sum(p, axis=0, keepdims=True)
        pv = jnp.dot(vt_ref[j], p.astype(BF16), preferred_element_type=F32)
        acc = alpha * acc_scr[...] + pv
        if diagonal:
            o_ref[pl.ds(pl.multiple_of(i * t, t), t), :] = (acc / l_new).T.astype(BF16)
        else:
            m_scr[...] = m_new
            l_scr[...] = l_new
            acc_scr[...] = acc

    def iteration(slot_a, ija, ijb):
        ib, jb = ijb

        @pl.when(jb == ib)
        def _():
            stage_a(ija, slot_a)
            stage_bc(ijb, 1 - slot_a, True)

        @pl.when(jb != ib)
        def _():
            stage_a(ija, slot_a)
            stage_bc(ijb, 1 - slot_a, False)

    zero = jnp.int32(0)
    m_scr[...] = jnp.full(m_scr.shape, -jnp.inf, F32)
    l_scr[...] = jnp.zeros(l_scr.shape, F32)
    acc_scr[...] = jnp.zeros(acc_scr.shape, F32)
    stage_a((zero, zero), 0)

    def pair(_, carry):
        ija, ijb = carry
        iteration(1, ija, ijb)
        ija2 = advance(ija)
        iteration(0, ija2, ija)
        return advance(ija2), ija2

    lax.fori_loop(0, nsteps // 2, pair, (advance((zero, zero)), (zero, zero)))


def _attn(qt, kp, vt, batch, seq):
    t = kp.shape[0]
    nq = seq // ATT_T
    assert (nq * (nq + 1) // 2) % 2 == 0
    nh = 2 * N_HEADS
    row = pltpu.VMEM((1, ATT_T), F32)
    return pl.pallas_call(
        _attn_kernel,
        grid=(batch, nh),
        in_specs=[
            pl.BlockSpec((nq, FEAT, ATT_T), lambda b, h: (b, h, 0)),
            pl.BlockSpec((seq, FEAT), lambda b, h: (b, h)),
            pl.BlockSpec((nq, HEAD_DIM, ATT_T), lambda b, h: (b, h, 0)),
        ],
        out_specs=pl.BlockSpec((seq, HEAD_DIM), lambda b, h: (b, h)),
        out_shape=jax.ShapeDtypeStruct((t, D_MIX), BF16),
        scratch_shapes=[
            pltpu.VMEM((ATT_T, ATT_T), F32), pltpu.VMEM((ATT_T, ATT_T), F32),
            row, row, row, row, pltpu.VMEM((HEAD_DIM, ATT_T), F32),
        ],
        compiler_params=pltpu.CompilerParams(
            dimension_semantics=("arbitrary", "arbitrary"),
            vmem_limit_bytes=VMEM_LIMIT),
        name="attn",
    )(qt, kp, vt)


def _silu(g):
    return g / (1.0 + jnp.exp(-g))


def _out_kernel(o_ref, gm_ref, gf_ref, x_ref, w_ref, g_ref, y_ref):
    o = o_ref[...].astype(F32)
    om = (o[:, :BRANCH_WIDTH] * _silu(gm_ref[...].astype(F32))).astype(BF16)
    of = (o[:, BRANCH_WIDTH:] * _silu(gf_ref[...].astype(F32))).astype(BF16)
    y = (jnp.dot(om, w_ref[:BRANCH_WIDTH, :], preferred_element_type=F32)
         + jnp.dot(of, w_ref[BRANCH_WIDTH:, :], preferred_element_type=F32))
    y_ref[...] = x_ref[...] + _rms(y, g_ref[...])


def _out(o, rest, x2, w_out, g_post):
    t = o.shape[0]
    return pl.pallas_call(
        _out_kernel,
        grid=(t // OUT_TM,),
        in_specs=[
            pl.BlockSpec((OUT_TM, D_MIX), lambda i: (i, 0)),
            pl.BlockSpec((OUT_TM, BRANCH_WIDTH), lambda i: (i, 0)),
            pl.BlockSpec((OUT_TM, BRANCH_WIDTH), lambda i: (i, 4)),
            pl.BlockSpec((OUT_TM, D_MODEL), lambda i: (i, 0)),
            pl.BlockSpec((D_MIX, D_MODEL), lambda i: (0, 0)),
            pl.BlockSpec((1, D_MODEL), lambda i: (0, 0)),
        ],
        out_specs=pl.BlockSpec((OUT_TM, D_MODEL), lambda i: (i, 0)),
        out_shape=jax.ShapeDtypeStruct((t, D_MODEL), F32),
        compiler_params=pltpu.CompilerParams(
            dimension_semantics=("arbitrary",),
            vmem_limit_bytes=VMEM_LIMIT),
        name="out",
    )(o, rest, rest, x2, w_out, g_post)


def _layer(x2, pos, inv_f, g_pre, w_in, g_q, w_uq, g_kv, w_ukv, b_forget,
           w_out, g_post, batch, seq):
    o_kr = LAT_DIM
    o_gm = o_kr + ROPE_DIM
    o_fq = o_gm + BRANCH_WIDTH
    o_fl = o_fq + 3 * BRANCH_WIDTH
    o_gf = o_fl + N_HEADS
    w_lat = w_in[:, :LAT_DIM].astype(BF16)
    w_small = jnp.zeros((D_MODEL, SMALL_W), F32)
    w_small = w_small.at[:, :ROPE_DIM].set(w_in[:, o_kr:o_gm])
    w_small = w_small.at[:, LANES:LANES + N_HEADS].set(w_in[:, o_fl:o_gf])
    w_small = w_small.astype(BF16)
    w_rest = jnp.concatenate([w_in[:, o_gm:o_fl], w_in[:, o_gf:]], axis=1).astype(BF16)
    w_uq_p = jnp.pad(w_uq.reshape(Q_RANK, N_HEADS, QK_DIM),
                     ((0, 0), (0, 0), (0, FEAT - QK_DIM))).reshape(
                         Q_RANK, N_HEADS * FEAT).astype(BF16)
    w_ukv_b = w_ukv.astype(BF16)
    w_out_b = w_out.astype(BF16)
    b_f = jnp.zeros((1, LANES), F32).at[0, :N_HEADS].set(b_forget)

    lat, small, rest = _proj(x2, g_pre[None, :], w_lat, w_small, w_rest)
    qt, kp, vt = _prep(lat, small, rest, pos, g_q[None, :], g_kv[None, :],
                       w_uq_p, w_ukv_b, b_f, inv_f, batch, seq)
    o = _attn(qt, kp, vt, batch, seq)
    return _out(o, rest, x2, w_out_b, g_post[None, :])


def kernel(x, positions, g_pre, w_in, g_q_latent, w_uq, g_kv_latent, w_ukv,
           b_forget, w_out, g_post):
    batch, seq, d = x.shape
    depth = g_pre.shape[0]
    x2 = x.reshape(batch * seq, d)
    pos = positions.reshape(batch * seq, 1)
    freqs = ROPE_THETA ** (-jnp.arange(0, ROPE_DIM, 2, dtype=F32) / ROPE_DIM)
    inv_f = jnp.zeros((1, LANES), F32)
    inv_f = inv_f.at[0, :ROPE_HALF].set(freqs).at[0, ROPE_HALF:ROPE_DIM].set(freqs)
    for l in range(depth):
        x2 = _layer(x2, pos, inv_f, g_pre[l], w_in[l], g_q_latent[l], w_uq[l],
                    g_kv_latent[l], w_ukv[l], b_forget[l], w_out[l], g_post[l],
                    batch, seq)
    return x2.reshape(batch, seq, d)
```

```python
import functools
import math

import jax
import jax.numpy as jnp
from jax import lax
from jax.experimental import pallas as pl
from jax.experimental.pallas import tpu as pltpu

F32 = jnp.float32
BF16 = jnp.bfloat16

D_MODEL = 2048
N_HEADS = 8
HEAD_DIM = 128
NOPE_DIM = 128
ROPE_DIM = 64
ROPE_HALF = ROPE_DIM // 2
QK_DIM = NOPE_DIM + ROPE_DIM
Q_RANK = 768
KV_RANK = 512
LAT_DIM = Q_RANK + KV_RANK
BRANCH_WIDTH = N_HEADS * HEAD_DIM
D_MIX = 2 * BRANCH_WIDTH
ROPE_THETA = 10000.0
NORM_EPS = 1e-6
LOG2E = math.log2(math.e)

LANES = 128
FEAT = 256
SMALL_W = 2 * LANES
REST_W = 5 * BRANCH_WIDTH

MLA_QSCALE = QK_DIM ** -0.5 * LOG2E
FOX_QSCALE = HEAD_DIM ** -0.5 * LOG2E

PROJ_TM = 512
PREP_TM = 512
ATT_T = 1024
ATT_BLK = 256
OUT_TM = 512

VMEM_LIMIT = 56 * 1024 * 1024


def _rms(x, g):
    return x * lax.rsqrt(jnp.mean(x * x, axis=-1, keepdims=True) + NORM_EPS) * g


def _proj_kernel(x_ref, g_ref, wlat_ref, wsm_ref, wrest_ref,
                 lat_ref, sm_ref, rest_ref, h_scr):
    j = pl.program_id(1)

    @pl.when(j == 0)
    def _():
        hb = _rms(x_ref[...], g_ref[...]).astype(BF16)
        h_scr[...] = hb
        lat_ref[...] = jnp.dot(hb, wlat_ref[...],
                               preferred_element_type=F32).astype(BF16)
        sm_ref[...] = jnp.dot(hb, wsm_ref[...], preferred_element_type=F32)

    acc = jnp.dot(h_scr[...], wrest_ref[...], preferred_element_type=F32)
    scale = jnp.where(j == 1, FOX_QSCALE, 1.0).astype(F32)
    rest_ref[...] = (acc * scale).astype(BF16)


def _proj(x2, g_pre, w_lat, w_small, w_rest):
    t = x2.shape[0]
    n_rest = REST_W // BRANCH_WIDTH
    return pl.pallas_call(
        _proj_kernel,
        grid=(t // PROJ_TM, n_rest),
        in_specs=[
            pl.BlockSpec((PROJ_TM, D_MODEL), lambda i, j: (i, 0)),
            pl.BlockSpec((1, D_MODEL), lambda i, j: (0, 0)),
            pl.BlockSpec((D_MODEL, LAT_DIM), lambda i, j: (0, 0)),
            pl.BlockSpec((D_MODEL, SMALL_W), lambda i, j: (0, 0)),
            pl.BlockSpec((D_MODEL, BRANCH_WIDTH), lambda i, j: (0, j)),
        ],
        out_specs=[
            pl.BlockSpec((PROJ_TM, LAT_DIM), lambda i, j: (i, 0)),
            pl.BlockSpec((PROJ_TM, SMALL_W), lambda i, j: (i, 0)),
            pl.BlockSpec((PROJ_TM, BRANCH_WIDTH), lambda i, j: (i, j)),
        ],
        out_shape=[
            jax.ShapeDtypeStruct((t, LAT_DIM), BF16),
            jax.ShapeDtypeStruct((t, SMALL_W), F32),
            jax.ShapeDtypeStruct((t, REST_W), BF16),
        ],
        scratch_shapes=[pltpu.VMEM((PROJ_TM, D_MODEL), BF16)],
        compiler_params=pltpu.CompilerParams(
            dimension_semantics=("arbitrary", "arbitrary"),
            vmem_limit_bytes=VMEM_LIMIT),
        name="proj",
    )(x2, g_pre, w_lat, w_small, w_rest)


def _split3(x):
    hi = x.astype(BF16).astype(F32)
    r = x - hi
    mid = r.astype(BF16).astype(F32)
    lo = (r - mid).astype(BF16).astype(F32)
    return hi, mid, lo


def _prep_kernel(lat_ref, sm_ref, fq_ref, fk_ref, fv_ref, pos_ref,
                 gq_ref, gkv_ref, wuq_ref, wukv_ref, bf_ref, invf_ref,
                 qt_ref, kp_ref, vt_ref, carry_scr):
    tm = lat_ref.shape[0]

    def put_t(ref, r0, x):
        ref[0, r0:r0 + x.shape[1], :] = x.astype(F32).T.astype(BF16)

    @pl.when(pl.program_id(1) == 0)
    def _():
        carry_scr[...] = jnp.zeros_like(carry_scr)

    lat = lat_ref[...].astype(F32)
    qn = _rms(lat[:, :Q_RANK], gq_ref[...]).astype(BF16)
    kvn = _rms(lat[:, Q_RANK:], gkv_ref[...]).astype(BF16)
    q = jnp.dot(qn, wuq_ref[...], preferred_element_type=F32)
    kv = jnp.dot(kvn, wukv_ref[...], preferred_element_type=F32)
    sm = sm_ref[...]

    lane = lax.broadcasted_iota(jnp.int32, (tm, LANES), 1)
    ang = pos_ref[...].astype(F32) * invf_ref[...]
    cosv = jnp.cos(ang)
    sinv = jnp.sin(ang)
    sin_hi = jnp.where(lane < ROPE_HALF, 0.0,
                       jnp.where(lane < ROPE_DIM, sinv, 0.0))
    sin_lo = jnp.where(lane < ROPE_HALF, -sinv, 0.0)

    def rope(r):
        return (r * cosv
                + pltpu.roll(r, ROPE_HALF, 1) * sin_hi
                + pltpu.roll(r, LANES - ROPE_HALF, 1) * sin_lo)

    k_rope = rope(sm[:, :LANES]).astype(BF16)
    for h in range(N_HEADS):
        c0 = h * FEAT
        put_t(qt_ref, c0, q[:, c0:c0 + LANES] * MLA_QSCALE)
        put_t(qt_ref, c0 + LANES, rope(q[:, c0 + LANES:c0 + FEAT]) * MLA_QSCALE)
        kp_ref[:, c0:c0 + LANES] = kv[:, c0:c0 + LANES].astype(BF16)
        kp_ref[:, c0 + LANES:c0 + FEAT] = k_rope
        put_t(vt_ref, h * HEAD_DIM, kv[:, c0 + LANES:c0 + FEAT])

    fl = sm[:, LANES:] + bf_ref[...]
    logf = jnp.minimum(fl, 0.0) - jnp.log1p(jnp.exp(-jnp.abs(fl)))
    row = lax.broadcasted_iota(jnp.int32, (tm, tm), 0)
    col = lax.broadcasted_iota(jnp.int32, (tm, tm), 1)
    tri = jnp.where(row >= col, 1.0, 0.0).astype(BF16)
    c = carry_scr[...]
    for piece in _split3(logf):
        c = c + jnp.dot(tri, piece.astype(BF16), preferred_element_type=F32)
    carry_scr[...] = c[tm - 1:tm, :]
    c2 = c * LOG2E

    for h in range(N_HEADS):
        c0 = (N_HEADS + h) * FEAT
        hi, mid, lo = _split3(jnp.broadcast_to(c2[:, h:h + 1], (tm, LANES)))
        cq = jnp.where(lane == 0, hi, jnp.where(lane == 1, mid, jnp.where(
            lane == 2, lo, jnp.where(lane < 6, 1.0, 0.0))))
        ck = jnp.where(lane < 3, 1.0, jnp.where(lane == 3, -hi, jnp.where(
            lane == 4, -mid, jnp.where(lane == 5, -lo, 0.0))))
        s0 = h * HEAD_DIM
        put_t(qt_ref, c0, fq_ref[:, s0:s0 + HEAD_DIM])
        put_t(qt_ref, c0 + LANES, cq)
        kp_ref[:, c0:c0 + LANES] = fk_ref[:, s0:s0 + HEAD_DIM]
        kp_ref[:, c0 + LANES:c0 + FEAT] = ck.astype(BF16)
        put_t(vt_ref, BRANCH_WIDTH + s0, fv_ref[:, s0:s0 + HEAD_DIM])


def _prep(lat, small, rest, pos, g_q, g_kv, w_uq, w_ukv, b_f, inv_f, batch, seq):
    t = lat.shape[0]
    ns = seq // PREP_TM
    per = ATT_T // PREP_TM
    nt = t // ATT_T
    row = lambda b, i: (b * ns + i, 0)
    const = lambda b, i: (0, 0)
    tile_t = lambda b, i: ((b * ns + i) // per, 0, (b * ns + i) % per)
    return pl.pallas_call(
        _prep_kernel,
        grid=(batch, ns),
        in_specs=[
            pl.BlockSpec((PREP_TM, LAT_DIM), row),
            pl.BlockSpec((PREP_TM, SMALL_W), row),
            pl.BlockSpec((PREP_TM, BRANCH_WIDTH), lambda b, i: (b * ns + i, 1)),
            pl.BlockSpec((PREP_TM, BRANCH_WIDTH), lambda b, i: (b * ns + i, 2)),
            pl.BlockSpec((PREP_TM, BRANCH_WIDTH), lambda b, i: (b * ns + i, 3)),
            pl.BlockSpec((PREP_TM, 1), row),
            pl.BlockSpec((1, Q_RANK), const),
            pl.BlockSpec((1, KV_RANK), const),
            pl.BlockSpec((Q_RANK, N_HEADS * FEAT), const),
            pl.BlockSpec((KV_RANK, N_HEADS * FEAT), const),
            pl.BlockSpec((1, LANES), const),
            pl.BlockSpec((1, LANES), const),
        ],
        out_specs=[
            pl.BlockSpec((1, 2 * N_HEADS * FEAT, PREP_TM), tile_t),
            pl.BlockSpec((PREP_TM, 2 * N_HEADS * FEAT), row),
            pl.BlockSpec((1, D_MIX, PREP_TM), tile_t),
        ],
        out_shape=[
            jax.ShapeDtypeStruct((nt, 2 * N_HEADS * FEAT, ATT_T), BF16),
            jax.ShapeDtypeStruct((t, 2 * N_HEADS * FEAT), BF16),
            jax.ShapeDtypeStruct((nt, D_MIX, ATT_T), BF16),
        ],
        scratch_shapes=[pltpu.VMEM((1, LANES), F32)],
        compiler_params=pltpu.CompilerParams(
            dimension_semantics=("arbitrary", "arbitrary"),
            vmem_limit_bytes=VMEM_LIMIT),
        name="prep",
    )(lat, small, rest, rest, rest, pos, g_q, g_kv, w_uq, w_ukv, b_f, inv_f)


def _attn_kernel(qt_ref, k_ref, vt_ref, o_ref, s0, s1, cm0, cm1, m_scr, l_scr, acc_scr):
    t = ATT_T
    nq = qt_ref.shape[0]
    nsteps = nq * (nq + 1) // 2
    s_buf, cm_buf = (s0, s1), (cm0, cm1)

    def advance(ij):
        i, j = ij
        wrap = j >= i
        return jnp.where(wrap, i + 1, i), jnp.where(wrap, 0, j + 1)

    nblk = t // ATT_BLK

    def softmax_update(s, cm, j, cols):
        m_old = jnp.where(j == 0, -jnp.inf, m_scr[:, cols])
        m_new = jnp.maximum(m_old, cm)
        alpha = jnp.exp2(m_old - m_new)
        p = jnp.exp2(s - m_new)
        l_new = alpha * l_scr[:, cols] + jnp.sum(p, axis=0, keepdims=True)
        return p.astype(BF16), l_new, alpha, m_new

    def stage_a(ij, slot, diagonal):
        i, j = ij
        i = jnp.minimum(i, nq - 1)
        koff = pl.multiple_of(j * t, t)
        if not diagonal:
            k = k_ref[pl.ds(koff, t), :]
            s = jnp.dot(k, qt_ref[i], preferred_element_type=F32)
            s_buf[slot][...] = s
            cm_buf[slot][...] = jnp.max(s, axis=0, keepdims=True)
            return
        for qb in range(nblk):
            rows, c0 = (qb + 1) * ATT_BLK, qb * ATT_BLK
            s_buf[slot][0:rows, c0:c0 + ATT_BLK] = jnp.dot(
                k_ref[pl.ds(koff, rows), :], qt_ref[i, :, c0:c0 + ATT_BLK],
                preferred_element_type=F32)

    def stage_bc(ij, slot, diagonal):
        i, j = ij
        if not diagonal:
            full = slice(None)
            p, l_new, alpha, m_new = softmax_update(
                s_buf[slot][...], cm_buf[slot][...], j, full)
            pv = jnp.dot(vt_ref[j], p, preferred_element_type=F32)
            m_scr[...] = m_new
            l_scr[...] = l_new
            acc_scr[...] = alpha * acc_scr[...] + pv
            return
        for qb in range(nblk):
            rows, c0 = (qb + 1) * ATT_BLK, qb * ATT_BLK
            cols = slice(c0, c0 + ATT_BLK)
            s = s_buf[slot][0:rows, cols]
            key = lax.broadcasted_iota(jnp.int32, (rows, ATT_BLK), 0)
            qry = lax.broadcasted_iota(jnp.int32, (rows, ATT_BLK), 1) + c0
            s = jnp.where(key <= qry, s, -jnp.inf)
            p, l_new, alpha, _ = softmax_update(
                s, jnp.max(s, axis=0, keepdims=True), j, cols)
            pv = jnp.dot(vt_ref[j, :, 0:rows], p, preferred_element_type=F32)
            acc = alpha * acc_scr[:, cols] + pv
            o_ref[pl.ds(pl.multiple_of(i * t + c0, ATT_BLK), ATT_BLK), :] = (
                acc / l_new).T.astype(BF16)

    def iteration(slot_a, ija, ijb):
        a_diag = ija[1] == ija[0]
        b_diag = ijb[1] == ijb[0]

        @pl.when(b_diag)
        def _():
            stage_a(ija, slot_a, False)
            stage_bc(ijb, 1 - slot_a, True)

        @pl.when(a_diag)
        def _():
            stage_a(ija, slot_a, True)
            stage_bc(ijb, 1 - slot_a, False)

        @pl.when(jnp.logical_not(jnp.logical_or(a_diag, b_diag)))
        def _():
            stage_a(ija, slot_a, False)
            stage_bc(ijb, 1 - slot_a, False)

    zero = jnp.int32(0)
    m_scr[...] = jnp.full(m_scr.shape, -jnp.inf, F32)
    l_scr[...] = jnp.zeros(l_scr.shape, F32)
    acc_scr[...] = jnp.zeros(acc_scr.shape, F32)
    stage_a((zero, zero), 0, True)

    def pair(_, carry):
        ija, ijb = carry
        iteration(1, ija, ijb)
        ija2 = advance(ija)
        iteration(0, ija2, ija)
        return advance(ija2), ija2

    lax.fori_loop(0, nsteps // 2, pair, (advance((zero, zero)), (zero, zero)))


def _attn(qt, kp, vt, batch, seq):
    t = kp.shape[0]
    nq = seq // ATT_T
    assert (nq * (nq + 1) // 2) % 2 == 0
    nh = 2 * N_HEADS
    row = pltpu.VMEM((1, ATT_T), F32)
    return pl.pallas_call(
        _attn_kernel,
        grid=(batch, nh),
        in_specs=[
            pl.BlockSpec((nq, FEAT, ATT_T), lambda b, h: (b, h, 0)),
            pl.BlockSpec((seq, FEAT), lambda b, h: (b, h)),
            pl.BlockSpec((nq, HEAD_DIM, ATT_T), lambda b, h: (b, h, 0)),
        ],
        out_specs=pl.BlockSpec((seq, HEAD_DIM), lambda b, h: (b, h)),
        out_shape=jax.ShapeDtypeStruct((t, D_MIX), BF16),
        scratch_shapes=[
            pltpu.VMEM((ATT_T, ATT_T), F32), pltpu.VMEM((ATT_T, ATT_T), F32),
            row, row, row, row, pltpu.VMEM((HEAD_DIM, ATT_T), F32),
        ],
        compiler_params=pltpu.CompilerParams(
            dimension_semantics=("arbitrary", "arbitrary"),
            vmem_limit_bytes=VMEM_LIMIT),
        name="attn",
    )(qt, kp, vt)


def _silu(g):
    return g / (1.0 + jnp.exp(-g))


def _out_kernel(o_ref, gm_ref, gf_ref, x_ref, w_ref, g_ref, y_ref):
    o = o_ref[...].astype(F32)
    om = (o[:, :BRANCH_WIDTH] * _silu(gm_ref[...].astype(F32))).astype(BF16)
    of = (o[:, BRANCH_WIDTH:] * _silu(gf_ref[...].astype(F32))).astype(BF16)
    y = (jnp.dot(om, w_ref[:BRANCH_WIDTH, :], preferred_element_type=F32)
         + jnp.dot(of, w_ref[BRANCH_WIDTH:, :], preferred_element_type=F32))
    y_ref[...] = x_ref[...] + _rms(y, g_ref[...])


def _out(o, rest, x2, w_out, g_post):
    t = o.shape[0]
    return pl.pallas_call(
        _out_kernel,
        grid=(t // OUT_TM,),
        in_specs=[
            pl.BlockSpec((OUT_TM, D_MIX), lambda i: (i, 0)),
            pl.BlockSpec((OUT_TM, BRANCH_WIDTH), lambda i: (i, 0)),
            pl.BlockSpec((OUT_TM, BRANCH_WIDTH), lambda i: (i, 4)),
            pl.BlockSpec((OUT_TM, D_MODEL), lambda i: (i, 0)),
            pl.BlockSpec((D_MIX, D_MODEL), lambda i: (0, 0)),
            pl.BlockSpec((1, D_MODEL), lambda i: (0, 0)),
        ],
        out_specs=pl.BlockSpec((OUT_TM, D_MODEL), lambda i: (i, 0)),
        out_shape=jax.ShapeDtypeStruct((t, D_MODEL), F32),
        compiler_params=pltpu.CompilerParams(
            dimension_semantics=("arbitrary",),
            vmem_limit_bytes=VMEM_LIMIT),
        name="out",
    )(o, rest, rest, x2, w_out, g_post)


def _layer(x2, pos, inv_f, g_pre, w_in, g_q, w_uq, g_kv, w_ukv, b_forget,
           w_out, g_post, batch, seq):
    o_kr = LAT_DIM
    o_gm = o_kr + ROPE_DIM
    o_fq = o_gm + BRANCH_WIDTH
    o_fl = o_fq + 3 * BRANCH_WIDTH
    o_gf = o_fl + N_HEADS
    w_lat = w_in[:, :LAT_DIM].astype(BF16)
    w_small = jnp.zeros((D_MODEL, SMALL_W), F32)
    w_small = w_small.at[:, :ROPE_DIM].set(w_in[:, o_kr:o_gm])
    w_small = w_small.at[:, LANES:LANES + N_HEADS].set(w_in[:, o_fl:o_gf])
    w_small = w_small.astype(BF16)
    w_rest = jnp.concatenate([w_in[:, o_gm:o_fl], w_in[:, o_gf:]], axis=1).astype(BF16)
    w_uq_p = jnp.pad(w_uq.reshape(Q_RANK, N_HEADS, QK_DIM),
                     ((0, 0), (0, 0), (0, FEAT - QK_DIM))).reshape(
                         Q_RANK, N_HEADS * FEAT).astype(BF16)
    w_ukv_b = w_ukv.astype(BF16)
    w_out_b = w_out.astype(BF16)
    b_f = jnp.zeros((1, LANES), F32).at[0, :N_HEADS].set(b_forget)

    lat, small, rest = _proj(x2, g_pre[None, :], w_lat, w_small, w_rest)
    qt, kp, vt = _prep(lat, small, rest, pos, g_q[None, :], g_kv[None, :],
                       w_uq_p, w_ukv_b, b_f, inv_f, batch, seq)
    o = _attn(qt, kp, vt, batch, seq)
    return _out(o, rest, x2, w_out_b, g_post[None, :])


def kernel(x, positions, g_pre, w_in, g_q_latent, w_uq, g_kv_latent, w_ukv,
           b_forget, w_out, g_post):
    batch, seq, d = x.shape
    depth = g_pre.shape[0]
    x2 = x.reshape(batch * seq, d)
    pos = positions.reshape(batch * seq, 1)
    freqs = ROPE_THETA ** (-jnp.arange(0, ROPE_DIM, 2, dtype=F32) / ROPE_DIM)
    inv_f = jnp.zeros((1, LANES), F32)
    inv_f = inv_f.at[0, :ROPE_HALF].set(freqs).at[0, ROPE_HALF:ROPE_DIM].set(freqs)
    for l in range(depth):
        x2 = _layer(x2, pos, inv_f, g_pre[l], w_in[l], g_q_latent[l], w_uq[l],
                    g_kv_latent[l], w_ukv[l], b_forget[l], w_out[l], g_post[l],
                    batch, seq)
    return x2.reshape(batch, seq, d)
```

```python
import functools
import math

import jax
import jax.numpy as jnp
from jax import lax
from jax.experimental import pallas as pl
from jax.experimental.pallas import tpu as pltpu

F32 = jnp.float32
BF16 = jnp.bfloat16

D_MODEL = 2048
N_HEADS = 8
HEAD_DIM = 128
NOPE_DIM = 128
ROPE_DIM = 64
ROPE_HALF = ROPE_DIM // 2
QK_DIM = NOPE_DIM + ROPE_DIM
Q_RANK = 768
KV_RANK = 512
LAT_DIM = Q_RANK + KV_RANK
BRANCH_WIDTH = N_HEADS * HEAD_DIM
D_MIX = 2 * BRANCH_WIDTH
ROPE_THETA = 10000.0
NORM_EPS = 1e-6
LOG2E = math.log2(math.e)

LANES = 128
FEAT = 256
SMALL_W = 2 * LANES
REST_W = 5 * BRANCH_WIDTH

MLA_QSCALE = QK_DIM ** -0.5 * LOG2E
FOX_QSCALE = HEAD_DIM ** -0.5 * LOG2E

PROJ_TM = 512
PREP_TM = 512
ATT_T = 1024
ATT_BLK = 256
ATT_HB = 2
OUT_TM = 512

VMEM_LIMIT = 56 * 1024 * 1024


def _rms(x, g):
    return x * lax.rsqrt(jnp.mean(x * x, axis=-1, keepdims=True) + NORM_EPS) * g


def _proj_kernel(x_ref, g_ref, wlat_ref, wsm_ref, wrest_ref,
                 lat_ref, sm_ref, rest_ref, h_scr):
    j = pl.program_id(1)

    @pl.when(j == 0)
    def _():
        hb = _rms(x_ref[...], g_ref[...]).astype(BF16)
        h_scr[...] = hb
        lat_ref[...] = jnp.dot(hb, wlat_ref[...],
                               preferred_element_type=F32).astype(BF16)
        sm_ref[...] = jnp.dot(hb, wsm_ref[...], preferred_element_type=F32)

    acc = jnp.dot(h_scr[...], wrest_ref[...], preferred_element_type=F32)
    scale = jnp.where(j == 1, FOX_QSCALE, 1.0).astype(F32)
    rest_ref[...] = (acc * scale).astype(BF16)


def _proj(x2, g_pre, w_lat, w_small, w_rest):
    t = x2.shape[0]
    n_rest = REST_W // BRANCH_WIDTH
    return pl.pallas_call(
        _proj_kernel,
        grid=(t // PROJ_TM, n_rest),
        in_specs=[
            pl.BlockSpec((PROJ_TM, D_MODEL), lambda i, j: (i, 0)),
            pl.BlockSpec((1, D_MODEL), lambda i, j: (0, 0)),
            pl.BlockSpec((D_MODEL, LAT_DIM), lambda i, j: (0, 0)),
            pl.BlockSpec((D_MODEL, SMALL_W), lambda i, j: (0, 0)),
            pl.BlockSpec((D_MODEL, BRANCH_WIDTH), lambda i, j: (0, j)),
        ],
        out_specs=[
            pl.BlockSpec((PROJ_TM, LAT_DIM), lambda i, j: (i, 0)),
            pl.BlockSpec((PROJ_TM, SMALL_W), lambda i, j: (i, 0)),
            pl.BlockSpec((PROJ_TM, BRANCH_WIDTH), lambda i, j: (i, j)),
        ],
        out_shape=[
            jax.ShapeDtypeStruct((t, LAT_DIM), BF16),
            jax.ShapeDtypeStruct((t, SMALL_W), F32),
            jax.ShapeDtypeStruct((t, REST_W), BF16),
        ],
        scratch_shapes=[pltpu.VMEM((PROJ_TM, D_MODEL), BF16)],
        compiler_params=pltpu.CompilerParams(
            dimension_semantics=("arbitrary", "arbitrary"),
            vmem_limit_bytes=VMEM_LIMIT),
        name="proj",
    )(x2, g_pre, w_lat, w_small, w_rest)


def _split3(x):
    hi = x.astype(BF16).astype(F32)
    r = x - hi
    mid = r.astype(BF16).astype(F32)
    lo = (r - mid).astype(BF16).astype(F32)
    return hi, mid, lo


def _prep_kernel(lat_ref, sm_ref, fq_ref, fk_ref, fv_ref, pos_ref,
                 gq_ref, gkv_ref, wuq_ref, wukv_ref, bf_ref, invf_ref,
                 qt_ref, kp_ref, vt_ref, carry_scr):
    tm = lat_ref.shape[0]

    def put_t(ref, r0, x):
        ref[0, r0:r0 + x.shape[1], :] = x.astype(F32).T.astype(BF16)

    @pl.when(pl.program_id(1) == 0)
    def _():
        carry_scr[...] = jnp.zeros_like(carry_scr)

    lat = lat_ref[...].astype(F32)
    qn = _rms(lat[:, :Q_RANK], gq_ref[...]).astype(BF16)
    kvn = _rms(lat[:, Q_RANK:], gkv_ref[...]).astype(BF16)
    q = jnp.dot(qn, wuq_ref[...], preferred_element_type=F32)
    kv = jnp.dot(kvn, wukv_ref[...], preferred_element_type=F32)
    sm = sm_ref[...]

    lane = lax.broadcasted_iota(jnp.int32, (tm, LANES), 1)
    ang = pos_ref[...].astype(F32) * invf_ref[...]
    cosv = jnp.cos(ang)
    sinv = jnp.sin(ang)
    sin_hi = jnp.where(lane < ROPE_HALF, 0.0,
                       jnp.where(lane < ROPE_DIM, sinv, 0.0))
    sin_lo = jnp.where(lane < ROPE_HALF, -sinv, 0.0)

    def rope(r):
        return (r * cosv
                + pltpu.roll(r, ROPE_HALF, 1) * sin_hi
                + pltpu.roll(r, LANES - ROPE_HALF, 1) * sin_lo)

    k_rope = rope(sm[:, :LANES]).astype(BF16)
    for h in range(N_HEADS):
        c0 = h * FEAT
        put_t(qt_ref, c0, q[:, c0:c0 + LANES] * MLA_QSCALE)
        put_t(qt_ref, c0 + LANES, rope(q[:, c0 + LANES:c0 + FEAT]) * MLA_QSCALE)
        kp_ref[:, c0:c0 + LANES] = kv[:, c0:c0 + LANES].astype(BF16)
        kp_ref[:, c0 + LANES:c0 + FEAT] = k_rope
        put_t(vt_ref, h * HEAD_DIM, kv[:, c0 + LANES:c0 + FEAT])

    fl = sm[:, LANES:] + bf_ref[...]
    logf = jnp.minimum(fl, 0.0) - jnp.log1p(jnp.exp(-jnp.abs(fl)))
    row = lax.broadcasted_iota(jnp.int32, (tm, tm), 0)
    col = lax.broadcasted_iota(jnp.int32, (tm, tm), 1)
    tri = jnp.where(row >= col, 1.0, 0.0).astype(BF16)
    c = carry_scr[...]
    for piece in _split3(logf):
        c = c + jnp.dot(tri, piece.astype(BF16), preferred_element_type=F32)
    carry_scr[...] = c[tm - 1:tm, :]
    c2 = c * LOG2E

    for h in range(N_HEADS):
        c0 = (N_HEADS + h) * FEAT
        hi, mid, lo = _split3(jnp.broadcast_to(c2[:, h:h + 1], (tm, LANES)))
        cq = jnp.where(lane == 0, hi, jnp.where(lane == 1, mid, jnp.where(
            lane == 2, lo, jnp.where(lane < 6, 1.0, 0.0))))
        ck = jnp.where(lane < 3, 1.0, jnp.where(lane == 3, -hi, jnp.where(
            lane == 4, -mid, jnp.where(lane == 5, -lo, 0.0))))
        s0 = h * HEAD_DIM
        put_t(qt_ref, c0, fq_ref[:, s0:s0 + HEAD_DIM])
        put_t(qt_ref, c0 + LANES, cq)
        kp_ref[:, c0:c0 + LANES] = fk_ref[:, s0:s0 + HEAD_DIM]
        kp_ref[:, c0 + LANES:c0 + FEAT] = ck.astype(BF16)
        put_t(vt_ref, BRANCH_WIDTH + s0, fv_ref[:, s0:s0 + HEAD_DIM])


def _prep(lat, small, rest, pos, g_q, g_kv, w_uq, w_ukv, b_f, inv_f, batch, seq):
    t = lat.shape[0]
    ns = seq // PREP_TM
    per = ATT_T // PREP_TM
    nt = t // ATT_T
    row = lambda b, i: (b * ns + i, 0)
    const = lambda b, i: (0, 0)
    tile_t = lambda b, i: ((b * ns + i) // per, 0, (b * ns + i) % per)
    return pl.pallas_call(
        _prep_kernel,
        grid=(batch, ns),
        in_specs=[
            pl.BlockSpec((PREP_TM, LAT_DIM), row),
            pl.BlockSpec((PREP_TM, SMALL_W), row),
            pl.BlockSpec((PREP_TM, BRANCH_WIDTH), lambda b, i: (b * ns + i, 1)),
            pl.BlockSpec((PREP_TM, BRANCH_WIDTH), lambda b, i: (b * ns + i, 2)),
            pl.BlockSpec((PREP_TM, BRANCH_WIDTH), lambda b, i: (b * ns + i, 3)),
            pl.BlockSpec((PREP_TM, 1), row),
            pl.BlockSpec((1, Q_RANK), const),
            pl.BlockSpec((1, KV_RANK), const),
            pl.BlockSpec((Q_RANK, N_HEADS * FEAT), const),
            pl.BlockSpec((KV_RANK, N_HEADS * FEAT), const),
            pl.BlockSpec((1, LANES), const),
            pl.BlockSpec((1, LANES), const),
        ],
        out_specs=[
            pl.BlockSpec((1, 2 * N_HEADS * FEAT, PREP_TM), tile_t),
            pl.BlockSpec((PREP_TM, 2 * N_HEADS * FEAT), row),
            pl.BlockSpec((1, D_MIX, PREP_TM), tile_t),
        ],
        out_shape=[
            jax.ShapeDtypeStruct((nt, 2 * N_HEADS * FEAT, ATT_T), BF16),
            jax.ShapeDtypeStruct((t, 2 * N_HEADS * FEAT), BF16),
            jax.ShapeDtypeStruct((nt, D_MIX, ATT_T), BF16),
        ],
        scratch_shapes=[pltpu.VMEM((1, LANES), F32)],
        compiler_params=pltpu.CompilerParams(
            dimension_semantics=("arbitrary", "arbitrary"),
            vmem_limit_bytes=VMEM_LIMIT),
        name="prep",
    )(lat, small, rest, rest, rest, pos, g_q, g_kv, w_uq, w_ukv, b_f, inv_f)


def _attn_kernel(qt_ref, k_ref, vt_ref, o_ref, s0, s1, cm0, cm1, m_scr, l_scr, acc_scr):
    t = ATT_T
    nq = qt_ref.shape[0]
    nsteps = nq * (nq + 1) // 2
    s_buf, cm_buf = (s0, s1), (cm0, cm1)

    def advance(ij):
        i, j = ij
        wrap = j >= i
        return jnp.where(wrap, i + 1, i), jnp.where(wrap, 0, j + 1)

    nblk = t // ATT_BLK

    def softmax_update(s, cm, j, hh, cols):
        m_old = jnp.where(j == 0, -jnp.inf, m_scr[hh, :, cols])
        m_new = jnp.maximum(m_old, cm)
        alpha = jnp.exp2(m_old - m_new)
        p = jnp.exp2(s - m_new)
        l_new = alpha * l_scr[hh, :, cols] + jnp.sum(p, axis=0, keepdims=True)
        return p.astype(BF16), l_new, alpha, m_new

    def stage_a(ij, slot, diagonal, hh):
        i, j = ij
        i = jnp.minimum(i, nq - 1)
        koff = pl.multiple_of(j * t, t)
        f0 = hh * FEAT
        if not diagonal:
            k = k_ref[pl.ds(koff, t), f0:f0 + FEAT]
            s = jnp.dot(k, qt_ref[i, f0:f0 + FEAT, :],
                        preferred_element_type=F32)
            s_buf[slot][hh] = s
            cm_buf[slot][hh] = jnp.max(s, axis=0, keepdims=True)
            return
        for qb in range(nblk):
            rows, c0 = (qb + 1) * ATT_BLK, qb * ATT_BLK
            s_buf[slot][hh, 0:rows, c0:c0 + ATT_BLK] = jnp.dot(
                k_ref[pl.ds(koff, rows), f0:f0 + FEAT],
                qt_ref[i, f0:f0 + FEAT, c0:c0 + ATT_BLK],
                preferred_element_type=F32)

    def stage_bc(ij, slot, diagonal, hh):
        i, j = ij
        d0 = hh * HEAD_DIM
        if not diagonal:
            full = slice(None)
            p, l_new, alpha, m_new = softmax_update(
                s_buf[slot][hh], cm_buf[slot][hh], j, hh, full)
            pv = jnp.dot(vt_ref[j, d0:d0 + HEAD_DIM, :], p, preferred_element_type=F32)
            m_scr[hh] = m_new
            l_scr[hh] = l_new
            acc_scr[hh] = alpha * acc_scr[hh] + pv
            return
        for qb in range(nblk):
            rows, c0 = (qb + 1) * ATT_BLK, qb * ATT_BLK
            cols = slice(c0, c0 + ATT_BLK)
            s = s_buf[slot][hh, 0:rows, cols]
            key = lax.broadcasted_iota(jnp.int32, (rows, ATT_BLK), 0)
            qry = lax.broadcasted_iota(jnp.int32, (rows, ATT_BLK), 1) + c0
            s = jnp.where(key <= qry, s, -jnp.inf)
            p, l_new, alpha, _ = softmax_update(
                s, jnp.max(s, axis=0, keepdims=True), j, hh, cols)
            pv = jnp.dot(vt_ref[j, d0:d0 + HEAD_DIM, 0:rows], p,
                         preferred_element_type=F32)
            acc = alpha * acc_scr[hh, :, cols] + pv
            o_ref[pl.ds(pl.multiple_of(i * t + c0, ATT_BLK), ATT_BLK),
                  d0:d0 + HEAD_DIM] = (acc / l_new).T.astype(BF16)

    def iteration(slot_a, ija, ijb):
        a_diag = ija[1] == ija[0]
        b_diag = ijb[1] == ijb[0]

        def run(a_is_diag, b_is_diag):
            for hh in range(ATT_HB):
                stage_a(ija, slot_a, a_is_diag, hh)
                stage_bc(ijb, 1 - slot_a, b_is_diag, hh)

        pl.when(b_diag)(lambda: run(False, True))
        pl.when(a_diag)(lambda: run(True, False))
        pl.when(jnp.logical_not(jnp.logical_or(a_diag, b_diag)))(
            lambda: run(False, False))

    zero = jnp.int32(0)
    m_scr[...] = jnp.full(m_scr.shape, -jnp.inf, F32)
    l_scr[...] = jnp.zeros(l_scr.shape, F32)
    acc_scr[...] = jnp.zeros(acc_scr.shape, F32)
    for hh in range(ATT_HB):
        stage_a((zero, zero), 0, True, hh)

    def pair(_, carry):
        ija, ijb = carry
        iteration(1, ija, ijb)
        ija2 = advance(ija)
        iteration(0, ija2, ija)
        return advance(ija2), ija2

    lax.fori_loop(0, nsteps // 2, pair, (advance((zero, zero)), (zero, zero)))


def _attn(qt, kp, vt, batch, seq):
    t = kp.shape[0]
    nq = seq // ATT_T
    assert (nq * (nq + 1) // 2) % 2 == 0
    nh = 2 * N_HEADS // ATT_HB
    row = pltpu.VMEM((ATT_HB, 1, ATT_T), F32)
    tile = pltpu.VMEM((ATT_HB, ATT_T, ATT_T), F32)
    return pl.pallas_call(
        _attn_kernel,
        grid=(batch, nh),
        in_specs=[
            pl.BlockSpec((nq, ATT_HB * FEAT, ATT_T), lambda b, h: (b, h, 0)),
            pl.BlockSpec((seq, ATT_HB * FEAT), lambda b, h: (b, h)),
            pl.BlockSpec((nq, ATT_HB * HEAD_DIM, ATT_T), lambda b, h: (b, h, 0)),
        ],
        out_specs=pl.BlockSpec((seq, ATT_HB * HEAD_DIM), lambda b, h: (b, h)),
        out_shape=jax.ShapeDtypeStruct((t, D_MIX), BF16),
        scratch_shapes=[
            tile, tile, row, row, row, row,
            pltpu.VMEM((ATT_HB, HEAD_DIM, ATT_T), F32),
        ],
        compiler_params=pltpu.CompilerParams(
            dimension_semantics=("arbitrary", "arbitrary"),
            vmem_limit_bytes=VMEM_LIMIT),
        name="attn",
    )(qt, kp, vt)


def _silu(g):
    return g / (1.0 + jnp.exp(-g))


def _out_kernel(o_ref, gm_ref, gf_ref, x_ref, w_ref, g_ref, y_ref):
    o = o_ref[...].astype(F32)
    om = (o[:, :BRANCH_WIDTH] * _silu(gm_ref[...].astype(F32))).astype(BF16)
    of = (o[:, BRANCH_WIDTH:] * _silu(gf_ref[...].astype(F32))).astype(BF16)
    y = (jnp.dot(om, w_ref[:BRANCH_WIDTH, :], preferred_element_type=F32)
         + jnp.dot(of, w_ref[BRANCH_WIDTH:, :], preferred_element_type=F32))
    y_ref[...] = x_ref[...] + _rms(y, g_ref[...])


def _out(o, rest, x2, w_out, g_post):
    t = o.shape[0]
    return pl.pallas_call(
        _out_kernel,
        grid=(t // OUT_TM,),
        in_specs=[
            pl.BlockSpec((OUT_TM, D_MIX), lambda i: (i, 0)),
            pl.BlockSpec((OUT_TM, BRANCH_WIDTH), lambda i: (i, 0)),
            pl.BlockSpec((OUT_TM, BRANCH_WIDTH), lambda i: (i, 4)),
            pl.BlockSpec((OUT_TM, D_MODEL), lambda i: (i, 0)),
            pl.BlockSpec((D_MIX, D_MODEL), lambda i: (0, 0)),
            pl.BlockSpec((1, D_MODEL), lambda i: (0, 0)),
        ],
        out_specs=pl.BlockSpec((OUT_TM, D_MODEL), lambda i: (i, 0)),
        out_shape=jax.ShapeDtypeStruct((t, D_MODEL), F32),
        compiler_params=pltpu.CompilerParams(
            dimension_semantics=("arbitrary",),
            vmem_limit_bytes=VMEM_LIMIT),
        name="out",
    )(o, rest, rest, x2, w_out, g_post)


def _layer(x2, pos, inv_f, g_pre, w_in, g_q, w_uq, g_kv, w_ukv, b_forget,
           w_out, g_post, batch, seq):
    o_kr = LAT_DIM
    o_gm = o_kr + ROPE_DIM
    o_fq = o_gm + BRANCH_WIDTH
    o_fl = o_fq + 3 * BRANCH_WIDTH
    o_gf = o_fl + N_HEADS
    w_lat = w_in[:, :LAT_DIM].astype(BF16)
    w_small = jnp.zeros((D_MODEL, SMALL_W), F32)
    w_small = w_small.at[:, :ROPE_DIM].set(w_in[:, o_kr:o_gm])
    w_small = w_small.at[:, LANES:LANES + N_HEADS].set(w_in[:, o_fl:o_gf])
    w_small = w_small.astype(BF16)
    w_rest = jnp.concatenate([w_in[:, o_gm:o_fl], w_in[:, o_gf:]], axis=1).astype(BF16)
    w_uq_p = jnp.pad(w_uq.reshape(Q_RANK, N_HEADS, QK_DIM),
                     ((0, 0), (0, 0), (0, FEAT - QK_DIM))).reshape(
                         Q_RANK, N_HEADS * FEAT).astype(BF16)
    w_ukv_b = w_ukv.astype(BF16)
    w_out_b = w_out.astype(BF16)
    b_f = jnp.zeros((1, LANES), F32).at[0, :N_HEADS].set(b_forget)

    lat, small, rest = _proj(x2, g_pre[None, :], w_lat, w_small, w_rest)
    qt, kp, vt = _prep(lat, small, rest, pos, g_q[None, :], g_kv[None, :],
                       w_uq_p, w_ukv_b, b_f, inv_f, batch, seq)
    o = _attn(qt, kp, vt, batch, seq)
    return _out(o, rest, x2, w_out_b, g_post[None, :])


def kernel(x, positions, g_pre, w_in, g_q_latent, w_uq, g_kv_latent, w_ukv,
           b_forget, w_out, g_post):
    batch, seq, d = x.shape
    depth = g_pre.shape[0]
    x2 = x.reshape(batch * seq, d)
    pos = positions.reshape(batch * seq, 1)
    freqs = ROPE_THETA ** (-jnp.arange(0, ROPE_DIM, 2, dtype=F32) / ROPE_DIM)
    inv_f = jnp.zeros((1, LANES), F32)
    inv_f = inv_f.at[0, :ROPE_HALF].set(freqs).at[0, ROPE_HALF:ROPE_DIM].set(freqs)
    for l in range(depth):
        x2 = _layer(x2, pos, inv_f, g_pre[l], w_in[l], g_q_latent[l], w_uq[l],
                    g_kv_latent[l], w_ukv[l], b_forget[l], w_out[l], g_post[l],
                    batch, seq)
    return x2.reshape(batch, seq, d)
```

```python
import functools
import math

import jax
import jax.numpy as jnp
from jax import lax
from jax.experimental import pallas as pl
from jax.experimental.pallas import tpu as pltpu

F32 = jnp.float32
BF16 = jnp.bfloat16

D_MODEL = 2048
N_HEADS = 8
HEAD_DIM = 128
NOPE_DIM = 128
ROPE_DIM = 64
ROPE_HALF = ROPE_DIM // 2
QK_DIM = NOPE_DIM + ROPE_DIM
Q_RANK = 768
KV_RANK = 512
LAT_DIM = Q_RANK + KV_RANK
BRANCH_WIDTH = N_HEADS * HEAD_DIM
D_MIX = 2 * BRANCH_WIDTH
ROPE_THETA = 10000.0
NORM_EPS = 1e-6
LOG2E = math.log2(math.e)

LANES = 128
FEAT = 256
SMALL_W = 2 * LANES
REST_W = 5 * BRANCH_WIDTH

MLA_QSCALE = QK_DIM ** -0.5 * LOG2E
FOX_QSCALE = HEAD_DIM ** -0.5 * LOG2E

PROJ_TM = 512
PREP_TM = 512
ATT_T = 1024
ATT_BLK = 256
ATT_HB = 2
V_ROWS = HEAD_DIM + 16
OUT_TM = 512

VMEM_LIMIT = 56 * 1024 * 1024


def _rms(x, g):
    return x * lax.rsqrt(jnp.mean(x * x, axis=-1, keepdims=True) + NORM_EPS) * g


def _proj_kernel(x_ref, g_ref, wlat_ref, wsm_ref, wrest_ref,
                 lat_ref, sm_ref, rest_ref, h_scr):
    j = pl.program_id(1)

    @pl.when(j == 0)
    def _():
        hb = _rms(x_ref[...], g_ref[...]).astype(BF16)
        h_scr[...] = hb
        lat_ref[...] = jnp.dot(hb, wlat_ref[...],
                               preferred_element_type=F32).astype(BF16)
        sm_ref[...] = jnp.dot(hb, wsm_ref[...], preferred_element_type=F32)

    acc = jnp.dot(h_scr[...], wrest_ref[...], preferred_element_type=F32)
    scale = jnp.where(j == 1, FOX_QSCALE, 1.0).astype(F32)
    rest_ref[...] = (acc * scale).astype(BF16)


def _proj(x2, g_pre, w_lat, w_small, w_rest):
    t = x2.shape[0]
    n_rest = REST_W // BRANCH_WIDTH
    return pl.pallas_call(
        _proj_kernel,
        grid=(t // PROJ_TM, n_rest),
        in_specs=[
            pl.BlockSpec((PROJ_TM, D_MODEL), lambda i, j: (i, 0)),
            pl.BlockSpec((1, D_MODEL), lambda i, j: (0, 0)),
            pl.BlockSpec((D_MODEL, LAT_DIM), lambda i, j: (0, 0)),
            pl.BlockSpec((D_MODEL, SMALL_W), lambda i, j: (0, 0)),
            pl.BlockSpec((D_MODEL, BRANCH_WIDTH), lambda i, j: (0, j)),
        ],
        out_specs=[
            pl.BlockSpec((PROJ_TM, LAT_DIM), lambda i, j: (i, 0)),
            pl.BlockSpec((PROJ_TM, SMALL_W), lambda i, j: (i, 0)),
            pl.BlockSpec((PROJ_TM, BRANCH_WIDTH), lambda i, j: (i, j)),
        ],
        out_shape=[
            jax.ShapeDtypeStruct((t, LAT_DIM), BF16),
            jax.ShapeDtypeStruct((t, SMALL_W), F32),
            jax.ShapeDtypeStruct((t, REST_W), BF16),
        ],
        scratch_shapes=[pltpu.VMEM((PROJ_TM, D_MODEL), BF16)],
        compiler_params=pltpu.CompilerParams(
            dimension_semantics=("arbitrary", "arbitrary"),
            vmem_limit_bytes=VMEM_LIMIT),
        name="proj",
    )(x2, g_pre, w_lat, w_small, w_rest)


def _split3(x):
    hi = x.astype(BF16).astype(F32)
    r = x - hi
    mid = r.astype(BF16).astype(F32)
    lo = (r - mid).astype(BF16).astype(F32)
    return hi, mid, lo


def _prep_kernel(lat_ref, sm_ref, fq_ref, fk_ref, fv_ref, pos_ref,
                 gq_ref, gkv_ref, wuq_ref, wukv_ref, bf_ref, invf_ref,
                 qt_ref, kp_ref, vt_ref, carry_scr):
    tm = lat_ref.shape[0]

    def put_t(ref, r0, x):
        ref[0, r0:r0 + x.shape[1], :] = x.astype(F32).T.astype(BF16)

    @pl.when(pl.program_id(1) == 0)
    def _():
        carry_scr[...] = jnp.zeros_like(carry_scr)

    lat = lat_ref[...].astype(F32)
    qn = _rms(lat[:, :Q_RANK], gq_ref[...]).astype(BF16)
    kvn = _rms(lat[:, Q_RANK:], gkv_ref[...]).astype(BF16)
    q = jnp.dot(qn, wuq_ref[...], preferred_element_type=F32)
    kv = jnp.dot(kvn, wukv_ref[...], preferred_element_type=F32)
    sm = sm_ref[...]

    lane = lax.broadcasted_iota(jnp.int32, (tm, LANES), 1)
    ang = pos_ref[...].astype(F32) * invf_ref[...]
    cosv = jnp.cos(ang)
    sinv = jnp.sin(ang)
    sin_hi = jnp.where(lane < ROPE_HALF, 0.0,
                       jnp.where(lane < ROPE_DIM, sinv, 0.0))
    sin_lo = jnp.where(lane < ROPE_HALF, -sinv, 0.0)

    def rope(r):
        return (r * cosv
                + pltpu.roll(r, ROPE_HALF, 1) * sin_hi
                + pltpu.roll(r, LANES - ROPE_HALF, 1) * sin_lo)

    k_rope = rope(sm[:, :LANES]).astype(BF16)
    for h in range(N_HEADS):
        c0 = h * FEAT
        put_t(qt_ref, c0, q[:, c0:c0 + LANES] * MLA_QSCALE)
        put_t(qt_ref, c0 + LANES, rope(q[:, c0 + LANES:c0 + FEAT]) * MLA_QSCALE)
        kp_ref[:, c0:c0 + LANES] = kv[:, c0:c0 + LANES].astype(BF16)
        kp_ref[:, c0 + LANES:c0 + FEAT] = k_rope
        put_t(vt_ref, h * V_ROWS, kv[:, c0 + LANES:c0 + FEAT])

    fl = sm[:, LANES:] + bf_ref[...]
    logf = jnp.minimum(fl, 0.0) - jnp.log1p(jnp.exp(-jnp.abs(fl)))
    row = lax.broadcasted_iota(jnp.int32, (tm, tm), 0)
    col = lax.broadcasted_iota(jnp.int32, (tm, tm), 1)
    tri = jnp.where(row >= col, 1.0, 0.0).astype(BF16)
    c = carry_scr[...]
    for piece in _split3(logf):
        c = c + jnp.dot(tri, piece.astype(BF16), preferred_element_type=F32)
    carry_scr[...] = c[tm - 1:tm, :]
    c2 = c * LOG2E

    for h in range(N_HEADS):
        c0 = (N_HEADS + h) * FEAT
        hi, mid, lo = _split3(jnp.broadcast_to(c2[:, h:h + 1], (tm, LANES)))
        cq = jnp.where(lane == 0, hi, jnp.where(lane == 1, mid, jnp.where(
            lane == 2, lo, jnp.where(lane < 6, 1.0, 0.0))))
        ck = jnp.where(lane < 3, 1.0, jnp.where(lane == 3, -hi, jnp.where(
            lane == 4, -mid, jnp.where(lane == 5, -lo, 0.0))))
        s0 = h * HEAD_DIM
        put_t(qt_ref, c0, fq_ref[:, s0:s0 + HEAD_DIM])
        put_t(qt_ref, c0 + LANES, cq)
        kp_ref[:, c0:c0 + LANES] = fk_ref[:, s0:s0 + HEAD_DIM]
        kp_ref[:, c0 + LANES:c0 + FEAT] = ck.astype(BF16)
        put_t(vt_ref, (N_HEADS + h) * V_ROWS, fv_ref[:, s0:s0 + HEAD_DIM])
    pad = V_ROWS - HEAD_DIM
    ones_blk = jnp.where(lax.broadcasted_iota(jnp.int32, (pad, tm), 0) == 0,
                         1.0, 0.0).astype(BF16)
    for h in range(2 * N_HEADS):
        vt_ref[0, h * V_ROWS + HEAD_DIM:(h + 1) * V_ROWS, :] = ones_blk


def _prep(lat, small, rest, pos, g_q, g_kv, w_uq, w_ukv, b_f, inv_f, batch, seq):
    t = lat.shape[0]
    ns = seq // PREP_TM
    per = ATT_T // PREP_TM
    nt = t // ATT_T
    row = lambda b, i: (b * ns + i, 0)
    const = lambda b, i: (0, 0)
    tile_t = lambda b, i: ((b * ns + i) // per, 0, (b * ns + i) % per)
    return pl.pallas_call(
        _prep_kernel,
        grid=(batch, ns),
        in_specs=[
            pl.BlockSpec((PREP_TM, LAT_DIM), row),
            pl.BlockSpec((PREP_TM, SMALL_W), row),
            pl.BlockSpec((PREP_TM, BRANCH_WIDTH), lambda b, i: (b * ns + i, 1)),
            pl.BlockSpec((PREP_TM, BRANCH_WIDTH), lambda b, i: (b * ns + i, 2)),
            pl.BlockSpec((PREP_TM, BRANCH_WIDTH), lambda b, i: (b * ns + i, 3)),
            pl.BlockSpec((PREP_TM, 1), row),
            pl.BlockSpec((1, Q_RANK), const),
            pl.BlockSpec((1, KV_RANK), const),
            pl.BlockSpec((Q_RANK, N_HEADS * FEAT), const),
            pl.BlockSpec((KV_RANK, N_HEADS * FEAT), const),
            pl.BlockSpec((1, LANES), const),
            pl.BlockSpec((1, LANES), const),
        ],
        out_specs=[
            pl.BlockSpec((1, 2 * N_HEADS * FEAT, PREP_TM), tile_t),
            pl.BlockSpec((PREP_TM, 2 * N_HEADS * FEAT), row),
            pl.BlockSpec((1, 2 * N_HEADS * V_ROWS, PREP_TM), tile_t),
        ],
        out_shape=[
            jax.ShapeDtypeStruct((nt, 2 * N_HEADS * FEAT, ATT_T), BF16),
            jax.ShapeDtypeStruct((t, 2 * N_HEADS * FEAT), BF16),
            jax.ShapeDtypeStruct((nt, 2 * N_HEADS * V_ROWS, ATT_T), BF16),
        ],
        scratch_shapes=[pltpu.VMEM((1, LANES), F32)],
        compiler_params=pltpu.CompilerParams(
            dimension_semantics=("arbitrary", "arbitrary"),
            vmem_limit_bytes=VMEM_LIMIT),
        name="prep",
    )(lat, small, rest, rest, rest, pos, g_q, g_kv, w_uq, w_ukv, b_f, inv_f)


def _attn_kernel(qt_ref, k_ref, vt_ref, o_ref, s0, s1, cm0, cm1, m_scr, acc_scr):
    t = ATT_T
    nq = qt_ref.shape[0]
    nsteps = nq * (nq + 1) // 2
    s_buf, cm_buf = (s0, s1), (cm0, cm1)

    def advance(ij):
        i, j = ij
        wrap = j >= i
        return jnp.where(wrap, i + 1, i), jnp.where(wrap, 0, j + 1)

    nblk = t // ATT_BLK
    blk = ATT_BLK

    def a_begin():
        return [[None] * nblk for _ in range(ATT_HB)]

    def a_piece(ij, slot, diagonal, hh, r, cmax):
        i, j = ij
        i = jnp.minimum(i, nq - 1)
        f0, r0 = hh * FEAT, r * blk
        k = k_ref[pl.ds(pl.multiple_of(j * t + r0, blk), blk), f0:f0 + FEAT]
        c_lo = r0 if diagonal else 0
        s = jnp.dot(k, qt_ref[i, f0:f0 + FEAT, c_lo:], preferred_element_type=F32)
        if diagonal:
            key = lax.broadcasted_iota(jnp.int32, (blk, blk), 0)
            qry = lax.broadcasted_iota(jnp.int32, (blk, blk), 1)
            parts = [jnp.where(key <= qry, s[:, :blk], -jnp.inf)]
            if s.shape[1] > blk:
                parts.append(s[:, blk:])
            s = jnp.concatenate(parts, axis=1)
        s_buf[slot][hh, r0:r0 + blk, c_lo:] = s
        for qb in range(c_lo // blk, nblk):
            part = jnp.max(s[:, qb * blk - c_lo:(qb + 1) * blk - c_lo], axis=0, keepdims=True)
            cmax[hh][qb] = part if cmax[hh][qb] is None else jnp.maximum(cmax[hh][qb], part)

    def a_end(slot, hh, cmax):
        for qb in range(nblk):
            cm_buf[slot][hh, :, qb * blk:(qb + 1) * blk] = cmax[hh][qb]

    def bc_begin(ij, slot, hh):
        i, j = ij
        m_old = jnp.where(j == 0, -jnp.inf, m_scr[hh])
        m_new = jnp.maximum(m_old, cm_buf[slot][hh])
        return dict(m_new=m_new, alpha=jnp.exp2(m_old - m_new), pv=None)

    def bc_piece(ij, slot, hh, r, st):
        i, j = ij
        d0, r0 = hh * V_ROWS, r * blk
        p = jnp.exp2(s_buf[slot][hh, r0:r0 + blk, :] - st["m_new"])
        pv = jnp.dot(vt_ref[j, d0:d0 + V_ROWS, r0:r0 + blk], p.astype(BF16),
                     preferred_element_type=F32)
        st["pv"] = pv if st["pv"] is None else st["pv"] + pv

    def bc_end(hh, st):
        m_scr[hh] = st["m_new"]
        acc_scr[hh] = st["alpha"] * acc_scr[hh] + st["pv"]

    def bc_diag_piece(ij, slot, hh, qb):
        i, j = ij
        d0, rows, c0 = hh * V_ROWS, (qb + 1) * blk, qb * blk
        cols = slice(c0, c0 + blk)
        m_old = jnp.where(j == 0, -jnp.inf, m_scr[hh, :, cols])
        m_new = jnp.maximum(m_old, cm_buf[slot][hh, :, cols])
        alpha = jnp.exp2(m_old - m_new)
        p = jnp.exp2(s_buf[slot][hh, 0:rows, cols] - m_new)
        pv = jnp.dot(vt_ref[j, d0:d0 + V_ROWS, 0:rows], p.astype(BF16),
                     preferred_element_type=F32)
        acc = alpha * acc_scr[hh, :, cols] + pv
        out = acc[:HEAD_DIM] / acc[HEAD_DIM:HEAD_DIM + 1]
        o_ref[pl.ds(pl.multiple_of(i * t + c0, blk), blk),
              hh * HEAD_DIM:(hh + 1) * HEAD_DIM] = out.T.astype(BF16)

    def run(slot_a, ija, ijb, a_is_diag, b_is_diag):
        slot_b = 1 - slot_a
        cmax = a_begin()
        sts = None if b_is_diag else [bc_begin(ijb, slot_b, hh) for hh in range(ATT_HB)]
        for r in range(nblk):
            for hh in range(ATT_HB):
                a_piece(ija, slot_a, a_is_diag, hh, r, cmax)
                if b_is_diag:
                    bc_diag_piece(ijb, slot_b, hh, r)
                else:
                    bc_piece(ijb, slot_b, hh, r, sts[hh])
        for hh in range(ATT_HB):
            a_end(slot_a, hh, cmax)
            if not b_is_diag:
                bc_end(hh, sts[hh])

    def iteration(slot_a, ija, ijb):
        a_diag = ija[1] == ija[0]
        b_diag = ijb[1] == ijb[0]
        pl.when(b_diag)(lambda: run(slot_a, ija, ijb, False, True))
        pl.when(a_diag)(lambda: run(slot_a, ija, ijb, True, False))
        pl.when(jnp.logical_not(jnp.logical_or(a_diag, b_diag)))(
            lambda: run(slot_a, ija, ijb, False, False))

    zero = jnp.int32(0)
    m_scr[...] = jnp.full(m_scr.shape, -jnp.inf, F32)
    acc_scr[...] = jnp.zeros(acc_scr.shape, F32)
    cmax0 = a_begin()
    for hh in range(ATT_HB):
        for r in range(nblk):
            a_piece((zero, zero), 0, True, hh, r, cmax0)
        a_end(0, hh, cmax0)


    def pair(_, carry):
        ija, ijb = carry
        iteration(1, ija, ijb)
        ija2 = advance(ija)
        iteration(0, ija2, ija)
        return advance(ija2), ija2

    lax.fori_loop(0, nsteps // 2, pair, (advance((zero, zero)), (zero, zero)))


def _attn(qt, kp, vt, batch, seq):
    t = kp.shape[0]
    nq = seq // ATT_T
    assert (nq * (nq + 1) // 2) % 2 == 0
    nh = 2 * N_HEADS // ATT_HB
    row = pltpu.VMEM((ATT_HB, 1, ATT_T), F32)
    tile = pltpu.VMEM((ATT_HB, ATT_T, ATT_T), F32)
    return pl.pallas_call(
        _attn_kernel,
        grid=(batch, nh),
        in_specs=[
            pl.BlockSpec((nq, ATT_HB * FEAT, ATT_T), lambda b, h: (b, h, 0)),
            pl.BlockSpec((seq, ATT_HB * FEAT), lambda b, h: (b, h)),
            pl.BlockSpec((nq, ATT_HB * V_ROWS, ATT_T), lambda b, h: (b, h, 0)),
        ],
        out_specs=pl.BlockSpec((seq, ATT_HB * HEAD_DIM), lambda b, h: (b, h)),
        out_shape=jax.ShapeDtypeStruct((t, D_MIX), BF16),
        scratch_shapes=[
            tile, tile, row, row, row,
            pltpu.VMEM((ATT_HB, V_ROWS, ATT_T), F32),
        ],
        compiler_params=pltpu.CompilerParams(
            dimension_semantics=("arbitrary", "arbitrary"),
            vmem_limit_bytes=VMEM_LIMIT),
        name="attn",
    )(qt, kp, vt)


def _silu(g):
    return g / (1.0 + jnp.exp(-g))


def _out_kernel(o_ref, gm_ref, gf_ref, x_ref, w_ref, g_ref, y_ref):
    o = o_ref[...].astype(F32)
    om = (o[:, :BRANCH_WIDTH] * _silu(gm_ref[...].astype(F32))).astype(BF16)
    of = (o[:, BRANCH_WIDTH:] * _silu(gf_ref[...].astype(F32))).astype(BF16)
    y = (jnp.dot(om, w_ref[:BRANCH_WIDTH, :], preferred_element_type=F32)
         + jnp.dot(of, w_ref[BRANCH_WIDTH:, :], preferred_element_type=F32))
    y_ref[...] = x_ref[...] + _rms(y, g_ref[...])


def _out(o, rest, x2, w_out, g_post):
    t = o.shape[0]
    return pl.pallas_call(
        _out_kernel,
        grid=(t // OUT_TM,),
        in_specs=[
            pl.BlockSpec((OUT_TM, D_MIX), lambda i: (i, 0)),
            pl.BlockSpec((OUT_TM, BRANCH_WIDTH), lambda i: (i, 0)),
            pl.BlockSpec((OUT_TM, BRANCH_WIDTH), lambda i: (i, 4)),
            pl.BlockSpec((OUT_TM, D_MODEL), lambda i: (i, 0)),
            pl.BlockSpec((D_MIX, D_MODEL), lambda i: (0, 0)),
            pl.BlockSpec((1, D_MODEL), lambda i: (0, 0)),
        ],
        out_specs=pl.BlockSpec((OUT_TM, D_MODEL), lambda i: (i, 0)),
        out_shape=jax.ShapeDtypeStruct((t, D_MODEL), F32),
        compiler_params=pltpu.CompilerParams(
            dimension_semantics=("arbitrary",),
            vmem_limit_bytes=VMEM_LIMIT),
        name="out",
    )(o, rest, rest, x2, w_out, g_post)


def _layer(x2, pos, inv_f, g_pre, w_in, g_q, w_uq, g_kv, w_ukv, b_forget,
           w_out, g_post, batch, seq):
    o_kr = LAT_DIM
    o_gm = o_kr + ROPE_DIM
    o_fq = o_gm + BRANCH_WIDTH
    o_fl = o_fq + 3 * BRANCH_WIDTH
    o_gf = o_fl + N_HEADS
    w_lat = w_in[:, :LAT_DIM].astype(BF16)
    w_small = jnp.zeros((D_MODEL, SMALL_W), F32)
    w_small = w_small.at[:, :ROPE_DIM].set(w_in[:, o_kr:o_gm])
    w_small = w_small.at[:, LANES:LANES + N_HEADS].set(w_in[:, o_fl:o_gf])
    w_small = w_small.astype(BF16)
    w_rest = jnp.concatenate([w_in[:, o_gm:o_fl], w_in[:, o_gf:]], axis=1).astype(BF16)
    w_uq_p = jnp.pad(w_uq.reshape(Q_RANK, N_HEADS, QK_DIM),
                     ((0, 0), (0, 0), (0, FEAT - QK_DIM))).reshape(
                         Q_RANK, N_HEADS * FEAT).astype(BF16)
    w_ukv_b = w_ukv.astype(BF16)
    w_out_b = w_out.astype(BF16)
    b_f = jnp.zeros((1, LANES), F32).at[0, :N_HEADS].set(b_forget)

    lat, small, rest = _proj(x2, g_pre[None, :], w_lat, w_small, w_rest)
    qt, kp, vt = _prep(lat, small, rest, pos, g_q[None, :], g_kv[None, :],
                       w_uq_p, w_ukv_b, b_f, inv_f, batch, seq)
    o = _attn(qt, kp, vt, batch, seq)
    return _out(o, rest, x2, w_out_b, g_post[None, :])


def kernel(x, positions, g_pre, w_in, g_q_latent, w_uq, g_kv_latent, w_ukv,
           b_forget, w_out, g_post):
    batch, seq, d = x.shape
    depth = g_pre.shape[0]
    x2 = x.reshape(batch * seq, d)
    pos = positions.reshape(batch * seq, 1)
    freqs = ROPE_THETA ** (-jnp.arange(0, ROPE_DIM, 2, dtype=F32) / ROPE_DIM)
    inv_f = jnp.zeros((1, LANES), F32)
    inv_f = inv_f.at[0, :ROPE_HALF].set(freqs).at[0, ROPE_HALF:ROPE_DIM].set(freqs)
    for l in range(depth):
        x2 = _layer(x2, pos, inv_f, g_pre[l], w_in[l], g_q_latent[l], w_uq[l],
                    g_kv_latent[l], w_ukv[l], b_forget[l], w_out[l], g_post[l],
                    batch, seq)
    return x2.reshape(batch, seq, d)
```

```python
import functools
import math

import jax
import jax.numpy as jnp
from jax import lax
from jax.experimental import pallas as pl
from jax.experimental.pallas import tpu as pltpu

F32 = jnp.float32
BF16 = jnp.bfloat16

D_MODEL = 2048
N_HEADS = 8
HEAD_DIM = 128
NOPE_DIM = 128
ROPE_DIM = 64
ROPE_HALF = ROPE_DIM // 2
QK_DIM = NOPE_DIM + ROPE_DIM
Q_RANK = 768
KV_RANK = 512
LAT_DIM = Q_RANK + KV_RANK
BRANCH_WIDTH = N_HEADS * HEAD_DIM
D_MIX = 2 * BRANCH_WIDTH
ROPE_THETA = 10000.0
NORM_EPS = 1e-6
LOG2E = math.log2(math.e)

LANES = 128
FEAT = 256
SMALL_W = 2 * LANES
REST_W = 5 * BRANCH_WIDTH

MLA_QSCALE = QK_DIM ** -0.5 * LOG2E
FOX_QSCALE = HEAD_DIM ** -0.5 * LOG2E

PROJ_TM = 1024
PREP_TM = 512
ATT_T = 1024
ATT_BLK = 256
ATT_HB = 2
V_ROWS = HEAD_DIM + 16
OUT_TM = 512

VMEM_LIMIT = 56 * 1024 * 1024


def _rms(x, g):
    return x * lax.rsqrt(jnp.mean(x * x, axis=-1, keepdims=True) + NORM_EPS) * g


def _proj_kernel(x_ref, g_ref, wlat_ref, wsm_ref, wrest_ref,
                 lat_ref, sm_ref, rest_ref, h_scr):
    j = pl.program_id(1)

    @pl.when(j == 0)
    def _():
        hb = _rms(x_ref[...], g_ref[...]).astype(BF16)
        h_scr[...] = hb
        lat_ref[...] = jnp.dot(hb, wlat_ref[...],
                               preferred_element_type=F32).astype(BF16)
        sm_ref[...] = jnp.dot(hb, wsm_ref[...], preferred_element_type=F32)

    acc = jnp.dot(h_scr[...], wrest_ref[...], preferred_element_type=F32)
    scale = jnp.where(j == 1, FOX_QSCALE, 1.0).astype(F32)
    rest_ref[...] = (acc * scale).astype(BF16)


def _proj(x2, g_pre, w_all):
    t = x2.shape[0]
    n_rest = REST_W // BRANCH_WIDTH
    assert REST_W % LAT_DIM == 0 and (REST_W + LAT_DIM) % SMALL_W == 0
    once = pl.Buffered(1)
    return pl.pallas_call(
        _proj_kernel,
        grid=(t // PROJ_TM, n_rest),
        in_specs=[
            pl.BlockSpec((PROJ_TM, D_MODEL), lambda i, j: (i, 0)),
            pl.BlockSpec((1, D_MODEL), lambda i, j: (0, 0)),
            pl.BlockSpec((D_MODEL, LAT_DIM), lambda i, j: (0, REST_W // LAT_DIM),
                         pipeline_mode=once),
            pl.BlockSpec((D_MODEL, SMALL_W), lambda i, j: (0, (REST_W + LAT_DIM) // SMALL_W),
                         pipeline_mode=once),
            pl.BlockSpec((D_MODEL, BRANCH_WIDTH), lambda i, j: (0, j)),
        ],
        out_specs=[
            pl.BlockSpec((PROJ_TM, LAT_DIM), lambda i, j: (i, 0)),
            pl.BlockSpec((PROJ_TM, SMALL_W), lambda i, j: (i, 0)),
            pl.BlockSpec((PROJ_TM, BRANCH_WIDTH), lambda i, j: (i, j)),
        ],
        out_shape=[
            jax.ShapeDtypeStruct((t, LAT_DIM), BF16),
            jax.ShapeDtypeStruct((t, SMALL_W), F32),
            jax.ShapeDtypeStruct((t, REST_W), BF16),
        ],
        scratch_shapes=[pltpu.VMEM((PROJ_TM, D_MODEL), BF16)],
        compiler_params=pltpu.CompilerParams(
            dimension_semantics=("arbitrary", "arbitrary"),
            vmem_limit_bytes=VMEM_LIMIT),
        name="proj",
    )(x2, g_pre, w_all, w_all, w_all)


def _split3(x):
    hi = x.astype(BF16).astype(F32)
    r = x - hi
    mid = r.astype(BF16).astype(F32)
    lo = (r - mid).astype(BF16).astype(F32)
    return hi, mid, lo


def _prep_kernel(lat_ref, sm_ref, fq_ref, fk_ref, fv_ref, pos_ref,
                 gq_ref, gkv_ref, wuq_ref, wukv_ref, bf_ref, invf_ref,
                 qt_ref, kp_ref, vt_ref, carry_scr):
    tm = lat_ref.shape[0]

    def put_t(ref, r0, x):
        ref[0, r0:r0 + x.shape[1], :] = x.astype(F32).T.astype(BF16)

    @pl.when(pl.program_id(1) == 0)
    def _():
        carry_scr[...] = jnp.zeros_like(carry_scr)

    lat = lat_ref[...].astype(F32)
    qn = _rms(lat[:, :Q_RANK], gq_ref[...]).astype(BF16)
    kvn = _rms(lat[:, Q_RANK:], gkv_ref[...]).astype(BF16)
    q = jnp.dot(qn, wuq_ref[...], preferred_element_type=F32)
    kv = jnp.dot(kvn, wukv_ref[...], preferred_element_type=F32)
    sm = sm_ref[...]

    lane = lax.broadcasted_iota(jnp.int32, (tm, LANES), 1)
    ang = pos_ref[...].astype(F32) * invf_ref[...]
    cosv = jnp.cos(ang)
    sinv = jnp.sin(ang)
    sin_hi = jnp.where(lane < ROPE_HALF, 0.0,
                       jnp.where(lane < ROPE_DIM, sinv, 0.0))
    sin_lo = jnp.where(lane < ROPE_HALF, -sinv, 0.0)

    def rope(r):
        return (r * cosv
                + pltpu.roll(r, ROPE_HALF, 1) * sin_hi
                + pltpu.roll(r, LANES - ROPE_HALF, 1) * sin_lo)

    k_rope = rope(sm[:, :LANES]).astype(BF16)
    for h in range(N_HEADS):
        c0 = h * FEAT
        put_t(qt_ref, c0, q[:, c0:c0 + LANES] * MLA_QSCALE)
        put_t(qt_ref, c0 + LANES, rope(q[:, c0 + LANES:c0 + FEAT]) * MLA_QSCALE)
        kp_ref[:, c0:c0 + LANES] = kv[:, c0:c0 + LANES].astype(BF16)
        kp_ref[:, c0 + LANES:c0 + FEAT] = k_rope
        put_t(vt_ref, h * V_ROWS, kv[:, c0 + LANES:c0 + FEAT])

    fl = sm[:, LANES:] + bf_ref[...]
    logf = jnp.minimum(fl, 0.0) - jnp.log1p(jnp.exp(-jnp.abs(fl)))
    row = lax.broadcasted_iota(jnp.int32, (tm, tm), 0)
    col = lax.broadcasted_iota(jnp.int32, (tm, tm), 1)
    tri = jnp.where(row >= col, 1.0, 0.0).astype(BF16)
    c = carry_scr[...]
    for piece in _split3(logf):
        c = c + jnp.dot(tri, piece.astype(BF16), preferred_element_type=F32)
    carry_scr[...] = c[tm - 1:tm, :]
    c2 = c * LOG2E

    for h in range(N_HEADS):
        c0 = (N_HEADS + h) * FEAT
        hi, mid, lo = _split3(jnp.broadcast_to(c2[:, h:h + 1], (tm, LANES)))
        cq = jnp.where(lane == 0, hi, jnp.where(lane == 1, mid, jnp.where(
            lane == 2, lo, jnp.where(lane < 6, 1.0, 0.0))))
        ck = jnp.where(lane < 3, 1.0, jnp.where(lane == 3, -hi, jnp.where(
            lane == 4, -mid, jnp.where(lane == 5, -lo, 0.0))))
        s0 = h * HEAD_DIM
        put_t(qt_ref, c0, fq_ref[:, s0:s0 + HEAD_DIM])
        put_t(qt_ref, c0 + LANES, cq)
        kp_ref[:, c0:c0 + LANES] = fk_ref[:, s0:s0 + HEAD_DIM]
        kp_ref[:, c0 + LANES:c0 + FEAT] = ck.astype(BF16)
        put_t(vt_ref, (N_HEADS + h) * V_ROWS, fv_ref[:, s0:s0 + HEAD_DIM])
    pad = V_ROWS - HEAD_DIM
    ones_blk = jnp.where(lax.broadcasted_iota(jnp.int32, (pad, tm), 0) == 0,
                         1.0, 0.0).astype(BF16)
    for h in range(2 * N_HEADS):
        vt_ref[0, h * V_ROWS + HEAD_DIM:(h + 1) * V_ROWS, :] = ones_blk


def _prep(lat, small, rest, pos, g_q, g_kv, w_uq, w_ukv, b_f, inv_f, batch, seq):
    t = lat.shape[0]
    ns = seq // PREP_TM
    per = ATT_T // PREP_TM
    nt = t // ATT_T
    row = lambda b, i: (b * ns + i, 0)
    const = lambda b, i: (0, 0)
    tile_t = lambda b, i: ((b * ns + i) // per, 0, (b * ns + i) % per)
    return pl.pallas_call(
        _prep_kernel,
        grid=(batch, ns),
        in_specs=[
            pl.BlockSpec((PREP_TM, LAT_DIM), row),
            pl.BlockSpec((PREP_TM, SMALL_W), row),
            pl.BlockSpec((PREP_TM, BRANCH_WIDTH), lambda b, i: (b * ns + i, 1)),
            pl.BlockSpec((PREP_TM, BRANCH_WIDTH), lambda b, i: (b * ns + i, 2)),
            pl.BlockSpec((PREP_TM, BRANCH_WIDTH), lambda b, i: (b * ns + i, 3)),
            pl.BlockSpec((PREP_TM, 1), row),
            pl.BlockSpec((1, Q_RANK), const),
            pl.BlockSpec((1, KV_RANK), const),
            pl.BlockSpec((Q_RANK, N_HEADS * FEAT), const),
            pl.BlockSpec((KV_RANK, N_HEADS * FEAT), const),
            pl.BlockSpec((1, LANES), const),
            pl.BlockSpec((1, LANES), const),
        ],
        out_specs=[
            pl.BlockSpec((1, 2 * N_HEADS * FEAT, PREP_TM), tile_t),
            pl.BlockSpec((PREP_TM, 2 * N_HEADS * FEAT), row),
            pl.BlockSpec((1, 2 * N_HEADS * V_ROWS, PREP_TM), tile_t),
        ],
        out_shape=[
            jax.ShapeDtypeStruct((nt, 2 * N_HEADS * FEAT, ATT_T), BF16),
            jax.ShapeDtypeStruct((t, 2 * N_HEADS * FEAT), BF16),
            jax.ShapeDtypeStruct((nt, 2 * N_HEADS * V_ROWS, ATT_T), BF16),
        ],
        scratch_shapes=[pltpu.VMEM((1, LANES), F32)],
        compiler_params=pltpu.CompilerParams(
            dimension_semantics=("arbitrary", "arbitrary"),
            vmem_limit_bytes=VMEM_LIMIT),
        name="prep",
    )(lat, small, rest, rest, rest, pos, g_q, g_kv, w_uq, w_ukv, b_f, inv_f)


def _attn_kernel(qt_ref, k_ref, vt_ref, o_ref, s0, s1, cm0, cm1, m_scr, acc_scr):
    t = ATT_T
    nq = qt_ref.shape[0]
    nsteps = nq * (nq + 1) // 2
    s_buf, cm_buf = (s0, s1), (cm0, cm1)

    def advance(ij):
        i, j = ij
        wrap = j >= i
        return jnp.where(wrap, i + 1, i), jnp.where(wrap, 0, j + 1)

    nblk = t // ATT_BLK
    blk = ATT_BLK

    def a_begin():
        return [[None] * nblk for _ in range(ATT_HB)]

    def a_piece(ij, slot, diagonal, hh, r, cmax):
        i, j = ij
        i = jnp.minimum(i, nq - 1)
        f0, r0 = hh * FEAT, r * blk
        k = k_ref[pl.ds(pl.multiple_of(j * t + r0, blk), blk), f0:f0 + FEAT]
        c_lo = r0 if diagonal else 0
        s = jnp.dot(k, qt_ref[i, f0:f0 + FEAT, c_lo:], preferred_element_type=F32)
        if diagonal:
            key = lax.broadcasted_iota(jnp.int32, (blk, blk), 0)
            qry = lax.broadcasted_iota(jnp.int32, (blk, blk), 1)
            parts = [jnp.where(key <= qry, s[:, :blk], -jnp.inf)]
            if s.shape[1] > blk:
                parts.append(s[:, blk:])
            s = jnp.concatenate(parts, axis=1)
        s_buf[slot][hh, r0:r0 + blk, c_lo:] = s
        for qb in range(c_lo // blk, nblk):
            part = jnp.max(s[:, qb * blk - c_lo:(qb + 1) * blk - c_lo], axis=0, keepdims=True)
            cmax[hh][qb] = part if cmax[hh][qb] is None else jnp.maximum(cmax[hh][qb], part)

    def a_end(slot, hh, cmax):
        for qb in range(nblk):
            cm_buf[slot][hh, :, qb * blk:(qb + 1) * blk] = cmax[hh][qb]

    def bc_begin(ij, slot, hh):
        i, j = ij
        m_old = jnp.where(j == 0, -jnp.inf, m_scr[hh])
        m_new = jnp.maximum(m_old, cm_buf[slot][hh])
        return dict(m_new=m_new, alpha=jnp.exp2(m_old - m_new), pv=None)

    def bc_piece(ij, slot, hh, r, st):
        i, j = ij
        d0, r0 = hh * V_ROWS, r * blk
        p = jnp.exp2(s_buf[slot][hh, r0:r0 + blk, :] - st["m_new"])
        pv = jnp.dot(vt_ref[j, d0:d0 + V_ROWS, r0:r0 + blk], p.astype(BF16),
                     preferred_element_type=F32)
        st["pv"] = pv if st["pv"] is None else st["pv"] + pv

    def bc_end(hh, st):
        m_scr[hh] = st["m_new"]
        acc_scr[hh] = st["alpha"] * acc_scr[hh] + st["pv"]

    def bc_diag_piece(ij, slot, hh, qb):
        i, j = ij
        d0, rows, c0 = hh * V_ROWS, (qb + 1) * blk, qb * blk
        cols = slice(c0, c0 + blk)
        m_old = jnp.where(j == 0, -jnp.inf, m_scr[hh, :, cols])
        m_new = jnp.maximum(m_old, cm_buf[slot][hh, :, cols])
        alpha = jnp.exp2(m_old - m_new)
        p = jnp.exp2(s_buf[slot][hh, 0:rows, cols] - m_new)
        pv = jnp.dot(vt_ref[j, d0:d0 + V_ROWS, 0:rows], p.astype(BF16),
                     preferred_element_type=F32)
        acc = alpha * acc_scr[hh, :, cols] + pv
        out = acc[:HEAD_DIM] / acc[HEAD_DIM:HEAD_DIM + 1]
        o_ref[pl.ds(pl.multiple_of(i * t + c0, blk), blk),
              hh * HEAD_DIM:(hh + 1) * HEAD_DIM] = out.T.astype(BF16)

    def run(slot_a, ija, ijb, a_is_diag, b_is_diag):
        slot_b = 1 - slot_a
        cmax = a_begin()
        sts = None if b_is_diag else [bc_begin(ijb, slot_b, hh) for hh in range(ATT_HB)]
        for r in range(nblk):
            for hh in range(ATT_HB):
                a_piece(ija, slot_a, a_is_diag, hh, r, cmax)
                if b_is_diag:
                    bc_diag_piece(ijb, slot_b, hh, r)
                else:
                    bc_piece(ijb, slot_b, hh, r, sts[hh])
        for hh in range(ATT_HB):
            a_end(slot_a, hh, cmax)
            if not b_is_diag:
                bc_end(hh, sts[hh])

    def iteration(slot_a, ija, ijb):
        a_diag = ija[1] == ija[0]
        b_diag = ijb[1] == ijb[0]
        pl.when(b_diag)(lambda: run(slot_a, ija, ijb, False, True))
        pl.when(a_diag)(lambda: run(slot_a, ija, ijb, True, False))
        pl.when(jnp.logical_not(jnp.logical_or(a_diag, b_diag)))(
            lambda: run(slot_a, ija, ijb, False, False))

    zero = jnp.int32(0)
    m_scr[...] = jnp.full(m_scr.shape, -jnp.inf, F32)
    acc_scr[...] = jnp.zeros(acc_scr.shape, F32)
    cmax0 = a_begin()
    for hh in range(ATT_HB):
        for r in range(nblk):
            a_piece((zero, zero), 0, True, hh, r, cmax0)
        a_end(0, hh, cmax0)


    def pair(_, carry):
        ija, ijb = carry
        iteration(1, ija, ijb)
        ija2 = advance(ija)
        iteration(0, ija2, ija)
        return advance(ija2), ija2

    lax.fori_loop(0, nsteps // 2, pair, (advance((zero, zero)), (zero, zero)))


def _attn(qt, kp, vt, batch, seq):
    t = kp.shape[0]
    nq = seq // ATT_T
    assert (nq * (nq + 1) // 2) % 2 == 0
    nh = 2 * N_HEADS // ATT_HB
    row = pltpu.VMEM((ATT_HB, 1, ATT_T), F32)
    tile = pltpu.VMEM((ATT_HB, ATT_T, ATT_T), F32)
    return pl.pallas_call(
        _attn_kernel,
        grid=(batch, nh),
        in_specs=[
            pl.BlockSpec((nq, ATT_HB * FEAT, ATT_T), lambda b, h: (b, h, 0)),
            pl.BlockSpec((seq, ATT_HB * FEAT), lambda b, h: (b, h)),
            pl.BlockSpec((nq, ATT_HB * V_ROWS, ATT_T), lambda b, h: (b, h, 0)),
        ],
        out_specs=pl.BlockSpec((seq, ATT_HB * HEAD_DIM), lambda b, h: (b, h)),
        out_shape=jax.ShapeDtypeStruct((t, D_MIX), BF16),
        scratch_shapes=[
            tile, tile, row, row, row,
            pltpu.VMEM((ATT_HB, V_ROWS, ATT_T), F32),
        ],
        compiler_params=pltpu.CompilerParams(
            dimension_semantics=("arbitrary", "arbitrary"),
            vmem_limit_bytes=VMEM_LIMIT),
        name="attn",
    )(qt, kp, vt)


def _silu(g):
    return g / (1.0 + jnp.exp(-g))


def _out_kernel(o_ref, gm_ref, gf_ref, x_ref, w_ref, g_ref, y_ref):
    o = o_ref[...].astype(F32)
    om = (o[:, :BRANCH_WIDTH] * _silu(gm_ref[...].astype(F32))).astype(BF16)
    of = (o[:, BRANCH_WIDTH:] * _silu(gf_ref[...].astype(F32))).astype(BF16)
    y = (jnp.dot(om, w_ref[:BRANCH_WIDTH, :], preferred_element_type=F32)
         + jnp.dot(of, w_ref[BRANCH_WIDTH:, :], preferred_element_type=F32))
    y_ref[...] = x_ref[...] + _rms(y, g_ref[...])


def _out(o, rest, x2, w_out, g_post):
    t = o.shape[0]
    return pl.pallas_call(
        _out_kernel,
        grid=(t // OUT_TM,),
        in_specs=[
            pl.BlockSpec((OUT_TM, D_MIX), lambda i: (i, 0)),
            pl.BlockSpec((OUT_TM, BRANCH_WIDTH), lambda i: (i, 0)),
            pl.BlockSpec((OUT_TM, BRANCH_WIDTH), lambda i: (i, 4)),
            pl.BlockSpec((OUT_TM, D_MODEL), lambda i: (i, 0)),
            pl.BlockSpec((D_MIX, D_MODEL), lambda i: (0, 0)),
            pl.BlockSpec((1, D_MODEL), lambda i: (0, 0)),
        ],
        out_specs=pl.BlockSpec((OUT_TM, D_MODEL), lambda i: (i, 0)),
        out_shape=jax.ShapeDtypeStruct((t, D_MODEL), F32),
        compiler_params=pltpu.CompilerParams(
            dimension_semantics=("arbitrary",),
            vmem_limit_bytes=VMEM_LIMIT),
        name="out",
    )(o, rest, rest, x2, w_out, g_post)


def _layer(x2, pos, inv_f, g_pre, w_in, g_q, w_uq, g_kv, w_ukv, b_forget,
           w_out, g_post, batch, seq):
    o_kr = LAT_DIM
    o_gm = o_kr + ROPE_DIM
    o_fq = o_gm + BRANCH_WIDTH
    o_fl = o_fq + 3 * BRANCH_WIDTH
    o_gf = o_fl + N_HEADS
    zcols = lambda n: jnp.zeros((D_MODEL, n), F32)
    w_all = jnp.concatenate([
        w_in[:, o_gm:o_fl], w_in[:, o_gf:],
        w_in[:, :LAT_DIM],
        w_in[:, o_kr:o_gm], zcols(LANES - ROPE_DIM),
        w_in[:, o_fl:o_gf], zcols(LANES - N_HEADS),
    ], axis=1).astype(BF16)
    w_uq_p = jnp.pad(w_uq.reshape(Q_RANK, N_HEADS, QK_DIM),
                     ((0, 0), (0, 0), (0, FEAT - QK_DIM))).reshape(
                         Q_RANK, N_HEADS * FEAT).astype(BF16)
    w_ukv_b = w_ukv.astype(BF16)
    w_out_b = w_out.astype(BF16)
    b_f = jnp.pad(b_forget, (0, LANES - N_HEADS))[None, :]

    lat, small, rest = _proj(x2, g_pre[None, :], w_all)
    qt, kp, vt = _prep(lat, small, rest, pos, g_q[None, :], g_kv[None, :],
                       w_uq_p, w_ukv_b, b_f, inv_f, batch, seq)
    o = _attn(qt, kp, vt, batch, seq)
    return _out(o, rest, x2, w_out_b, g_post[None, :])


def kernel(x, positions, g_pre, w_in, g_q_latent, w_uq, g_kv_latent, w_ukv,
           b_forget, w_out, g_post):
    batch, seq, d = x.shape
    depth = g_pre.shape[0]
    x2 = x.reshape(batch * seq, d)
    pos = positions.reshape(batch * seq, 1)
    freqs = ROPE_THETA ** (-jnp.arange(0, ROPE_DIM, 2, dtype=F32) / ROPE_DIM)
    inv_f = jnp.concatenate([freqs, freqs, jnp.zeros((LANES - ROPE_DIM,), F32)])[None, :]
    for l in range(depth):
        x2 = _layer(x2, pos, inv_f, g_pre[l], w_in[l], g_q_latent[l], w_uq[l],
                    g_kv_latent[l], w_ukv[l], b_forget[l], w_out[l], g_post[l],
                    batch, seq)
    return x2.reshape(batch, seq, d)
```

```python
import functools
import math

import jax
import jax.numpy as jnp
from jax import lax
from jax.experimental import pallas as pl
from jax.experimental.pallas import tpu as pltpu

F32 = jnp.float32
BF16 = jnp.bfloat16

D_MODEL = 2048
N_HEADS = 8
HEAD_DIM = 128
NOPE_DIM = 128
ROPE_DIM = 64
ROPE_HALF = ROPE_DIM // 2
QK_DIM = NOPE_DIM + ROPE_DIM
Q_RANK = 768
KV_RANK = 512
LAT_DIM = Q_RANK + KV_RANK
BRANCH_WIDTH = N_HEADS * HEAD_DIM
D_MIX = 2 * BRANCH_WIDTH
ROPE_THETA = 10000.0
NORM_EPS = 1e-6
LOG2E = math.log2(math.e)

LANES = 128
FEAT = 256
SMALL_W = 2 * LANES
REST_W = 5 * BRANCH_WIDTH

MLA_QSCALE = QK_DIM ** -0.5 * LOG2E
FOX_QSCALE = HEAD_DIM ** -0.5 * LOG2E

PACK_TM = 256
PROJ_TM = 1024
PREP_TM = 512
ATT_T = 1024
ATT_BLK = 256
ATT_HB = 2
V_ROWS = HEAD_DIM + 16
OUT_TM = 512

VMEM_LIMIT = 56 * 1024 * 1024


def _rms(x, g):
    return x * lax.rsqrt(jnp.mean(x * x, axis=-1, keepdims=True) + NORM_EPS) * g


def _pack_kernel(w_ref, o_ref):
    o_kr = LAT_DIM
    o_gm = o_kr + ROPE_DIM
    o_fl = o_gm + 4 * BRANCH_WIDTH
    o_gf = o_fl + N_HEADS
    w = w_ref[...]
    rows = w.shape[0]
    zeros = lambda n: jnp.zeros((rows, n), F32)
    o_ref[...] = jnp.concatenate([
        w[:, o_gm:o_fl], w[:, o_gf:o_gf + BRANCH_WIDTH],
        w[:, :LAT_DIM],
        w[:, o_kr:o_gm], zeros(LANES - ROPE_DIM),
        w[:, o_fl:o_gf], zeros(LANES - N_HEADS),
    ], axis=1).astype(BF16)


def _pack_w_in(w_in):
    d, d_in = w_in.shape
    width = REST_W + LAT_DIM + SMALL_W
    return pl.pallas_call(
        _pack_kernel,
        grid=(d // PACK_TM,),
        in_specs=[pl.BlockSpec((PACK_TM, d_in), lambda i: (i, 0))],
        out_specs=pl.BlockSpec((PACK_TM, width), lambda i: (i, 0)),
        out_shape=jax.ShapeDtypeStruct((d, width), BF16),
        compiler_params=pltpu.CompilerParams(
            dimension_semantics=("arbitrary",),
            vmem_limit_bytes=VMEM_LIMIT),
        name="pack",
    )(w_in)


def _proj_kernel(x_ref, g_ref, wlat_ref, wsm_ref, wrest_ref,
                 lat_ref, sm_ref, rest_ref, h_scr):
    j = pl.program_id(1)

    @pl.when(j == 0)
    def _():
        hb = _rms(x_ref[...], g_ref[...]).astype(BF16)
        h_scr[...] = hb
        lat_ref[...] = jnp.dot(hb, wlat_ref[...],
                               preferred_element_type=F32).astype(BF16)
        sm_ref[...] = jnp.dot(hb, wsm_ref[...], preferred_element_type=F32)

    acc = jnp.dot(h_scr[...], wrest_ref[...], preferred_element_type=F32)
    scale = jnp.where(j == 1, FOX_QSCALE, 1.0).astype(F32)
    rest_ref[...] = (acc * scale).astype(BF16)


def _proj(x2, g_pre, w_all):
    t = x2.shape[0]
    n_rest = REST_W // BRANCH_WIDTH
    assert REST_W % LAT_DIM == 0 and (REST_W + LAT_DIM) % SMALL_W == 0
    once = pl.Buffered(1)
    return pl.pallas_call(
        _proj_kernel,
        grid=(t // PROJ_TM, n_rest),
        in_specs=[
            pl.BlockSpec((PROJ_TM, D_MODEL), lambda i, j: (i, 0)),
            pl.BlockSpec((1, D_MODEL), lambda i, j: (0, 0)),
            pl.BlockSpec((D_MODEL, LAT_DIM), lambda i, j: (0, REST_W // LAT_DIM),
                         pipeline_mode=once),
            pl.BlockSpec((D_MODEL, SMALL_W), lambda i, j: (0, (REST_W + LAT_DIM) // SMALL_W),
                         pipeline_mode=once),
            pl.BlockSpec((D_MODEL, BRANCH_WIDTH), lambda i, j: (0, j)),
        ],
        out_specs=[
            pl.BlockSpec((PROJ_TM, LAT_DIM), lambda i, j: (i, 0)),
            pl.BlockSpec((PROJ_TM, SMALL_W), lambda i, j: (i, 0)),
            pl.BlockSpec((PROJ_TM, BRANCH_WIDTH), lambda i, j: (i, j)),
        ],
        out_shape=[
            jax.ShapeDtypeStruct((t, LAT_DIM), BF16),
            jax.ShapeDtypeStruct((t, SMALL_W), F32),
            jax.ShapeDtypeStruct((t, REST_W), BF16),
        ],
        scratch_shapes=[pltpu.VMEM((PROJ_TM, D_MODEL), BF16)],
        compiler_params=pltpu.CompilerParams(
            dimension_semantics=("arbitrary", "arbitrary"),
            vmem_limit_bytes=VMEM_LIMIT),
        name="proj",
    )(x2, g_pre, w_all, w_all, w_all)


def _split3(x):
    hi = x.astype(BF16).astype(F32)
    r = x - hi
    mid = r.astype(BF16).astype(F32)
    lo = (r - mid).astype(BF16).astype(F32)
    return hi, mid, lo


def _prep_kernel(lat_ref, sm_ref, fq_ref, fk_ref, fv_ref, pos_ref,
                 gq_ref, gkv_ref, wuq_ref, wukv_ref, bf_ref, invf_ref,
                 qt_ref, kp_ref, vt_ref, carry_scr):
    tm = lat_ref.shape[0]

    def put_t(ref, r0, x):
        ref[0, r0:r0 + x.shape[1], :] = x.astype(F32).T.astype(BF16)

    @pl.when(pl.program_id(1) == 0)
    def _():
        carry_scr[...] = jnp.zeros_like(carry_scr)

    lat = lat_ref[...].astype(F32)
    qn = _rms(lat[:, :Q_RANK], gq_ref[...]).astype(BF16)
    kvn = _rms(lat[:, Q_RANK:], gkv_ref[...]).astype(BF16)
    q = jnp.dot(qn, wuq_ref[...], preferred_element_type=F32)
    kv = jnp.dot(kvn, wukv_ref[...], preferred_element_type=F32)
    sm = sm_ref[...]

    lane = lax.broadcasted_iota(jnp.int32, (tm, LANES), 1)
    ang = pos_ref[...].astype(F32) * invf_ref[...]
    cosv = jnp.cos(ang)
    sinv = jnp.sin(ang)
    sin_hi = jnp.where(lane < ROPE_HALF, 0.0,
                       jnp.where(lane < ROPE_DIM, sinv, 0.0))
    sin_lo = jnp.where(lane < ROPE_HALF, -sinv, 0.0)

    def rope(r):
        return (r * cosv
                + pltpu.roll(r, ROPE_HALF, 1) * sin_hi
                + pltpu.roll(r, LANES - ROPE_HALF, 1) * sin_lo)

    k_rope = rope(sm[:, :LANES]).astype(BF16)
    for h in range(N_HEADS):
        c0 = h * FEAT
        put_t(qt_ref, c0, q[:, c0:c0 + LANES] * MLA_QSCALE)
        put_t(qt_ref, c0 + LANES, rope(q[:, c0 + LANES:c0 + FEAT]) * MLA_QSCALE)
        kp_ref[:, c0:c0 + LANES] = kv[:, c0:c0 + LANES].astype(BF16)
        kp_ref[:, c0 + LANES:c0 + FEAT] = k_rope
        put_t(vt_ref, h * V_ROWS, kv[:, c0 + LANES:c0 + FEAT])

    fl = sm[:, LANES:] + bf_ref[...]
    logf = jnp.minimum(fl, 0.0) - jnp.log1p(jnp.exp(-jnp.abs(fl)))
    row = lax.broadcasted_iota(jnp.int32, (tm, tm), 0)
    col = lax.broadcasted_iota(jnp.int32, (tm, tm), 1)
    tri = jnp.where(row >= col, 1.0, 0.0).astype(BF16)
    c = carry_scr[...]
    for piece in _split3(logf):
        c = c + jnp.dot(tri, piece.astype(BF16), preferred_element_type=F32)
    carry_scr[...] = c[tm - 1:tm, :]
    c2 = c * LOG2E

    for h in range(N_HEADS):
        c0 = (N_HEADS + h) * FEAT
        hi, mid, lo = _split3(jnp.broadcast_to(c2[:, h:h + 1], (tm, LANES)))
        cq = jnp.where(lane == 0, hi, jnp.where(lane == 1, mid, jnp.where(
            lane == 2, lo, jnp.where(lane < 6, 1.0, 0.0))))
        ck = jnp.where(lane < 3, 1.0, jnp.where(lane == 3, -hi, jnp.where(
            lane == 4, -mid, jnp.where(lane == 5, -lo, 0.0))))
        s0 = h * HEAD_DIM
        put_t(qt_ref, c0, fq_ref[:, s0:s0 + HEAD_DIM])
        put_t(qt_ref, c0 + LANES, cq)
        kp_ref[:, c0:c0 + LANES] = fk_ref[:, s0:s0 + HEAD_DIM]
        kp_ref[:, c0 + LANES:c0 + FEAT] = ck.astype(BF16)
        put_t(vt_ref, (N_HEADS + h) * V_ROWS, fv_ref[:, s0:s0 + HEAD_DIM])
    pad = V_ROWS - HEAD_DIM
    ones_blk = jnp.where(lax.broadcasted_iota(jnp.int32, (pad, tm), 0) == 0,
                         1.0, 0.0).astype(BF16)
    for h in range(2 * N_HEADS):
        vt_ref[0, h * V_ROWS + HEAD_DIM:(h + 1) * V_ROWS, :] = ones_blk


def _prep(lat, small, rest, pos, g_q, g_kv, w_uq, w_ukv, b_f, inv_f, batch, seq):
    t = lat.shape[0]
    ns = seq // PREP_TM
    per = ATT_T // PREP_TM
    nt = t // ATT_T
    row = lambda b, i: (b * ns + i, 0)
    const = lambda b, i: (0, 0)
    tile_t = lambda b, i: ((b * ns + i) // per, 0, (b * ns + i) % per)
    return pl.pallas_call(
        _prep_kernel,
        grid=(batch, ns),
        in_specs=[
            pl.BlockSpec((PREP_TM, LAT_DIM), row),
            pl.BlockSpec((PREP_TM, SMALL_W), row),
            pl.BlockSpec((PREP_TM, BRANCH_WIDTH), lambda b, i: (b * ns + i, 1)),
            pl.BlockSpec((PREP_TM, BRANCH_WIDTH), lambda b, i: (b * ns + i, 2)),
            pl.BlockSpec((PREP_TM, BRANCH_WIDTH), lambda b, i: (b * ns + i, 3)),
            pl.BlockSpec((PREP_TM, 1), row),
            pl.BlockSpec((1, Q_RANK), const),
            pl.BlockSpec((1, KV_RANK), const),
            pl.BlockSpec((Q_RANK, N_HEADS * FEAT), const),
            pl.BlockSpec((KV_RANK, N_HEADS * FEAT), const),
            pl.BlockSpec((1, LANES), const),
            pl.BlockSpec((1, LANES), const),
        ],
        out_specs=[
            pl.BlockSpec((1, 2 * N_HEADS * FEAT, PREP_TM), tile_t),
            pl.BlockSpec((PREP_TM, 2 * N_HEADS * FEAT), row),
            pl.BlockSpec((1, 2 * N_HEADS * V_ROWS, PREP_TM), tile_t),
        ],
        out_shape=[
            jax.ShapeDtypeStruct((nt, 2 * N_HEADS * FEAT, ATT_T), BF16),
            jax.ShapeDtypeStruct((t, 2 * N_HEADS * FEAT), BF16),
            jax.ShapeDtypeStruct((nt, 2 * N_HEADS * V_ROWS, ATT_T), BF16),
        ],
        scratch_shapes=[pltpu.VMEM((1, LANES), F32)],
        compiler_params=pltpu.CompilerParams(
            dimension_semantics=("arbitrary", "arbitrary"),
            vmem_limit_bytes=VMEM_LIMIT),
        name="prep",
    )(lat, small, rest, rest, rest, pos, g_q, g_kv, w_uq, w_ukv, b_f, inv_f)


def _attn_kernel(qt_ref, k_ref, vt_ref, o_ref, s0, s1, cm0, cm1, m_scr, acc_scr):
    t = ATT_T
    nq = qt_ref.shape[0]
    nsteps = nq * (nq + 1) // 2
    s_buf, cm_buf = (s0, s1), (cm0, cm1)

    def advance(ij):
        i, j = ij
        wrap = j >= i
        return jnp.where(wrap, i + 1, i), jnp.where(wrap, 0, j + 1)

    nblk = t // ATT_BLK
    blk = ATT_BLK

    def a_begin():
        return [[None] * nblk for _ in range(ATT_HB)]

    def a_piece(ij, slot, diagonal, hh, r, cmax):
        i, j = ij
        i = jnp.minimum(i, nq - 1)
        f0, r0 = hh * FEAT, r * blk
        k = k_ref[pl.ds(pl.multiple_of(j * t + r0, blk), blk), f0:f0 + FEAT]
        c_lo = r0 if diagonal else 0
        s = jnp.dot(k, qt_ref[i, f0:f0 + FEAT, c_lo:], preferred_element_type=F32)
        if diagonal:
            key = lax.broadcasted_iota(jnp.int32, (blk, blk), 0)
            qry = lax.broadcasted_iota(jnp.int32, (blk, blk), 1)
            parts = [jnp.where(key <= qry, s[:, :blk], -jnp.inf)]
            if s.shape[1] > blk:
                parts.append(s[:, blk:])
            s = jnp.concatenate(parts, axis=1)
        s_buf[slot][hh, r0:r0 + blk, c_lo:] = s
        for qb in range(c_lo // blk, nblk):
            part = jnp.max(s[:, qb * blk - c_lo:(qb + 1) * blk - c_lo], axis=0, keepdims=True)
            cmax[hh][qb] = part if cmax[hh][qb] is None else jnp.maximum(cmax[hh][qb], part)

    def a_end(slot, hh, cmax):
        for qb in range(nblk):
            cm_buf[slot][hh, :, qb * blk:(qb + 1) * blk] = cmax[hh][qb]

    def bc_begin(ij, slot, hh):
        i, j = ij
        m_old = jnp.where(j == 0, -jnp.inf, m_scr[hh])
        m_new = jnp.maximum(m_old, cm_buf[slot][hh])
        return dict(m_new=m_new, alpha=jnp.exp2(m_old - m_new), pv=None)

    def bc_piece(ij, slot, hh, r, st):
        i, j = ij
        d0, r0 = hh * V_ROWS, r * blk
        p = jnp.exp2(s_buf[slot][hh, r0:r0 + blk, :] - st["m_new"])
        pv = jnp.dot(vt_ref[j, d0:d0 + V_ROWS, r0:r0 + blk], p.astype(BF16),
                     preferred_element_type=F32)
        st["pv"] = pv if st["pv"] is None else st["pv"] + pv

    def bc_end(hh, st):
        m_scr[hh] = st["m_new"]
        acc_scr[hh] = st["alpha"] * acc_scr[hh] + st["pv"]

    def bc_diag_piece(ij, slot, hh, qb):
        i, j = ij
        d0, rows, c0 = hh * V_ROWS, (qb + 1) * blk, qb * blk
        cols = slice(c0, c0 + blk)
        m_old = jnp.where(j == 0, -jnp.inf, m_scr[hh, :, cols])
        m_new = jnp.maximum(m_old, cm_buf[slot][hh, :, cols])
        alpha = jnp.exp2(m_old - m_new)
        p = jnp.exp2(s_buf[slot][hh, 0:rows, cols] - m_new)
        pv = jnp.dot(vt_ref[j, d0:d0 + V_ROWS, 0:rows], p.astype(BF16),
                     preferred_element_type=F32)
        acc = alpha * acc_scr[hh, :, cols] + pv
        out = acc[:HEAD_DIM] / acc[HEAD_DIM:HEAD_DIM + 1]
        o_ref[pl.ds(pl.multiple_of(i * t + c0, blk), blk),
              hh * HEAD_DIM:(hh + 1) * HEAD_DIM] = out.T.astype(BF16)

    def run(slot_a, ija, ijb, a_is_diag, b_is_diag):
        slot_b = 1 - slot_a
        cmax = a_begin()
        sts = None if b_is_diag else [bc_begin(ijb, slot_b, hh) for hh in range(ATT_HB)]
        for r in range(nblk):
            for hh in range(ATT_HB):
                a_piece(ija, slot_a, a_is_diag, hh, r, cmax)
                if b_is_diag:
                    bc_diag_piece(ijb, slot_b, hh, r)
                else:
                    bc_piece(ijb, slot_b, hh, r, sts[hh])
        for hh in range(ATT_HB):
            a_end(slot_a, hh, cmax)
            if not b_is_diag:
                bc_end(hh, sts[hh])

    def iteration(slot_a, ija, ijb):
        a_diag = ija[1] == ija[0]
        b_diag = ijb[1] == ijb[0]
        pl.when(b_diag)(lambda: run(slot_a, ija, ijb, False, True))
        pl.when(a_diag)(lambda: run(slot_a, ija, ijb, True, False))
        pl.when(jnp.logical_not(jnp.logical_or(a_diag, b_diag)))(
            lambda: run(slot_a, ija, ijb, False, False))

    zero = jnp.int32(0)
    m_scr[...] = jnp.full(m_scr.shape, -jnp.inf, F32)
    acc_scr[...] = jnp.zeros(acc_scr.shape, F32)
    cmax0 = a_begin()
    for hh in range(ATT_HB):
        for r in range(nblk):
            a_piece((zero, zero), 0, True, hh, r, cmax0)
        a_end(0, hh, cmax0)


    def pair(_, carry):
        ija, ijb = carry
        iteration(1, ija, ijb)
        ija2 = advance(ija)
        iteration(0, ija2, ija)
        return advance(ija2), ija2

    lax.fori_loop(0, nsteps // 2, pair, (advance((zero, zero)), (zero, zero)))


def _attn(qt, kp, vt, batch, seq):
    t = kp.shape[0]
    nq = seq // ATT_T
    assert (nq * (nq + 1) // 2) % 2 == 0
    nh = 2 * N_HEADS // ATT_HB
    row = pltpu.VMEM((ATT_HB, 1, ATT_T), F32)
    tile = pltpu.VMEM((ATT_HB, ATT_T, ATT_T), F32)
    return pl.pallas_call(
        _attn_kernel,
        grid=(batch, nh),
        in_specs=[
            pl.BlockSpec((nq, ATT_HB * FEAT, ATT_T), lambda b, h: (b, h, 0)),
            pl.BlockSpec((seq, ATT_HB * FEAT), lambda b, h: (b, h)),
            pl.BlockSpec((nq, ATT_HB * V_ROWS, ATT_T), lambda b, h: (b, h, 0)),
        ],
        out_specs=pl.BlockSpec((seq, ATT_HB * HEAD_DIM), lambda b, h: (b, h)),
        out_shape=jax.ShapeDtypeStruct((t, D_MIX), BF16),
        scratch_shapes=[
            tile, tile, row, row, row,
            pltpu.VMEM((ATT_HB, V_ROWS, ATT_T), F32),
        ],
        compiler_params=pltpu.CompilerParams(
            dimension_semantics=("arbitrary", "arbitrary"),
            vmem_limit_bytes=VMEM_LIMIT),
        name="attn",
    )(qt, kp, vt)


def _silu(g):
    return g / (1.0 + jnp.exp(-g))


def _out_kernel(o_ref, gm_ref, gf_ref, x_ref, w_ref, g_ref, y_ref):
    o = o_ref[...].astype(F32)
    om = (o[:, :BRANCH_WIDTH] * _silu(gm_ref[...].astype(F32))).astype(BF16)
    of = (o[:, BRANCH_WIDTH:] * _silu(gf_ref[...].astype(F32))).astype(BF16)
    y = (jnp.dot(om, w_ref[:BRANCH_WIDTH, :], preferred_element_type=F32)
         + jnp.dot(of, w_ref[BRANCH_WIDTH:, :], preferred_element_type=F32))
    y_ref[...] = x_ref[...] + _rms(y, g_ref[...])


def _out(o, rest, x2, w_out, g_post):
    t = o.shape[0]
    return pl.pallas_call(
        _out_kernel,
        grid=(t // OUT_TM,),
        in_specs=[
            pl.BlockSpec((OUT_TM, D_MIX), lambda i: (i, 0)),
            pl.BlockSpec((OUT_TM, BRANCH_WIDTH), lambda i: (i, 0)),
            pl.BlockSpec((OUT_TM, BRANCH_WIDTH), lambda i: (i, 4)),
            pl.BlockSpec((OUT_TM, D_MODEL), lambda i: (i, 0)),
            pl.BlockSpec((D_MIX, D_MODEL), lambda i: (0, 0)),
            pl.BlockSpec((1, D_MODEL), lambda i: (0, 0)),
        ],
        out_specs=pl.BlockSpec((OUT_TM, D_MODEL), lambda i: (i, 0)),
        out_shape=jax.ShapeDtypeStruct((t, D_MODEL), F32),
        compiler_params=pltpu.CompilerParams(
            dimension_semantics=("arbitrary",),
            vmem_limit_bytes=VMEM_LIMIT),
        name="out",
    )(o, rest, rest, x2, w_out, g_post)


def _layer(x2, pos, inv_f, g_pre, w_in, g_q, w_uq, g_kv, w_ukv, b_forget,
           w_out, g_post, batch, seq):
    w_all = _pack_w_in(w_in)
    w_uq_p = jnp.pad(w_uq.reshape(Q_RANK, N_HEADS, QK_DIM),
                     ((0, 0), (0, 0), (0, FEAT - QK_DIM))).reshape(
                         Q_RANK, N_HEADS * FEAT).astype(BF16)
    w_ukv_b = w_ukv.astype(BF16)
    w_out_b = w_out.astype(BF16)
    b_f = jnp.pad(b_forget, (0, LANES - N_HEADS))[None, :]

    lat, small, rest = _proj(x2, g_pre[None, :], w_all)
    qt, kp, vt = _prep(lat, small, rest, pos, g_q[None, :], g_kv[None, :],
                       w_uq_p, w_ukv_b, b_f, inv_f, batch, seq)
    o = _attn(qt, kp, vt, batch, seq)
    return _out(o, rest, x2, w_out_b, g_post[None, :])


def kernel(x, positions, g_pre, w_in, g_q_latent, w_uq, g_kv_latent, w_ukv,
           b_forget, w_out, g_post):
    batch, seq, d = x.shape
    depth = g_pre.shape[0]
    x2 = x.reshape(batch * seq, d)
    pos = positions.reshape(batch * seq, 1)
    freqs = ROPE_THETA ** (-jnp.arange(0, ROPE_DIM, 2, dtype=F32) / ROPE_DIM)
    inv_f = jnp.concatenate([freqs, freqs, jnp.zeros((LANES - ROPE_DIM,), F32)])[None, :]
    for l in range(depth):
        x2 = _layer(x2, pos, inv_f, g_pre[l], w_in[l], g_q_latent[l], w_uq[l],
                    g_kv_latent[l], w_ukv[l], b_forget[l], w_out[l], g_post[l],
                    batch, seq)
    return x2.reshape(batch, seq, d)
```

```python
import functools
import math

import jax
import jax.numpy as jnp
from jax import lax
from jax.experimental import pallas as pl
from jax.experimental.pallas import tpu as pltpu

F32 = jnp.float32
BF16 = jnp.bfloat16

D_MODEL = 2048
N_HEADS = 8
HEAD_DIM = 128
NOPE_DIM = 128
ROPE_DIM = 64
ROPE_HALF = ROPE_DIM // 2
QK_DIM = NOPE_DIM + ROPE_DIM
Q_RANK = 768
KV_RANK = 512
LAT_DIM = Q_RANK + KV_RANK
BRANCH_WIDTH = N_HEADS * HEAD_DIM
D_MIX = 2 * BRANCH_WIDTH
ROPE_THETA = 10000.0
NORM_EPS = 1e-6
LOG2E = math.log2(math.e)

LANES = 128
FEAT = 256
SMALL_W = 2 * LANES
REST_W = 5 * BRANCH_WIDTH

MLA_QSCALE = QK_DIM ** -0.5 * LOG2E
FOX_QSCALE = HEAD_DIM ** -0.5 * LOG2E

PACK_TN = 256
PROJ_TM = 1024
PREP_TM = 512
ATT_T = 1024
ATT_BLK = 256
ATT_HB = 2
V_ROWS = HEAD_DIM + 16
OUT_TM = 512

VMEM_LIMIT = 56 * 1024 * 1024


def _rms(x, g):
    return x * lax.rsqrt(jnp.mean(x * x, axis=-1, keepdims=True) + NORM_EPS) * g


def _pack_kernel(w_ref, o_ref):
    o_kr = LAT_DIM
    o_gm = o_kr + ROPE_DIM
    o_fl = o_gm + 4 * BRANCH_WIDTH
    o_gf = o_fl + N_HEADS
    cols = w_ref.shape[1]
    zeros = lambda n: jnp.zeros((n, cols), F32)
    o_ref[...] = jnp.concatenate([
        w_ref[o_gm:o_fl, :], w_ref[o_gf:o_gf + BRANCH_WIDTH, :],
        w_ref[:LAT_DIM, :],
        w_ref[o_kr:o_gm, :], zeros(LANES - ROPE_DIM),
        w_ref[o_fl:o_gf, :], zeros(LANES - N_HEADS),
    ], axis=0).astype(BF16)


def _pack_w_in(w_in_t):
    d_in, d = w_in_t.shape
    height = REST_W + LAT_DIM + SMALL_W
    return pl.pallas_call(
        _pack_kernel,
        grid=(d // PACK_TN,),
        in_specs=[pl.BlockSpec((d_in, PACK_TN), lambda i: (0, i))],
        out_specs=pl.BlockSpec((height, PACK_TN), lambda i: (0, i)),
        out_shape=jax.ShapeDtypeStruct((height, d), BF16),
        compiler_params=pltpu.CompilerParams(
            dimension_semantics=("arbitrary",),
            vmem_limit_bytes=VMEM_LIMIT),
        name="pack",
    )(w_in_t)


def _dot_nt(a, b):
    return lax.dot_general(a, b, (((1,), (1,)), ((), ())), preferred_element_type=F32)


def _proj_kernel(x_ref, g_ref, wlat_ref, wsm_ref, wrest_ref,
                 lat_ref, sm_ref, rest_ref, h_scr):
    j = pl.program_id(1)

    @pl.when(j == 0)
    def _():
        hb = _rms(x_ref[...], g_ref[...]).astype(BF16)
        h_scr[...] = hb
        lat_ref[...] = _dot_nt(hb, wlat_ref[...]).astype(BF16)
        sm_ref[...] = _dot_nt(hb, wsm_ref[...])

    acc = _dot_nt(h_scr[...], wrest_ref[...])
    scale = jnp.where(j == 1, FOX_QSCALE, 1.0).astype(F32)
    rest_ref[...] = (acc * scale).astype(BF16)


def _proj(x2, g_pre, w_all):
    t = x2.shape[0]
    n_rest = REST_W // BRANCH_WIDTH
    assert REST_W % LAT_DIM == 0 and (REST_W + LAT_DIM) % SMALL_W == 0
    once = pl.Buffered(1)
    return pl.pallas_call(
        _proj_kernel,
        grid=(t // PROJ_TM, n_rest),
        in_specs=[
            pl.BlockSpec((PROJ_TM, D_MODEL), lambda i, j: (i, 0)),
            pl.BlockSpec((1, D_MODEL), lambda i, j: (0, 0)),
            pl.BlockSpec((LAT_DIM, D_MODEL), lambda i, j: (REST_W // LAT_DIM, 0),
                         pipeline_mode=once),
            pl.BlockSpec((SMALL_W, D_MODEL), lambda i, j: ((REST_W + LAT_DIM) // SMALL_W, 0),
                         pipeline_mode=once),
            pl.BlockSpec((BRANCH_WIDTH, D_MODEL), lambda i, j: (j, 0)),
        ],
        out_specs=[
            pl.BlockSpec((PROJ_TM, LAT_DIM), lambda i, j: (i, 0)),
            pl.BlockSpec((PROJ_TM, SMALL_W), lambda i, j: (i, 0)),
            pl.BlockSpec((PROJ_TM, BRANCH_WIDTH), lambda i, j: (i, j)),
        ],
        out_shape=[
            jax.ShapeDtypeStruct((t, LAT_DIM), BF16),
            jax.ShapeDtypeStruct((t, SMALL_W), F32),
            jax.ShapeDtypeStruct((t, REST_W), BF16),
        ],
        scratch_shapes=[pltpu.VMEM((PROJ_TM, D_MODEL), BF16)],
        compiler_params=pltpu.CompilerParams(
            dimension_semantics=("arbitrary", "arbitrary"),
            vmem_limit_bytes=VMEM_LIMIT),
        name="proj",
    )(x2, g_pre, w_all, w_all, w_all)


def _split3(x):
    hi = x.astype(BF16).astype(F32)
    r = x - hi
    mid = r.astype(BF16).astype(F32)
    lo = (r - mid).astype(BF16).astype(F32)
    return hi, mid, lo


def _prep_kernel(lat_ref, sm_ref, fq_ref, fk_ref, fv_ref, pos_ref,
                 gq_ref, gkv_ref, wuq_ref, wukv_ref, bf_ref, invf_ref,
                 qt_ref, kp_ref, vt_ref, carry_scr):
    tm = lat_ref.shape[0]

    def put_t(ref, r0, x):
        ref[0, r0:r0 + x.shape[1], :] = x.astype(F32).T.astype(BF16)

    @pl.when(pl.program_id(1) == 0)
    def _():
        carry_scr[...] = jnp.zeros_like(carry_scr)

    lat = lat_ref[...].astype(F32)
    qn = _rms(lat[:, :Q_RANK], gq_ref[...]).astype(BF16)
    kvn = _rms(lat[:, Q_RANK:], gkv_ref[...]).astype(BF16)
    q = jnp.dot(qn, wuq_ref[...], preferred_element_type=F32)
    kv = jnp.dot(kvn, wukv_ref[...], preferred_element_type=F32)
    sm = sm_ref[...]

    lane = lax.broadcasted_iota(jnp.int32, (tm, LANES), 1)
    ang = pos_ref[...].astype(F32) * invf_ref[...]
    cosv = jnp.cos(ang)
    sinv = jnp.sin(ang)
    sin_hi = jnp.where(lane < ROPE_HALF, 0.0,
                       jnp.where(lane < ROPE_DIM, sinv, 0.0))
    sin_lo = jnp.where(lane < ROPE_HALF, -sinv, 0.0)

    def rope(r):
        return (r * cosv
                + pltpu.roll(r, ROPE_HALF, 1) * sin_hi
                + pltpu.roll(r, LANES - ROPE_HALF, 1) * sin_lo)

    k_rope = rope(sm[:, :LANES]).astype(BF16)
    for h in range(N_HEADS):
        c0 = h * FEAT
        put_t(qt_ref, c0, q[:, c0:c0 + LANES] * MLA_QSCALE)
        put_t(qt_ref, c0 + LANES, rope(q[:, c0 + LANES:c0 + FEAT]) * MLA_QSCALE)
        kp_ref[:, c0:c0 + LANES] = kv[:, c0:c0 + LANES].astype(BF16)
        kp_ref[:, c0 + LANES:c0 + FEAT] = k_rope
        put_t(vt_ref, h * V_ROWS, kv[:, c0 + LANES:c0 + FEAT])

    fl = sm[:, LANES:] + bf_ref[...]
    logf = jnp.minimum(fl, 0.0) - jnp.log1p(jnp.exp(-jnp.abs(fl)))
    row = lax.broadcasted_iota(jnp.int32, (tm, tm), 0)
    col = lax.broadcasted_iota(jnp.int32, (tm, tm), 1)
    tri = jnp.where(row >= col, 1.0, 0.0).astype(BF16)
    c = carry_scr[...]
    for piece in _split3(logf):
        c = c + jnp.dot(tri, piece.astype(BF16), preferred_element_type=F32)
    carry_scr[...] = c[tm - 1:tm, :]
    c2 = c * LOG2E

    for h in range(N_HEADS):
        c0 = (N_HEADS + h) * FEAT
        hi, mid, lo = _split3(jnp.broadcast_to(c2[:, h:h + 1], (tm, LANES)))
        cq = jnp.where(lane == 0, hi, jnp.where(lane == 1, mid, jnp.where(
            lane == 2, lo, jnp.where(lane < 6, 1.0, 0.0))))
        ck = jnp.where(lane < 3, 1.0, jnp.where(lane == 3, -hi, jnp.where(
            lane == 4, -mid, jnp.where(lane == 5, -lo, 0.0))))
        s0 = h * HEAD_DIM
        put_t(qt_ref, c0, fq_ref[:, s0:s0 + HEAD_DIM])
        put_t(qt_ref, c0 + LANES, cq)
        kp_ref[:, c0:c0 + LANES] = fk_ref[:, s0:s0 + HEAD_DIM]
        kp_ref[:, c0 + LANES:c0 + FEAT] = ck.astype(BF16)
        put_t(vt_ref, (N_HEADS + h) * V_ROWS, fv_ref[:, s0:s0 + HEAD_DIM])
    pad = V_ROWS - HEAD_DIM
    ones_blk = jnp.where(lax.broadcasted_iota(jnp.int32, (pad, tm), 0) == 0,
                         1.0, 0.0).astype(BF16)
    for h in range(2 * N_HEADS):
        vt_ref[0, h * V_ROWS + HEAD_DIM:(h + 1) * V_ROWS, :] = ones_blk


def _prep(lat, small, rest, pos, g_q, g_kv, w_uq, w_ukv, b_f, inv_f, batch, seq):
    t = lat.shape[0]
    ns = seq // PREP_TM
    per = ATT_T // PREP_TM
    nt = t // ATT_T
    row = lambda b, i: (b * ns + i, 0)
    const = lambda b, i: (0, 0)
    tile_t = lambda b, i: ((b * ns + i) // per, 0, (b * ns + i) % per)
    return pl.pallas_call(
        _prep_kernel,
        grid=(batch, ns),
        in_specs=[
            pl.BlockSpec((PREP_TM, LAT_DIM), row),
            pl.BlockSpec((PREP_TM, SMALL_W), row),
            pl.BlockSpec((PREP_TM, BRANCH_WIDTH), lambda b, i: (b * ns + i, 1)),
            pl.BlockSpec((PREP_TM, BRANCH_WIDTH), lambda b, i: (b * ns + i, 2)),
            pl.BlockSpec((PREP_TM, BRANCH_WIDTH), lambda b, i: (b * ns + i, 3)),
            pl.BlockSpec((PREP_TM, 1), row),
            pl.BlockSpec((1, Q_RANK), const),
            pl.BlockSpec((1, KV_RANK), const),
            pl.BlockSpec((Q_RANK, N_HEADS * FEAT), const),
            pl.BlockSpec((KV_RANK, N_HEADS * FEAT), const),
            pl.BlockSpec((1, LANES), const),
            pl.BlockSpec((1, LANES), const),
        ],
        out_specs=[
            pl.BlockSpec((1, 2 * N_HEADS * FEAT, PREP_TM), tile_t),
            pl.BlockSpec((PREP_TM, 2 * N_HEADS * FEAT), row),
            pl.BlockSpec((1, 2 * N_HEADS * V_ROWS, PREP_TM), tile_t),
        ],
        out_shape=[
            jax.ShapeDtypeStruct((nt, 2 * N_HEADS * FEAT, ATT_T), BF16),
            jax.ShapeDtypeStruct((t, 2 * N_HEADS * FEAT), BF16),
            jax.ShapeDtypeStruct((nt, 2 * N_HEADS * V_ROWS, ATT_T), BF16),
        ],
        scratch_shapes=[pltpu.VMEM((1, LANES), F32)],
        compiler_params=pltpu.CompilerParams(
            dimension_semantics=("arbitrary", "arbitrary"),
            vmem_limit_bytes=VMEM_LIMIT),
        name="prep",
    )(lat, small, rest, rest, rest, pos, g_q, g_kv, w_uq, w_ukv, b_f, inv_f)


def _attn_kernel(qt_ref, k_ref, vt_ref, o_ref, s0, s1, cm0, cm1, m_scr, acc_scr):
    t = ATT_T
    nq = qt_ref.shape[0]
    nsteps = nq * (nq + 1) // 2
    s_buf, cm_buf = (s0, s1), (cm0, cm1)

    def advance(ij):
        i, j = ij
        wrap = j >= i
        return jnp.where(wrap, i + 1, i), jnp.where(wrap, 0, j + 1)

    nblk = t // ATT_BLK
    blk = ATT_BLK

    def a_begin():
        return [[None] * nblk for _ in range(ATT_HB)]

    def a_piece(ij, slot, diagonal, hh, r, cmax):
        i, j = ij
        i = jnp.minimum(i, nq - 1)
        f0, r0 = hh * FEAT, r * blk
        k = k_ref[pl.ds(pl.multiple_of(j * t + r0, blk), blk), f0:f0 + FEAT]
        c_lo = r0 if diagonal else 0
        s = jnp.dot(k, qt_ref[i, f0:f0 + FEAT, c_lo:], preferred_element_type=F32)
        if diagonal:
            key = lax.broadcasted_iota(jnp.int32, (blk, blk), 0)
            qry = lax.broadcasted_iota(jnp.int32, (blk, blk), 1)
            parts = [jnp.where(key <= qry, s[:, :blk], -jnp.inf)]
            if s.shape[1] > blk:
                parts.append(s[:, blk:])
            s = jnp.concatenate(parts, axis=1)
        s_buf[slot][hh, r0:r0 + blk, c_lo:] = s
        for qb in range(c_lo // blk, nblk):
            part = jnp.max(s[:, qb * blk - c_lo:(qb + 1) * blk - c_lo], axis=0, keepdims=True)
            cmax[hh][qb] = part if cmax[hh][qb] is None else jnp.maximum(cmax[hh][qb], part)

    def a_end(slot, hh, cmax):
        for qb in range(nblk):
            cm_buf[slot][hh, :, qb * blk:(qb + 1) * blk] = cmax[hh][qb]

    def bc_begin(ij, slot, hh):
        i, j = ij
        m_old = jnp.where(j == 0, -jnp.inf, m_scr[hh])
        m_new = jnp.maximum(m_old, cm_buf[slot][hh])
        return dict(m_new=m_new, alpha=jnp.exp2(m_old - m_new), pv=None)

    def bc_piece(ij, slot, hh, r, st):
        i, j = ij
        d0, r0 = hh * V_ROWS, r * blk
        p = jnp.exp2(s_buf[slot][hh, r0:r0 + blk, :] - st["m_new"])
        pv = jnp.dot(vt_ref[j, d0:d0 + V_ROWS, r0:r0 + blk], p.astype(BF16),
                     preferred_element_type=F32)
        st["pv"] = pv if st["pv"] is None else st["pv"] + pv

    def bc_end(hh, st):
        m_scr[hh] = st["m_new"]
        acc_scr[hh] = st["alpha"] * acc_scr[hh] + st["pv"]

    def bc_diag_piece(ij, slot, hh, qb):
        i, j = ij
        d0, rows, c0 = hh * V_ROWS, (qb + 1) * blk, qb * blk
        cols = slice(c0, c0 + blk)
        m_old = jnp.where(j == 0, -jnp.inf, m_scr[hh, :, cols])
        m_new = jnp.maximum(m_old, cm_buf[slot][hh, :, cols])
        alpha = jnp.exp2(m_old - m_new)
        p = jnp.exp2(s_buf[slot][hh, 0:rows, cols] - m_new)
        pv = jnp.dot(vt_ref[j, d0:d0 + V_ROWS, 0:rows], p.astype(BF16),
                     preferred_element_type=F32)
        acc = alpha * acc_scr[hh, :, cols] + pv
        out = acc[:HEAD_DIM] / acc[HEAD_DIM:HEAD_DIM + 1]
        o_ref[pl.ds(pl.multiple_of(i * t + c0, blk), blk),
              hh * HEAD_DIM:(hh + 1) * HEAD_DIM] = out.T.astype(BF16)

    def run(slot_a, ija, ijb, a_is_diag, b_is_diag):
        slot_b = 1 - slot_a
        cmax = a_begin()
        sts = None if b_is_diag else [bc_begin(ijb, slot_b, hh) for hh in range(ATT_HB)]
        for r in range(nblk):
            for hh in range(ATT_HB):
                a_piece(ija, slot_a, a_is_diag, hh, r, cmax)
                if b_is_diag:
                    bc_diag_piece(ijb, slot_b, hh, r)
                else:
                    bc_piece(ijb, slot_b, hh, r, sts[hh])
        for hh in range(ATT_HB):
            a_end(slot_a, hh, cmax)
            if not b_is_diag:
                bc_end(hh, sts[hh])

    def iteration(slot_a, ija, ijb):
        a_diag = ija[1] == ija[0]
        b_diag = ijb[1] == ijb[0]
        pl.when(b_diag)(lambda: run(slot_a, ija, ijb, False, True))
        pl.when(a_diag)(lambda: run(slot_a, ija, ijb, True, False))
        pl.when(jnp.logical_not(jnp.logical_or(a_diag, b_diag)))(
            lambda: run(slot_a, ija, ijb, False, False))

    zero = jnp.int32(0)
    m_scr[...] = jnp.full(m_scr.shape, -jnp.inf, F32)
    acc_scr[...] = jnp.zeros(acc_scr.shape, F32)
    cmax0 = a_begin()
    for hh in range(ATT_HB):
        for r in range(nblk):
            a_piece((zero, zero), 0, True, hh, r, cmax0)
        a_end(0, hh, cmax0)


    def pair(_, carry):
        ija, ijb = carry
        iteration(1, ija, ijb)
        ija2 = advance(ija)
        iteration(0, ija2, ija)
        return advance(ija2), ija2

    lax.fori_loop(0, nsteps // 2, pair, (advance((zero, zero)), (zero, zero)))


def _attn(qt, kp, vt, batch, seq):
    t = kp.shape[0]
    nq = seq // ATT_T
    assert (nq * (nq + 1) // 2) % 2 == 0
    nh = 2 * N_HEADS // ATT_HB
    row = pltpu.VMEM((ATT_HB, 1, ATT_T), F32)
    tile = pltpu.VMEM((ATT_HB, ATT_T, ATT_T), F32)
    return pl.pallas_call(
        _attn_kernel,
        grid=(batch, nh),
        in_specs=[
            pl.BlockSpec((nq, ATT_HB * FEAT, ATT_T), lambda b, h: (b, h, 0)),
            pl.BlockSpec((seq, ATT_HB * FEAT), lambda b, h: (b, h)),
            pl.BlockSpec((nq, ATT_HB * V_ROWS, ATT_T), lambda b, h: (b, h, 0)),
        ],
        out_specs=pl.BlockSpec((seq, ATT_HB * HEAD_DIM), lambda b, h: (b, h)),
        out_shape=jax.ShapeDtypeStruct((t, D_MIX), BF16),
        scratch_shapes=[
            tile, tile, row, row, row,
            pltpu.VMEM((ATT_HB, V_ROWS, ATT_T), F32),
        ],
        compiler_params=pltpu.CompilerParams(
            dimension_semantics=("arbitrary", "arbitrary"),
            vmem_limit_bytes=VMEM_LIMIT),
        name="attn",
    )(qt, kp, vt)


def _silu(g):
    return g / (1.0 + jnp.exp(-g))


def _out_kernel(o_ref, gm_ref, gf_ref, x_ref, w_ref, g_ref, y_ref):
    o = o_ref[...].astype(F32)
    om = (o[:, :BRANCH_WIDTH] * _silu(gm_ref[...].astype(F32))).astype(BF16)
    of = (o[:, BRANCH_WIDTH:] * _silu(gf_ref[...].astype(F32))).astype(BF16)
    y = (jnp.dot(om, w_ref[:BRANCH_WIDTH, :], preferred_element_type=F32)
         + jnp.dot(of, w_ref[BRANCH_WIDTH:, :], preferred_element_type=F32))
    y_ref[...] = x_ref[...] + _rms(y, g_ref[...])


def _out(o, rest, x2, w_out, g_post):
    t = o.shape[0]
    return pl.pallas_call(
        _out_kernel,
        grid=(t // OUT_TM,),
        in_specs=[
            pl.BlockSpec((OUT_TM, D_MIX), lambda i: (i, 0)),
            pl.BlockSpec((OUT_TM, BRANCH_WIDTH), lambda i: (i, 0)),
            pl.BlockSpec((OUT_TM, BRANCH_WIDTH), lambda i: (i, 4)),
            pl.BlockSpec((OUT_TM, D_MODEL), lambda i: (i, 0)),
            pl.BlockSpec((D_MIX, D_MODEL), lambda i: (0, 0)),
            pl.BlockSpec((1, D_MODEL), lambda i: (0, 0)),
        ],
        out_specs=pl.BlockSpec((OUT_TM, D_MODEL), lambda i: (i, 0)),
        out_shape=jax.ShapeDtypeStruct((t, D_MODEL), F32),
        compiler_params=pltpu.CompilerParams(
            dimension_semantics=("arbitrary",),
            vmem_limit_bytes=VMEM_LIMIT),
        name="out",
    )(o, rest, rest, x2, w_out, g_post)


def _layer(x2, pos, inv_f, g_pre, w_in, g_q, w_uq, g_kv, w_ukv, b_forget,
           w_out, g_post, batch, seq):
    w_all = _pack_w_in(jnp.swapaxes(w_in, 0, 1))
    w_uq_p = jnp.pad(w_uq.reshape(Q_RANK, N_HEADS, QK_DIM),
                     ((0, 0), (0, 0), (0, FEAT - QK_DIM))).reshape(
                         Q_RANK, N_HEADS * FEAT).astype(BF16)
    w_ukv_b = w_ukv.astype(BF16)
    w_out_b = w_out.astype(BF16)
    b_f = jnp.pad(b_forget, (0, LANES - N_HEADS))[None, :]

    lat, small, rest = _proj(x2, g_pre[None, :], w_all)
    qt, kp, vt = _prep(lat, small, rest, pos, g_q[None, :], g_kv[None, :],
                       w_uq_p, w_ukv_b, b_f, inv_f, batch, seq)
    o = _attn(qt, kp, vt, batch, seq)
    return _out(o, rest, x2, w_out_b, g_post[None, :])


def kernel(x, positions, g_pre, w_in, g_q_latent, w_uq, g_kv_latent, w_ukv,
           b_forget, w_out, g_post):
    batch, seq, d = x.shape
    depth = g_pre.shape[0]
    x2 = x.reshape(batch * seq, d)
    pos = positions.reshape(batch * seq, 1)
    freqs = ROPE_THETA ** (-jnp.arange(0, ROPE_DIM, 2, dtype=F32) / ROPE_DIM)
    inv_f = jnp.concatenate([freqs, freqs, jnp.zeros((LANES - ROPE_DIM,), F32)])[None, :]
    for l in range(depth):
        x2 = _layer(x2, pos, inv_f, g_pre[l], w_in[l], g_q_latent[l], w_uq[l],
                    g_kv_latent[l], w_ukv[l], b_forget[l], w_out[l], g_post[l],
                    batch, seq)
    return x2.reshape(batch, seq, d)
```

```python
import functools
import math

import jax
import jax.numpy as jnp
from jax import lax
from jax.experimental import pallas as pl
from jax.experimental.pallas import tpu as pltpu

F32 = jnp.float32
BF16 = jnp.bfloat16

D_MODEL = 2048
N_HEADS = 8
HEAD_DIM = 128
NOPE_DIM = 128
ROPE_DIM = 64
ROPE_HALF = ROPE_DIM // 2
QK_DIM = NOPE_DIM + ROPE_DIM
Q_RANK = 768
KV_RANK = 512
LAT_DIM = Q_RANK + KV_RANK
BRANCH_WIDTH = N_HEADS * HEAD_DIM
D_MIX = 2 * BRANCH_WIDTH
ROPE_THETA = 10000.0
NORM_EPS = 1e-6
LOG2E = math.log2(math.e)

LANES = 128
FEAT = 256
SMALL_W = 2 * LANES
REST_W = 5 * BRANCH_WIDTH

MLA_QSCALE = QK_DIM ** -0.5 * LOG2E
FOX_QSCALE = HEAD_DIM ** -0.5 * LOG2E

PACK_TN = 256
PROJ_TM = 1024
PREP_TM = 512
ATT_T = 1024
ATT_BLK = 256
ATT_HB = 2
V_ROWS = HEAD_DIM + 16
OUT_TM = 512

VMEM_LIMIT = 56 * 1024 * 1024


def _rms(x, g):
    return x * lax.rsqrt(jnp.mean(x * x, axis=-1, keepdims=True) + NORM_EPS) * g


def _pack_kernel(w_ref, o_ref):
    o_kr = LAT_DIM
    o_gm = o_kr + ROPE_DIM
    o_fl = o_gm + 4 * BRANCH_WIDTH
    o_gf = o_fl + N_HEADS
    cols = w_ref.shape[1]
    zeros = lambda n: jnp.zeros((n, cols), F32)
    o_ref[...] = jnp.concatenate([
        w_ref[o_gm:o_fl, :], w_ref[o_gf:o_gf + BRANCH_WIDTH, :],
        w_ref[:LAT_DIM, :],
        w_ref[o_kr:o_gm, :], zeros(LANES - ROPE_DIM),
        w_ref[o_fl:o_gf, :], zeros(LANES - N_HEADS),
    ], axis=0).astype(BF16)


def _pack_w_in(w_in_t):
    d_in, d = w_in_t.shape
    height = REST_W + LAT_DIM + SMALL_W
    return pl.pallas_call(
        _pack_kernel,
        grid=(d // PACK_TN,),
        in_specs=[pl.BlockSpec((d_in, PACK_TN), lambda i: (0, i))],
        out_specs=pl.BlockSpec((height, PACK_TN), lambda i: (0, i)),
        out_shape=jax.ShapeDtypeStruct((height, d), BF16),
        compiler_params=pltpu.CompilerParams(
            dimension_semantics=("arbitrary",),
            vmem_limit_bytes=VMEM_LIMIT),
        name="pack",
    )(w_in_t)


def _dot_nt(a, b):
    return lax.dot_general(a, b, (((1,), (1,)), ((), ())), preferred_element_type=F32)


def _proj_kernel(x_ref, g_ref, wlat_ref, wsm_ref, wrest_ref,
                 lat_ref, sm_ref, rest_ref, h_scr):
    j = pl.program_id(1)

    @pl.when(j == 0)
    def _():
        hb = _rms(x_ref[...], g_ref[...]).astype(BF16)
        h_scr[...] = hb
        lat_ref[...] = _dot_nt(hb, wlat_ref[...]).astype(BF16)
        sm_ref[...] = _dot_nt(hb, wsm_ref[...])

    acc = _dot_nt(h_scr[...], wrest_ref[...])
    scale = jnp.where(j == 1, FOX_QSCALE, 1.0).astype(F32)
    rest_ref[...] = (acc * scale).astype(BF16)


def _proj(x2, g_pre, w_all):
    t = x2.shape[0]
    n_rest = REST_W // BRANCH_WIDTH
    assert REST_W % LAT_DIM == 0 and (REST_W + LAT_DIM) % SMALL_W == 0
    once = pl.Buffered(1)
    return pl.pallas_call(
        _proj_kernel,
        grid=(t // PROJ_TM, n_rest),
        in_specs=[
            pl.BlockSpec((PROJ_TM, D_MODEL), lambda i, j: (i, 0)),
            pl.BlockSpec((1, D_MODEL), lambda i, j: (0, 0)),
            pl.BlockSpec((LAT_DIM, D_MODEL), lambda i, j: (REST_W // LAT_DIM, 0),
                         pipeline_mode=once),
            pl.BlockSpec((SMALL_W, D_MODEL), lambda i, j: ((REST_W + LAT_DIM) // SMALL_W, 0),
                         pipeline_mode=once),
            pl.BlockSpec((BRANCH_WIDTH, D_MODEL), lambda i, j: (j, 0)),
        ],
        out_specs=[
            pl.BlockSpec((PROJ_TM, LAT_DIM), lambda i, j: (i, 0)),
            pl.BlockSpec((PROJ_TM, SMALL_W), lambda i, j: (i, 0)),
            pl.BlockSpec((PROJ_TM, BRANCH_WIDTH), lambda i, j: (i, j)),
        ],
        out_shape=[
            jax.ShapeDtypeStruct((t, LAT_DIM), BF16),
            jax.ShapeDtypeStruct((t, SMALL_W), F32),
            jax.ShapeDtypeStruct((t, REST_W), BF16),
        ],
        scratch_shapes=[pltpu.VMEM((PROJ_TM, D_MODEL), BF16)],
        compiler_params=pltpu.CompilerParams(
            dimension_semantics=("arbitrary", "arbitrary"),
            vmem_limit_bytes=VMEM_LIMIT),
        name="proj",
    )(x2, g_pre, w_all, w_all, w_all)


def _split3(x):
    hi = x.astype(BF16).astype(F32)
    r = x - hi
    mid = r.astype(BF16).astype(F32)
    lo = (r - mid).astype(BF16).astype(F32)
    return hi, mid, lo


def _prep_kernel(lat_ref, sm_ref, fq_ref, fk_ref, fv_ref, pos_ref,
                 gq_ref, gkv_ref, wuqt_ref, wk_ref, wvt_ref, bf_ref, invf_ref,
                 qt_ref, kp_ref, vt_ref, carry_scr):
    tm = lat_ref.shape[0]

    def put_t(ref, r0, x):
        ref[0, r0:r0 + x.shape[1], :] = x.astype(F32).T.astype(BF16)

    @pl.when(pl.program_id(1) == 0)
    def _():
        carry_scr[...] = jnp.zeros_like(carry_scr)

    lat = lat_ref[...].astype(F32)
    qn = _rms(lat[:, :Q_RANK], gq_ref[...]).astype(BF16)
    kvn = _rms(lat[:, Q_RANK:], gkv_ref[...]).astype(BF16)
    q_t = _dot_nt(wuqt_ref[...], qn) * MLA_QSCALE
    k_nope = jnp.dot(kvn, wk_ref[...], preferred_element_type=F32)
    v_t = _dot_nt(wvt_ref[...], kvn)

    ang = invf_ref[...] * pos_ref[0].astype(F32)
    cos_t = jnp.cos(ang)
    sin_t = jnp.sin(ang)

    def rope_t(x):
        x1, x2 = x[:ROPE_HALF], x[ROPE_HALF:ROPE_DIM]
        return x1 * cos_t - x2 * sin_t, x2 * cos_t + x1 * sin_t

    kr1, kr2 = rope_t(sm_ref[:, :LANES].T)
    k_rope = jnp.concatenate(
        [kr1, kr2, jnp.zeros((LANES - ROPE_DIM, tm), F32)], axis=0).T.astype(BF16)
    q_pad = jnp.zeros((FEAT - QK_DIM, tm), BF16)
    for h in range(N_HEADS):
        c0, d0 = h * FEAT, h * HEAD_DIM
        qr1, qr2 = rope_t(q_t[c0 + NOPE_DIM:c0 + QK_DIM])
        qt_ref[0, c0:c0 + NOPE_DIM, :] = q_t[c0:c0 + NOPE_DIM].astype(BF16)
        qt_ref[0, c0 + NOPE_DIM:c0 + NOPE_DIM + ROPE_HALF, :] = qr1.astype(BF16)
        qt_ref[0, c0 + NOPE_DIM + ROPE_HALF:c0 + QK_DIM, :] = qr2.astype(BF16)
        qt_ref[0, c0 + QK_DIM:c0 + FEAT, :] = q_pad
        kp_ref[:, c0:c0 + LANES] = k_nope[:, d0:d0 + HEAD_DIM].astype(BF16)
        kp_ref[:, c0 + LANES:c0 + FEAT] = k_rope
        vt_ref[0, h * V_ROWS:h * V_ROWS + HEAD_DIM, :] = v_t[d0:d0 + HEAD_DIM].astype(BF16)

    lane = lax.broadcasted_iota(jnp.int32, (tm, LANES), 1)
    sm = sm_ref[...]
    fl = sm[:, LANES:] + bf_ref[...]
    logf = jnp.minimum(fl, 0.0) - jnp.log1p(jnp.exp(-jnp.abs(fl)))
    row = lax.broadcasted_iota(jnp.int32, (tm, tm), 0)
    col = lax.broadcasted_iota(jnp.int32, (tm, tm), 1)
    tri = jnp.where(row >= col, 1.0, 0.0).astype(BF16)
    c = carry_scr[...]
    for piece in _split3(logf):
        c = c + jnp.dot(tri, piece.astype(BF16), preferred_element_type=F32)
    carry_scr[...] = c[tm - 1:tm, :]
    c2 = c * LOG2E
    c2_t = c2.T

    aug = 16
    srow = lax.broadcasted_iota(jnp.int32, (aug, tm), 0)
    q_zero = jnp.zeros((FEAT - HEAD_DIM - aug, tm), BF16)
    for h in range(N_HEADS):
        c0 = (N_HEADS + h) * FEAT
        s0 = h * HEAD_DIM
        hi, mid, lo = _split3(jnp.broadcast_to(c2[:, h:h + 1], (tm, LANES)))
        ck = jnp.where(lane < 3, 1.0, jnp.where(lane == 3, -hi, jnp.where(
            lane == 4, -mid, jnp.where(lane == 5, -lo, 0.0))))
        kp_ref[:, c0:c0 + LANES] = fk_ref[:, s0:s0 + HEAD_DIM]
        kp_ref[:, c0 + LANES:c0 + FEAT] = ck.astype(BF16)
        hi_t, mid_t, lo_t = _split3(c2_t[h:h + 1, :])
        cq_t = jnp.where(srow == 0, hi_t, jnp.where(srow == 1, mid_t, jnp.where(
            srow == 2, lo_t, jnp.where(srow < 6, 1.0, 0.0))))
        put_t(qt_ref, c0, fq_ref[:, s0:s0 + HEAD_DIM])
        qt_ref[0, c0 + HEAD_DIM:c0 + HEAD_DIM + aug, :] = cq_t.astype(BF16)
        qt_ref[0, c0 + HEAD_DIM + aug:c0 + FEAT, :] = q_zero
        put_t(vt_ref, (N_HEADS + h) * V_ROWS, fv_ref[:, s0:s0 + HEAD_DIM])
    pad = V_ROWS - HEAD_DIM
    ones_blk = jnp.where(lax.broadcasted_iota(jnp.int32, (pad, tm), 0) == 0,
                         1.0, 0.0).astype(BF16)
    for h in range(2 * N_HEADS):
        vt_ref[0, h * V_ROWS + HEAD_DIM:(h + 1) * V_ROWS, :] = ones_blk


def _prep(lat, small, rest, pos, g_q, g_kv, w_uq_t, w_k, w_v_t, b_f, inv_f, batch, seq):
    t = lat.shape[0]
    ns = seq // PREP_TM
    per = ATT_T // PREP_TM
    nt = t // ATT_T
    row = lambda b, i: (b * ns + i, 0)
    const = lambda b, i: (0, 0)
    tile_t = lambda b, i: ((b * ns + i) // per, 0, (b * ns + i) % per)
    return pl.pallas_call(
        _prep_kernel,
        grid=(batch, ns),
        in_specs=[
            pl.BlockSpec((PREP_TM, LAT_DIM), row),
            pl.BlockSpec((PREP_TM, SMALL_W), row),
            pl.BlockSpec((PREP_TM, BRANCH_WIDTH), lambda b, i: (b * ns + i, 1)),
            pl.BlockSpec((PREP_TM, BRANCH_WIDTH), lambda b, i: (b * ns + i, 2)),
            pl.BlockSpec((PREP_TM, BRANCH_WIDTH), lambda b, i: (b * ns + i, 3)),
            pl.BlockSpec((1, 1, PREP_TM), lambda b, i: (b * ns + i, 0, 0)),
            pl.BlockSpec((1, Q_RANK), const),
            pl.BlockSpec((1, KV_RANK), const),
            pl.BlockSpec((N_HEADS * FEAT, Q_RANK), const),
            pl.BlockSpec((KV_RANK, BRANCH_WIDTH), const),
            pl.BlockSpec((BRANCH_WIDTH, KV_RANK), const),
            pl.BlockSpec((1, LANES), const),
            pl.BlockSpec((ROPE_HALF, 1), const),
        ],
        out_specs=[
            pl.BlockSpec((1, 2 * N_HEADS * FEAT, PREP_TM), tile_t),
            pl.BlockSpec((PREP_TM, 2 * N_HEADS * FEAT), row),
            pl.BlockSpec((1, 2 * N_HEADS * V_ROWS, PREP_TM), tile_t),
        ],
        out_shape=[
            jax.ShapeDtypeStruct((nt, 2 * N_HEADS * FEAT, ATT_T), BF16),
            jax.ShapeDtypeStruct((t, 2 * N_HEADS * FEAT), BF16),
            jax.ShapeDtypeStruct((nt, 2 * N_HEADS * V_ROWS, ATT_T), BF16),
        ],
        scratch_shapes=[pltpu.VMEM((1, LANES), F32)],
        compiler_params=pltpu.CompilerParams(
            dimension_semantics=("arbitrary", "arbitrary"),
            vmem_limit_bytes=VMEM_LIMIT),
        name="prep",
    )(lat, small, rest, rest, rest, pos, g_q, g_kv, w_uq_t, w_k, w_v_t, b_f, inv_f)


def _attn_kernel(qt_ref, k_ref, vt_ref, o_ref, s0, s1, cm0, cm1, m_scr, acc_scr):
    t = ATT_T
    nq = qt_ref.shape[0]
    nsteps = nq * (nq + 1) // 2
    s_buf, cm_buf = (s0, s1), (cm0, cm1)

    def advance(ij):
        i, j = ij
        wrap = j >= i
        return jnp.where(wrap, i + 1, i), jnp.where(wrap, 0, j + 1)

    nblk = t // ATT_BLK
    blk = ATT_BLK

    def a_begin():
        return [[None] * nblk for _ in range(ATT_HB)]

    def a_piece(ij, slot, diagonal, hh, r, cmax):
        i, j = ij
        i = jnp.minimum(i, nq - 1)
        f0, r0 = hh * FEAT, r * blk
        k = k_ref[pl.ds(pl.multiple_of(j * t + r0, blk), blk), f0:f0 + FEAT]
        c_lo = r0 if diagonal else 0
        s = jnp.dot(k, qt_ref[i, f0:f0 + FEAT, c_lo:], preferred_element_type=F32)
        if diagonal:
            key = lax.broadcasted_iota(jnp.int32, (blk, blk), 0)
            qry = lax.broadcasted_iota(jnp.int32, (blk, blk), 1)
            parts = [jnp.where(key <= qry, s[:, :blk], -jnp.inf)]
            if s.shape[1] > blk:
                parts.append(s[:, blk:])
            s = jnp.concatenate(parts, axis=1)
        s_buf[slot][hh, r0:r0 + blk, c_lo:] = s
        for qb in range(c_lo // blk, nblk):
            part = jnp.max(s[:, qb * blk - c_lo:(qb + 1) * blk - c_lo], axis=0, keepdims=True)
            cmax[hh][qb] = part if cmax[hh][qb] is None else jnp.maximum(cmax[hh][qb], part)

    def a_end(slot, hh, cmax):
        for qb in range(nblk):
            cm_buf[slot][hh, :, qb * blk:(qb + 1) * blk] = cmax[hh][qb]

    def bc_begin(ij, slot, hh):
        i, j = ij
        m_old = jnp.where(j == 0, -jnp.inf, m_scr[hh])
        m_new = jnp.maximum(m_old, cm_buf[slot][hh])
        return dict(m_new=m_new, alpha=jnp.exp2(m_old - m_new), pv=None)

    def bc_piece(ij, slot, hh, r, st):
        i, j = ij
        d0, r0 = hh * V_ROWS, r * blk
        p = jnp.exp2(s_buf[slot][hh, r0:r0 + blk, :] - st["m_new"])
        pv = jnp.dot(vt_ref[j, d0:d0 + V_ROWS, r0:r0 + blk], p.astype(BF16),
                     preferred_element_type=F32)
        st["pv"] = pv if st["pv"] is None else st["pv"] + pv

    def bc_end(hh, st):
        m_scr[hh] = st["m_new"]
        acc_scr[hh] = st["alpha"] * acc_scr[hh] + st["pv"]

    def bc_diag_piece(ij, slot, hh, qb):
        i, j = ij
        d0, rows, c0 = hh * V_ROWS, (qb + 1) * blk, qb * blk
        cols = slice(c0, c0 + blk)
        m_old = jnp.where(j == 0, -jnp.inf, m_scr[hh, :, cols])
        m_new = jnp.maximum(m_old, cm_buf[slot][hh, :, cols])
        alpha = jnp.exp2(m_old - m_new)
        p = jnp.exp2(s_buf[slot][hh, 0:rows, cols] - m_new)
        pv = jnp.dot(vt_ref[j, d0:d0 + V_ROWS, 0:rows], p.astype(BF16),
                     preferred_element_type=F32)
        acc = alpha * acc_scr[hh, :, cols] + pv
        out = acc[:HEAD_DIM] / acc[HEAD_DIM:HEAD_DIM + 1]
        o_ref[pl.ds(pl.multiple_of(i * t + c0, blk), blk),
              hh * HEAD_DIM:(hh + 1) * HEAD_DIM] = out.T.astype(BF16)

    def run(slot_a, ija, ijb, a_is_diag, b_is_diag):
        slot_b = 1 - slot_a
        cmax = a_begin()
        sts = None if b_is_diag else [bc_begin(ijb, slot_b, hh) for hh in range(ATT_HB)]
        for r in range(nblk):
            for hh in range(ATT_HB):
                a_piece(ija, slot_a, a_is_diag, hh, r, cmax)
                if b_is_diag:
                    bc_diag_piece(ijb, slot_b, hh, r)
                else:
                    bc_piece(ijb, slot_b, hh, r, sts[hh])
        for hh in range(ATT_HB):
            a_end(slot_a, hh, cmax)
            if not b_is_diag:
                bc_end(hh, sts[hh])

    def iteration(slot_a, ija, ijb):
        a_diag = ija[1] == ija[0]
        b_diag = ijb[1] == ijb[0]
        pl.when(b_diag)(lambda: run(slot_a, ija, ijb, False, True))
        pl.when(a_diag)(lambda: run(slot_a, ija, ijb, True, False))
        pl.when(jnp.logical_not(jnp.logical_or(a_diag, b_diag)))(
            lambda: run(slot_a, ija, ijb, False, False))

    zero = jnp.int32(0)
    m_scr[...] = jnp.full(m_scr.shape, -jnp.inf, F32)
    acc_scr[...] = jnp.zeros(acc_scr.shape, F32)
    cmax0 = a_begin()
    for hh in range(ATT_HB):
        for r in range(nblk):
            a_piece((zero, zero), 0, True, hh, r, cmax0)
        a_end(0, hh, cmax0)


    def pair(_, carry):
        ija, ijb = carry
        iteration(1, ija, ijb)
        ija2 = advance(ija)
        iteration(0, ija2, ija)
        return advance(ija2), ija2

    lax.fori_loop(0, nsteps // 2, pair, (advance((zero, zero)), (zero, zero)))


def _attn(qt, kp, vt, batch, seq):
    t = kp.shape[0]
    nq = seq // ATT_T
    assert (nq * (nq + 1) // 2) % 2 == 0
    nh = 2 * N_HEADS // ATT_HB
    row = pltpu.VMEM((ATT_HB, 1, ATT_T), F32)
    tile = pltpu.VMEM((ATT_HB, ATT_T, ATT_T), F32)
    return pl.pallas_call(
        _attn_kernel,
        grid=(batch, nh),
        in_specs=[
            pl.BlockSpec((nq, ATT_HB * FEAT, ATT_T), lambda b, h: (b, h, 0)),
            pl.BlockSpec((seq, ATT_HB * FEAT), lambda b, h: (b, h)),
            pl.BlockSpec((nq, ATT_HB * V_ROWS, ATT_T), lambda b, h: (b, h, 0)),
        ],
        out_specs=pl.BlockSpec((seq, ATT_HB * HEAD_DIM), lambda b, h: (b, h)),
        out_shape=jax.ShapeDtypeStruct((t, D_MIX), BF16),
        scratch_shapes=[
            tile, tile, row, row, row,
            pltpu.VMEM((ATT_HB, V_ROWS, ATT_T), F32),
        ],
        compiler_params=pltpu.CompilerParams(
            dimension_semantics=("arbitrary", "arbitrary"),
            vmem_limit_bytes=VMEM_LIMIT),
        name="attn",
    )(qt, kp, vt)


def _silu(g):
    return g / (1.0 + jnp.exp(-g))


def _out_kernel(o_ref, gm_ref, gf_ref, x_ref, w_ref, g_ref, y_ref):
    o = o_ref[...].astype(F32)
    om = (o[:, :BRANCH_WIDTH] * _silu(gm_ref[...].astype(F32))).astype(BF16)
    of = (o[:, BRANCH_WIDTH:] * _silu(gf_ref[...].astype(F32))).astype(BF16)
    y = (jnp.dot(om, w_ref[:BRANCH_WIDTH, :], preferred_element_type=F32)
         + jnp.dot(of, w_ref[BRANCH_WIDTH:, :], preferred_element_type=F32))
    y_ref[...] = x_ref[...] + _rms(y, g_ref[...])


def _out(o, rest, x2, w_out, g_post):
    t = o.shape[0]
    return pl.pallas_call(
        _out_kernel,
        grid=(t // OUT_TM,),
        in_specs=[
            pl.BlockSpec((OUT_TM, D_MIX), lambda i: (i, 0)),
            pl.BlockSpec((OUT_TM, BRANCH_WIDTH), lambda i: (i, 0)),
            pl.BlockSpec((OUT_TM, BRANCH_WIDTH), lambda i: (i, 4)),
            pl.BlockSpec((OUT_TM, D_MODEL), lambda i: (i, 0)),
            pl.BlockSpec((D_MIX, D_MODEL), lambda i: (0, 0)),
            pl.BlockSpec((1, D_MODEL), lambda i: (0, 0)),
        ],
        out_specs=pl.BlockSpec((OUT_TM, D_MODEL), lambda i: (i, 0)),
        out_shape=jax.ShapeDtypeStruct((t, D_MODEL), F32),
        compiler_params=pltpu.CompilerParams(
            dimension_semantics=("arbitrary",),
            vmem_limit_bytes=VMEM_LIMIT),
        name="out",
    )(o, rest, rest, x2, w_out, g_post)


def _layer(x2, pos, inv_f, g_pre, w_in, g_q, w_uq, g_kv, w_ukv, b_forget,
           w_out, g_post, batch, seq):
    w_all = _pack_w_in(jnp.swapaxes(w_in, 0, 1))
    w_uq_t = jnp.pad(w_uq.T.reshape(N_HEADS, QK_DIM, Q_RANK),
                     ((0, 0), (0, FEAT - QK_DIM), (0, 0))).reshape(
                         N_HEADS * FEAT, Q_RANK).astype(BF16)
    w_kv = w_ukv.reshape(KV_RANK, N_HEADS, NOPE_DIM + HEAD_DIM)
    w_k = w_kv[:, :, :NOPE_DIM].reshape(KV_RANK, BRANCH_WIDTH).astype(BF16)
    w_v_t = w_kv[:, :, NOPE_DIM:].reshape(KV_RANK, BRANCH_WIDTH).T.astype(BF16)
    w_out_b = w_out.astype(BF16)
    b_f = jnp.pad(b_forget, (0, LANES - N_HEADS))[None, :]

    lat, small, rest = _proj(x2, g_pre[None, :], w_all)
    qt, kp, vt = _prep(lat, small, rest, pos, g_q[None, :], g_kv[None, :],
                       w_uq_t, w_k, w_v_t, b_f, inv_f, batch, seq)
    o = _attn(qt, kp, vt, batch, seq)
    return _out(o, rest, x2, w_out_b, g_post[None, :])


def kernel(x, positions, g_pre, w_in, g_q_latent, w_uq, g_kv_latent, w_ukv,
           b_forget, w_out, g_post):
    batch, seq, d = x.shape
    depth = g_pre.shape[0]
    x2 = x.reshape(batch * seq, d)
    pos = positions.reshape(batch * seq // PREP_TM, 1, PREP_TM)
    inv_f = (ROPE_THETA ** (-jnp.arange(0, ROPE_DIM, 2, dtype=F32) / ROPE_DIM))[:, None]
    for l in range(depth):
        x2 = _layer(x2, pos, inv_f, g_pre[l], w_in[l], g_q_latent[l], w_uq[l],
                    g_kv_latent[l], w_ukv[l], b_forget[l], w_out[l], g_post[l],
                    batch, seq)
    return x2.reshape(batch, seq, d)
```

```python
import functools
import math

import jax
import jax.numpy as jnp
from jax import lax
from jax.experimental import pallas as pl
from jax.experimental.pallas import tpu as pltpu

F32 = jnp.float32
BF16 = jnp.bfloat16

D_MODEL = 2048
N_HEADS = 8
HEAD_DIM = 128
NOPE_DIM = 128
ROPE_DIM = 64
ROPE_HALF = ROPE_DIM // 2
QK_DIM = NOPE_DIM + ROPE_DIM
Q_RANK = 768
KV_RANK = 512
LAT_DIM = Q_RANK + KV_RANK
BRANCH_WIDTH = N_HEADS * HEAD_DIM
D_MIX = 2 * BRANCH_WIDTH
ROPE_THETA = 10000.0
NORM_EPS = 1e-6
LOG2E = math.log2(math.e)

LANES = 128
FEAT = 256
SMALL_W = 2 * LANES
REST_W = 5 * BRANCH_WIDTH

MLA_QSCALE = QK_DIM ** -0.5 * LOG2E
FOX_QSCALE = HEAD_DIM ** -0.5 * LOG2E

PACK_TN = 256
PROJ_TM = 1024
PREP_TM = 512
ATT_T = 1024
ATT_BLK = 256
ATT_HB = 2
V_ROWS = HEAD_DIM + 16
OUT_TM = 512

VMEM_LIMIT = 56 * 1024 * 1024


def _rms(x, g):
    return x * lax.rsqrt(jnp.mean(x * x, axis=-1, keepdims=True) + NORM_EPS) * g


def _pack_kernel(w_ref, o_ref):
    o_kr = LAT_DIM
    o_gm = o_kr + ROPE_DIM
    o_fl = o_gm + 4 * BRANCH_WIDTH
    o_gf = o_fl + N_HEADS
    cols = w_ref.shape[1]
    zeros = lambda n: jnp.zeros((n, cols), F32)
    o_ref[...] = jnp.concatenate([
        w_ref[o_gm:o_fl, :], w_ref[o_gf:o_gf + BRANCH_WIDTH, :],
        w_ref[:LAT_DIM, :],
        w_ref[o_kr:o_gm, :], zeros(LANES - ROPE_DIM),
        w_ref[o_fl:o_gf, :], zeros(LANES - N_HEADS),
    ], axis=0).astype(BF16)


def _pack_w_in(w_in_t):
    d_in, d = w_in_t.shape
    height = REST_W + LAT_DIM + SMALL_W
    return pl.pallas_call(
        _pack_kernel,
        grid=(d // PACK_TN,),
        in_specs=[pl.BlockSpec((d_in, PACK_TN), lambda i: (0, i))],
        out_specs=pl.BlockSpec((height, PACK_TN), lambda i: (0, i)),
        out_shape=jax.ShapeDtypeStruct((height, d), BF16),
        compiler_params=pltpu.CompilerParams(
            dimension_semantics=("arbitrary",),
            vmem_limit_bytes=VMEM_LIMIT),
        name="pack",
    )(w_in_t)


def _dot_nt(a, b):
    return lax.dot_general(a, b, (((1,), (1,)), ((), ())), preferred_element_type=F32)


def _proj_kernel(x_ref, g_ref, wlat_ref, wsm_ref, wrest_ref,
                 lat_ref, sm_ref, rest_ref, h_scr):
    j = pl.program_id(1)

    @pl.when(j == 0)
    def _():
        hb = _rms(x_ref[...], g_ref[...]).astype(BF16)
        h_scr[...] = hb
        lat_ref[...] = _dot_nt(hb, wlat_ref[...]).astype(BF16)
        sm_ref[...] = _dot_nt(hb, wsm_ref[...])

    acc = _dot_nt(h_scr[...], wrest_ref[...])
    scale = jnp.where(j == 1, FOX_QSCALE, 1.0).astype(F32)
    rest_ref[...] = (acc * scale).astype(BF16)


def _proj(x2, g_pre, w_all):
    t = x2.shape[0]
    n_rest = REST_W // BRANCH_WIDTH
    assert REST_W % LAT_DIM == 0 and (REST_W + LAT_DIM) % SMALL_W == 0
    once = pl.Buffered(1)
    return pl.pallas_call(
        _proj_kernel,
        grid=(t // PROJ_TM, n_rest),
        in_specs=[
            pl.BlockSpec((PROJ_TM, D_MODEL), lambda i, j: (i, 0)),
            pl.BlockSpec((1, D_MODEL), lambda i, j: (0, 0)),
            pl.BlockSpec((LAT_DIM, D_MODEL), lambda i, j: (REST_W // LAT_DIM, 0),
                         pipeline_mode=once),
            pl.BlockSpec((SMALL_W, D_MODEL), lambda i, j: ((REST_W + LAT_DIM) // SMALL_W, 0),
                         pipeline_mode=once),
            pl.BlockSpec((BRANCH_WIDTH, D_MODEL), lambda i, j: (j, 0)),
        ],
        out_specs=[
            pl.BlockSpec((PROJ_TM, LAT_DIM), lambda i, j: (i, 0)),
            pl.BlockSpec((PROJ_TM, SMALL_W), lambda i, j: (i, 0)),
            pl.BlockSpec((PROJ_TM, BRANCH_WIDTH), lambda i, j: (i, j)),
        ],
        out_shape=[
            jax.ShapeDtypeStruct((t, LAT_DIM), BF16),
            jax.ShapeDtypeStruct((t, SMALL_W), F32),
            jax.ShapeDtypeStruct((t, REST_W), BF16),
        ],
        scratch_shapes=[pltpu.VMEM((PROJ_TM, D_MODEL), BF16)],
        compiler_params=pltpu.CompilerParams(
            dimension_semantics=("arbitrary", "arbitrary"),
            vmem_limit_bytes=VMEM_LIMIT),
        name="proj",
    )(x2, g_pre, w_all, w_all, w_all)


def _split3(x):
    hi = x.astype(BF16).astype(F32)
    r = x - hi
    mid = r.astype(BF16).astype(F32)
    lo = (r - mid).astype(BF16).astype(F32)
    return hi, mid, lo


def _prep_kernel(lat_ref, sm_ref, fq_ref, fk_ref, fv_ref, pos_ref,
                 gq_ref, gkv_ref, wuqt_ref, wk_ref, wvt_ref, bf_ref, invf_ref,
                 qt_ref, kp_ref, vt_ref, carry_scr):
    tm = lat_ref.shape[0]

    def put_t(ref, r0, x):
        ref[0, r0:r0 + x.shape[1], :] = x.astype(F32).T.astype(BF16)

    @pl.when(pl.program_id(1) == 0)
    def _():
        carry_scr[...] = jnp.zeros_like(carry_scr)

    lat = lat_ref[...].astype(F32)
    qn = _rms(lat[:, :Q_RANK], gq_ref[...]).astype(BF16)
    kvn = _rms(lat[:, Q_RANK:], gkv_ref[...]).astype(BF16)
    q_t = _dot_nt(wuqt_ref[...], qn) * MLA_QSCALE
    k_nope = jnp.dot(kvn, wk_ref[...], preferred_element_type=F32)
    v_t = _dot_nt(wvt_ref[...], kvn)

    ang = invf_ref[...] * pos_ref[0].astype(F32)
    cos_t = jnp.cos(ang)
    sin_t = jnp.sin(ang)

    def rope_t(x):
        x1, x2 = x[:ROPE_HALF], x[ROPE_HALF:ROPE_DIM]
        return x1 * cos_t - x2 * sin_t, x2 * cos_t + x1 * sin_t

    kr1, kr2 = rope_t(sm_ref[:, :LANES].T)
    k_rope = jnp.concatenate(
        [kr1, kr2, jnp.zeros((LANES - ROPE_DIM, tm), F32)], axis=0).T.astype(BF16)
    q_pad = jnp.zeros((FEAT - QK_DIM, tm), BF16)
    for h in range(N_HEADS):
        c0, d0 = h * FEAT, h * HEAD_DIM
        qr1, qr2 = rope_t(q_t[c0 + NOPE_DIM:c0 + QK_DIM])
        qt_ref[0, c0:c0 + NOPE_DIM, :] = q_t[c0:c0 + NOPE_DIM].astype(BF16)
        qt_ref[0, c0 + NOPE_DIM:c0 + NOPE_DIM + ROPE_HALF, :] = qr1.astype(BF16)
        qt_ref[0, c0 + NOPE_DIM + ROPE_HALF:c0 + QK_DIM, :] = qr2.astype(BF16)
        qt_ref[0, c0 + QK_DIM:c0 + FEAT, :] = q_pad
        kp_ref[:, c0:c0 + LANES] = k_nope[:, d0:d0 + HEAD_DIM].astype(BF16)
        kp_ref[:, c0 + LANES:c0 + FEAT] = k_rope
        vt_ref[0, h * V_ROWS:h * V_ROWS + HEAD_DIM, :] = v_t[d0:d0 + HEAD_DIM].astype(BF16)

    lane = lax.broadcasted_iota(jnp.int32, (tm, LANES), 1)
    sm = sm_ref[...]
    fl = sm[:, LANES:] + bf_ref[...]
    logf = jnp.minimum(fl, 0.0) - jnp.log1p(jnp.exp(-jnp.abs(fl)))
    row = lax.broadcasted_iota(jnp.int32, (tm, tm), 0)
    col = lax.broadcasted_iota(jnp.int32, (tm, tm), 1)
    tri = jnp.where(row >= col, 1.0, 0.0).astype(BF16)
    c = carry_scr[...]
    for piece in _split3(logf):
        c = c + jnp.dot(tri, piece.astype(BF16), preferred_element_type=F32)
    carry_scr[...] = c[tm - 1:tm, :]
    c2 = c * LOG2E
    c2_t = c2.T

    aug = 16
    srow = lax.broadcasted_iota(jnp.int32, (aug, tm), 0)
    q_zero = jnp.zeros((FEAT - HEAD_DIM - aug, tm), BF16)
    for h in range(N_HEADS):
        c0 = (N_HEADS + h) * FEAT
        s0 = h * HEAD_DIM
        hi, mid, lo = _split3(jnp.broadcast_to(c2[:, h:h + 1], (tm, LANES)))
        ck = jnp.where(lane < 3, 1.0, jnp.where(lane == 3, -hi, jnp.where(
            lane == 4, -mid, jnp.where(lane == 5, -lo, 0.0))))
        kp_ref[:, c0:c0 + LANES] = fk_ref[:, s0:s0 + HEAD_DIM]
        kp_ref[:, c0 + LANES:c0 + FEAT] = ck.astype(BF16)
        hi_t, mid_t, lo_t = _split3(c2_t[h:h + 1, :])
        cq_t = jnp.where(srow == 0, hi_t, jnp.where(srow == 1, mid_t, jnp.where(
            srow == 2, lo_t, jnp.where(srow < 6, 1.0, 0.0))))
        put_t(qt_ref, c0, fq_ref[:, s0:s0 + HEAD_DIM])
        qt_ref[0, c0 + HEAD_DIM:c0 + HEAD_DIM + aug, :] = cq_t.astype(BF16)
        qt_ref[0, c0 + HEAD_DIM + aug:c0 + FEAT, :] = q_zero
        put_t(vt_ref, (N_HEADS + h) * V_ROWS, fv_ref[:, s0:s0 + HEAD_DIM])
    pad = V_ROWS - HEAD_DIM
    ones_blk = jnp.where(lax.broadcasted_iota(jnp.int32, (pad, tm), 0) == 0,
                         1.0, 0.0).astype(BF16)
    for h in range(2 * N_HEADS):
        vt_ref[0, h * V_ROWS + HEAD_DIM:(h + 1) * V_ROWS, :] = ones_blk


def _prep(lat, small, rest, pos, g_q, g_kv, w_uq_t, w_k, w_v_t, b_f, inv_f, batch, seq):
    t = lat.shape[0]
    ns = seq // PREP_TM
    per = ATT_T // PREP_TM
    nt = t // ATT_T
    row = lambda b, i: (b * ns + i, 0)
    const = lambda b, i: (0, 0)
    tile_t = lambda b, i: ((b * ns + i) // per, 0, (b * ns + i) % per)
    return pl.pallas_call(
        _prep_kernel,
        grid=(batch, ns),
        in_specs=[
            pl.BlockSpec((PREP_TM, LAT_DIM), row),
            pl.BlockSpec((PREP_TM, SMALL_W), row),
            pl.BlockSpec((PREP_TM, BRANCH_WIDTH), lambda b, i: (b * ns + i, 1)),
            pl.BlockSpec((PREP_TM, BRANCH_WIDTH), lambda b, i: (b * ns + i, 2)),
            pl.BlockSpec((PREP_TM, BRANCH_WIDTH), lambda b, i: (b * ns + i, 3)),
            pl.BlockSpec((1, 1, PREP_TM), lambda b, i: (b * ns + i, 0, 0)),
            pl.BlockSpec((1, Q_RANK), const),
            pl.BlockSpec((1, KV_RANK), const),
            pl.BlockSpec((N_HEADS * FEAT, Q_RANK), const),
            pl.BlockSpec((KV_RANK, BRANCH_WIDTH), const),
            pl.BlockSpec((BRANCH_WIDTH, KV_RANK), const),
            pl.BlockSpec((1, LANES), const),
            pl.BlockSpec((ROPE_HALF, 1), const),
        ],
        out_specs=[
            pl.BlockSpec((1, 2 * N_HEADS * FEAT, PREP_TM), tile_t),
            pl.BlockSpec((PREP_TM, 2 * N_HEADS * FEAT), row),
            pl.BlockSpec((1, 2 * N_HEADS * V_ROWS, PREP_TM), tile_t),
        ],
        out_shape=[
            jax.ShapeDtypeStruct((nt, 2 * N_HEADS * FEAT, ATT_T), BF16),
            jax.ShapeDtypeStruct((t, 2 * N_HEADS * FEAT), BF16),
            jax.ShapeDtypeStruct((nt, 2 * N_HEADS * V_ROWS, ATT_T), BF16),
        ],
        scratch_shapes=[pltpu.VMEM((1, LANES), F32)],
        compiler_params=pltpu.CompilerParams(
            dimension_semantics=("arbitrary", "arbitrary"),
            vmem_limit_bytes=VMEM_LIMIT),
        name="prep",
    )(lat, small, rest, rest, rest, pos, g_q, g_kv, w_uq_t, w_k, w_v_t, b_f, inv_f)


def _attn_kernel(qt_ref, k_ref, vt_ref, o_ref, s0, s1, cm0, cm1, m_scr, acc_scr):
    t = ATT_T
    nq = qt_ref.shape[0]
    nsteps = nq * (nq + 1) // 2
    s_buf, cm_buf = (s0, s1), (cm0, cm1)

    def advance(ij):
        i, j = ij
        wrap = j >= i
        return jnp.where(wrap, i + 1, i), jnp.where(wrap, 0, j + 1)

    nblk = t // ATT_BLK
    blk = ATT_BLK

    def a_begin():
        return [[None] * nblk for _ in range(ATT_HB)]

    def a_piece(ij, slot, diagonal, hh, r, cmax):
        i, j = ij
        f0, r0 = hh * FEAT, r * blk
        k = k_ref[pl.ds(pl.multiple_of(j * t + r0, blk), blk), f0:f0 + FEAT]
        c_lo = r0 if diagonal else 0
        s = jnp.dot(k, qt_ref[i, f0:f0 + FEAT, c_lo:], preferred_element_type=F32)
        if diagonal:
            key = lax.broadcasted_iota(jnp.int32, (blk, blk), 0)
            qry = lax.broadcasted_iota(jnp.int32, (blk, blk), 1)
            parts = [jnp.where(key <= qry, s[:, :blk], -jnp.inf)]
            if s.shape[1] > blk:
                parts.append(s[:, blk:])
            s = jnp.concatenate(parts, axis=1)
        s_buf[slot][hh, r0:r0 + blk, c_lo:] = s
        for qb in range(c_lo // blk, nblk):
            part = jnp.max(s[:, qb * blk - c_lo:(qb + 1) * blk - c_lo], axis=0, keepdims=True)
            cmax[hh][qb] = part if cmax[hh][qb] is None else jnp.maximum(cmax[hh][qb], part)

    def a_end(slot, hh, cmax):
        for qb in range(nblk):
            cm_buf[slot][hh, :, qb * blk:(qb + 1) * blk] = cmax[hh][qb]

    def bc_begin(ij, slot, hh):
        i, j = ij
        m_old = jnp.where(j == 0, -jnp.inf, m_scr[hh])
        m_new = jnp.maximum(m_old, cm_buf[slot][hh])
        return dict(m_new=m_new, alpha=jnp.exp2(m_old - m_new), pv=None)

    def bc_piece(ij, slot, hh, r, st):
        i, j = ij
        d0, r0 = hh * V_ROWS, r * blk
        p = jnp.exp2(s_buf[slot][hh, r0:r0 + blk, :] - st["m_new"])
        pv = jnp.dot(vt_ref[j, d0:d0 + V_ROWS, r0:r0 + blk], p.astype(BF16),
                     preferred_element_type=F32)
        st["pv"] = pv if st["pv"] is None else st["pv"] + pv

    def bc_end(hh, st):
        m_scr[hh] = st["m_new"]
        acc_scr[hh] = st["alpha"] * acc_scr[hh] + st["pv"]

    def bcd_begin(ij, slot, hh):
        st = bc_begin(ij, slot, hh)
        st["pv"] = [None] * nblk
        return st

    def bcd_piece(ij, slot, hh, r, st):
        i, j = ij
        d0, r0 = hh * V_ROWS, r * blk
        p = jnp.exp2(s_buf[slot][hh, r0:r0 + blk, r0:] - st["m_new"][:, r0:])
        pv = jnp.dot(vt_ref[j, d0:d0 + V_ROWS, r0:r0 + blk], p.astype(BF16),
                     preferred_element_type=F32)
        for qb in range(r, nblk):
            part = pv[:, (qb - r) * blk:(qb - r + 1) * blk]
            st["pv"][qb] = part if st["pv"][qb] is None else st["pv"][qb] + part

    def bcd_end(ij, hh, st):
        i, j = ij
        for qb in range(nblk):
            cols = slice(qb * blk, (qb + 1) * blk)
            acc = st["alpha"][:, cols] * acc_scr[hh, :, cols] + st["pv"][qb]
            out = acc[:HEAD_DIM] / acc[HEAD_DIM:HEAD_DIM + 1]
            o_ref[pl.ds(pl.multiple_of(i * t + qb * blk, blk), blk),
                  hh * HEAD_DIM:(hh + 1) * HEAD_DIM] = out.T.astype(BF16)

    def run(slot_a, ija, ijb, a_is_diag, b_is_diag):
        slot_b = 1 - slot_a
        begin, piece = (bcd_begin, bcd_piece) if b_is_diag else (bc_begin, bc_piece)
        cmax = a_begin()
        sts = [begin(ijb, slot_b, hh) for hh in range(ATT_HB)]
        for r in range(nblk):
            for hh in range(ATT_HB):
                if ija is not None:
                    a_piece(ija, slot_a, a_is_diag, hh, r, cmax)
                piece(ijb, slot_b, hh, r, sts[hh])
        for hh in range(ATT_HB):
            if ija is not None:
                a_end(slot_a, hh, cmax)
            if b_is_diag:
                bcd_end(ijb, hh, sts[hh])
            else:
                bc_end(hh, sts[hh])

    def iteration(slot_a, ija, ijb):
        a_diag = ija[1] == ija[0]
        b_diag = ijb[1] == ijb[0]
        pl.when(b_diag)(lambda: run(slot_a, ija, ijb, False, True))
        pl.when(a_diag)(lambda: run(slot_a, ija, ijb, True, False))
        pl.when(jnp.logical_not(jnp.logical_or(a_diag, b_diag)))(
            lambda: run(slot_a, ija, ijb, False, False))

    zero = jnp.int32(0)
    m_scr[...] = jnp.full(m_scr.shape, -jnp.inf, F32)
    acc_scr[...] = jnp.zeros(acc_scr.shape, F32)
    cmax0 = a_begin()
    for hh in range(ATT_HB):
        for r in range(nblk):
            a_piece((zero, zero), 0, True, hh, r, cmax0)
        a_end(0, hh, cmax0)


    def pair(_, carry):
        ija, ijb = carry
        iteration(1, ija, ijb)
        ija2 = advance(ija)
        iteration(0, ija2, ija)
        return advance(ija2), ija2

    lax.fori_loop(0, nsteps // 2 - 1, pair, (advance((zero, zero)), (zero, zero)))
    last = (jnp.int32(nq - 1), jnp.int32(nq - 1))
    run(1, last, (jnp.int32(nq - 1), jnp.int32(nq - 2)), True, False)
    run(0, None, last, False, True)


def _attn(qt, kp, vt, batch, seq):
    t = kp.shape[0]
    nq = seq // ATT_T
    assert nq >= 2 and (nq * (nq + 1) // 2) % 2 == 0
    nh = 2 * N_HEADS // ATT_HB
    row = pltpu.VMEM((ATT_HB, 1, ATT_T), F32)
    tile = pltpu.VMEM((ATT_HB, ATT_T, ATT_T), F32)
    return pl.pallas_call(
        _attn_kernel,
        grid=(batch, nh),
        in_specs=[
            pl.BlockSpec((nq, ATT_HB * FEAT, ATT_T), lambda b, h: (b, h, 0)),
            pl.BlockSpec((seq, ATT_HB * FEAT), lambda b, h: (b, h)),
            pl.BlockSpec((nq, ATT_HB * V_ROWS, ATT_T), lambda b, h: (b, h, 0)),
        ],
        out_specs=pl.BlockSpec((seq, ATT_HB * HEAD_DIM), lambda b, h: (b, h)),
        out_shape=jax.ShapeDtypeStruct((t, D_MIX), BF16),
        scratch_shapes=[
            tile, tile, row, row, row,
            pltpu.VMEM((ATT_HB, V_ROWS, ATT_T), F32),
        ],
        compiler_params=pltpu.CompilerParams(
            dimension_semantics=("arbitrary", "arbitrary"),
            vmem_limit_bytes=VMEM_LIMIT),
        name="attn",
    )(qt, kp, vt)


def _silu(g):
    return g / (1.0 + jnp.exp(-g))


def _out_kernel(o_ref, gm_ref, gf_ref, x_ref, w_ref, g_ref, y_ref):
    o = o_ref[...].astype(F32)
    om = (o[:, :BRANCH_WIDTH] * _silu(gm_ref[...].astype(F32))).astype(BF16)
    of = (o[:, BRANCH_WIDTH:] * _silu(gf_ref[...].astype(F32))).astype(BF16)
    y = (jnp.dot(om, w_ref[:BRANCH_WIDTH, :], preferred_element_type=F32)
         + jnp.dot(of, w_ref[BRANCH_WIDTH:, :], preferred_element_type=F32))
    y_ref[...] = x_ref[...] + _rms(y, g_ref[...])


def _out(o, rest, x2, w_out, g_post):
    t = o.shape[0]
    return pl.pallas_call(
        _out_kernel,
        grid=(t // OUT_TM,),
        in_specs=[
            pl.BlockSpec((OUT_TM, D_MIX), lambda i: (i, 0)),
            pl.BlockSpec((OUT_TM, BRANCH_WIDTH), lambda i: (i, 0)),
            pl.BlockSpec((OUT_TM, BRANCH_WIDTH), lambda i: (i, 4)),
            pl.BlockSpec((OUT_TM, D_MODEL), lambda i: (i, 0)),
            pl.BlockSpec((D_MIX, D_MODEL), lambda i: (0, 0)),
            pl.BlockSpec((1, D_MODEL), lambda i: (0, 0)),
        ],
        out_specs=pl.BlockSpec((OUT_TM, D_MODEL), lambda i: (i, 0)),
        out_shape=jax.ShapeDtypeStruct((t, D_MODEL), F32),
        compiler_params=pltpu.CompilerParams(
            dimension_semantics=("arbitrary",),
            vmem_limit_bytes=VMEM_LIMIT),
        name="out",
    )(o, rest, rest, x2, w_out, g_post)


def _layer(x2, pos, inv_f, g_pre, w_in, g_q, w_uq, g_kv, w_ukv, b_forget,
           w_out, g_post, batch, seq):
    w_all = _pack_w_in(jnp.swapaxes(w_in, 0, 1))
    w_uq_t = jnp.pad(w_uq.T.reshape(N_HEADS, QK_DIM, Q_RANK),
                     ((0, 0), (0, FEAT - QK_DIM), (0, 0))).reshape(
                         N_HEADS * FEAT, Q_RANK).astype(BF16)
    w_kv = w_ukv.reshape(KV_RANK, N_HEADS, NOPE_DIM + HEAD_DIM)
    w_k = w_kv[:, :, :NOPE_DIM].reshape(KV_RANK, BRANCH_WIDTH).astype(BF16)
    w_v_t = w_kv[:, :, NOPE_DIM:].reshape(KV_RANK, BRANCH_WIDTH).T.astype(BF16)
    w_out_b = w_out.astype(BF16)
    b_f = jnp.pad(b_forget, (0, LANES - N_HEADS))[None, :]

    lat, small, rest = _proj(x2, g_pre[None, :], w_all)
    qt, kp, vt = _prep(lat, small, rest, pos, g_q[None, :], g_kv[None, :],
                       w_uq_t, w_k, w_v_t, b_f, inv_f, batch, seq)
    o = _attn(qt, kp, vt, batch, seq)
    return _out(o, rest, x2, w_out_b, g_post[None, :])


def kernel(x, positions, g_pre, w_in, g_q_latent, w_uq, g_kv_latent, w_ukv,
           b_forget, w_out, g_post):
    batch, seq, d = x.shape
    depth = g_pre.shape[0]
    x2 = x.reshape(batch * seq, d)
    pos = positions.reshape(batch * seq // PREP_TM, 1, PREP_TM)
    inv_f = (ROPE_THETA ** (-jnp.arange(0, ROPE_DIM, 2, dtype=F32) / ROPE_DIM))[:, None]
    for l in range(depth):
        x2 = _layer(x2, pos, inv_f, g_pre[l], w_in[l], g_q_latent[l], w_uq[l],
                    g_kv_latent[l], w_ukv[l], b_forget[l], w_out[l], g_post[l],
                    batch, seq)
    return x2.reshape(batch, seq, d)
```

```python
import functools
import math

import jax
import jax.numpy as jnp
from jax import lax
from jax.experimental import pallas as pl
from jax.experimental.pallas import tpu as pltpu

F32 = jnp.float32
BF16 = jnp.bfloat16

D_MODEL = 2048
N_HEADS = 8
HEAD_DIM = 128
NOPE_DIM = 128
ROPE_DIM = 64
ROPE_HALF = ROPE_DIM // 2
QK_DIM = NOPE_DIM + ROPE_DIM
Q_RANK = 768
KV_RANK = 512
LAT_DIM = Q_RANK + KV_RANK
BRANCH_WIDTH = N_HEADS * HEAD_DIM
D_MIX = 2 * BRANCH_WIDTH
ROPE_THETA = 10000.0
NORM_EPS = 1e-6
LOG2E = math.log2(math.e)

LANES = 128
FEAT = 256
SMALL_W = 2 * LANES
REST_W = 5 * BRANCH_WIDTH

MLA_QSCALE = QK_DIM ** -0.5 * LOG2E
FOX_QSCALE = HEAD_DIM ** -0.5 * LOG2E

PACK_TN = 256
PROJ_TM = 1024
PREP_TM = 512
ATT_T = 1024
ATT_BLK = 256
ATT_HB = 2
V_ROWS = HEAD_DIM + 16
OUT_TM = 512

VMEM_LIMIT = 56 * 1024 * 1024


def _rms(x, g):
    return x * lax.rsqrt(jnp.mean(x * x, axis=-1, keepdims=True) + NORM_EPS) * g


def _pack_kernel(w_ref, o_ref):
    o_kr = LAT_DIM
    o_gm = o_kr + ROPE_DIM
    o_fl = o_gm + 4 * BRANCH_WIDTH
    o_gf = o_fl + N_HEADS
    cols = w_ref.shape[1]
    zeros = lambda n: jnp.zeros((n, cols), F32)
    o_ref[...] = jnp.concatenate([
        w_ref[o_gm:o_fl, :], w_ref[o_gf:o_gf + BRANCH_WIDTH, :],
        w_ref[:LAT_DIM, :],
        w_ref[o_kr:o_gm, :], zeros(LANES - ROPE_DIM),
        w_ref[o_fl:o_gf, :], zeros(LANES - N_HEADS),
    ], axis=0).astype(BF16)


def _pack_w_in(w_in_t):
    d_in, d = w_in_t.shape
    height = REST_W + LAT_DIM + SMALL_W
    return pl.pallas_call(
        _pack_kernel,
        grid=(d // PACK_TN,),
        in_specs=[pl.BlockSpec((d_in, PACK_TN), lambda i: (0, i))],
        out_specs=pl.BlockSpec((height, PACK_TN), lambda i: (0, i)),
        out_shape=jax.ShapeDtypeStruct((height, d), BF16),
        compiler_params=pltpu.CompilerParams(
            dimension_semantics=("arbitrary",),
            vmem_limit_bytes=VMEM_LIMIT),
        name="pack",
    )(w_in_t)


def _dot_nt(a, b):
    return lax.dot_general(a, b, (((1,), (1,)), ((), ())), preferred_element_type=F32)


def _proj_kernel(x_ref, g_ref, wlat_ref, wsm_ref, wrest_ref,
                 lat_ref, sm_ref, rest_ref, h_scr):
    j = pl.program_id(1)

    @pl.when(j == 0)
    def _():
        hb = _rms(x_ref[...], g_ref[...]).astype(BF16)
        h_scr[...] = hb
        lat_ref[...] = _dot_nt(hb, wlat_ref[...]).astype(BF16)
        sm_ref[...] = _dot_nt(hb, wsm_ref[...])

    acc = _dot_nt(h_scr[...], wrest_ref[...])
    scale = jnp.where(j == 1, FOX_QSCALE, 1.0).astype(F32)
    rest_ref[...] = (acc * scale).astype(BF16)


def _proj(x2, g_pre, w_all):
    t = x2.shape[0]
    n_rest = REST_W // BRANCH_WIDTH
    assert REST_W % LAT_DIM == 0 and (REST_W + LAT_DIM) % SMALL_W == 0
    once = pl.Buffered(1)
    return pl.pallas_call(
        _proj_kernel,
        grid=(t // PROJ_TM, n_rest),
        in_specs=[
            pl.BlockSpec((PROJ_TM, D_MODEL), lambda i, j: (i, 0)),
            pl.BlockSpec((1, D_MODEL), lambda i, j: (0, 0)),
            pl.BlockSpec((LAT_DIM, D_MODEL), lambda i, j: (REST_W // LAT_DIM, 0),
                         pipeline_mode=once),
            pl.BlockSpec((SMALL_W, D_MODEL), lambda i, j: ((REST_W + LAT_DIM) // SMALL_W, 0),
                         pipeline_mode=once),
            pl.BlockSpec((BRANCH_WIDTH, D_MODEL), lambda i, j: (j, 0)),
        ],
        out_specs=[
            pl.BlockSpec((PROJ_TM, LAT_DIM), lambda i, j: (i, 0)),
            pl.BlockSpec((PROJ_TM, SMALL_W), lambda i, j: (i, 0)),
            pl.BlockSpec((PROJ_TM, BRANCH_WIDTH), lambda i, j: (i, j)),
        ],
        out_shape=[
            jax.ShapeDtypeStruct((t, LAT_DIM), BF16),
            jax.ShapeDtypeStruct((t, SMALL_W), F32),
            jax.ShapeDtypeStruct((t, REST_W), BF16),
        ],
        scratch_shapes=[pltpu.VMEM((PROJ_TM, D_MODEL), BF16)],
        compiler_params=pltpu.CompilerParams(
            dimension_semantics=("arbitrary", "arbitrary"),
            vmem_limit_bytes=VMEM_LIMIT),
        name="proj",
    )(x2, g_pre, w_all, w_all, w_all)


def _split3(x):
    hi = x.astype(BF16).astype(F32)
    r = x - hi
    mid = r.astype(BF16).astype(F32)
    lo = (r - mid).astype(BF16).astype(F32)
    return hi, mid, lo


def _prep_kernel(lat_ref, sm_ref, fq_ref, fk_ref, fv_ref, pos_ref,
                 gq_ref, gkv_ref, wuqt_ref, wk_ref, wvt_ref, bf_ref, invf_ref,
                 qt_ref, kp_ref, vt_ref, carry_scr):
    tm = lat_ref.shape[0]

    def put_t(ref, r0, x):
        ref[0, r0:r0 + x.shape[1], :] = x.T

    @pl.when(pl.program_id(1) == 0)
    def _():
        carry_scr[...] = jnp.zeros_like(carry_scr)

    lat = lat_ref[...].astype(F32)
    qn = _rms(lat[:, :Q_RANK], gq_ref[...]).astype(BF16)
    kvn = _rms(lat[:, Q_RANK:], gkv_ref[...]).astype(BF16)

    ang = invf_ref[...] * pos_ref[0].astype(F32)
    cos_t = jnp.cos(ang)
    sin_t = jnp.sin(ang)

    def rope_t(x):
        x1, x2 = x[:ROPE_HALF], x[ROPE_HALF:ROPE_DIM]
        return x1 * cos_t - x2 * sin_t, x2 * cos_t + x1 * sin_t

    kr1, kr2 = rope_t(sm_ref[:, :LANES].T)
    k_rope = jnp.concatenate(
        [kr1, kr2, jnp.zeros((LANES - ROPE_DIM, tm), F32)], axis=0).T.astype(BF16)

    lane = lax.broadcasted_iota(jnp.int32, (tm, LANES), 1)
    fl = sm_ref[:, LANES:] + bf_ref[...]
    logf = jnp.minimum(fl, 0.0) - jnp.log1p(jnp.exp(-jnp.abs(fl)))
    row = lax.broadcasted_iota(jnp.int32, (tm, tm), 0)
    col = lax.broadcasted_iota(jnp.int32, (tm, tm), 1)
    tri = jnp.where(row >= col, 1.0, 0.0).astype(BF16)
    c = carry_scr[...]
    for piece in _split3(logf):
        c = c + jnp.dot(tri, piece.astype(BF16), preferred_element_type=F32)
    carry_scr[...] = c[tm - 1:tm, :]
    c2 = c * LOG2E
    c2_t = c2.T

    aug = 16
    srow = lax.broadcasted_iota(jnp.int32, (aug, tm), 0)
    q_zero = jnp.zeros((FEAT - HEAD_DIM - aug, tm), BF16)
    q_pad = jnp.zeros((FEAT - QK_DIM, tm), BF16)

    def fox_head(h):
        c0 = (N_HEADS + h) * FEAT
        s0 = h * HEAD_DIM
        hi, mid, lo = _split3(jnp.broadcast_to(c2[:, h:h + 1], (tm, LANES)))
        ck = jnp.where(lane < 3, 1.0, jnp.where(lane == 3, -hi, jnp.where(
            lane == 4, -mid, jnp.where(lane == 5, -lo, 0.0))))
        kp_ref[:, c0:c0 + LANES] = fk_ref[:, s0:s0 + HEAD_DIM]
        kp_ref[:, c0 + LANES:c0 + FEAT] = ck.astype(BF16)
        hi_t, mid_t, lo_t = _split3(c2_t[h:h + 1, :])
        cq_t = jnp.where(srow == 0, hi_t, jnp.where(srow == 1, mid_t, jnp.where(
            srow == 2, lo_t, jnp.where(srow < 6, 1.0, 0.0))))
        put_t(qt_ref, c0, fq_ref[:, s0:s0 + HEAD_DIM])
        qt_ref[0, c0 + HEAD_DIM:c0 + HEAD_DIM + aug, :] = cq_t.astype(BF16)
        qt_ref[0, c0 + HEAD_DIM + aug:c0 + FEAT, :] = q_zero
        put_t(vt_ref, (N_HEADS + h) * V_ROWS, fv_ref[:, s0:s0 + HEAD_DIM])

    def mla_q_head(h, q_t, base):
        c0, l0 = h * FEAT, (h - base) * FEAT
        qr1, qr2 = rope_t(q_t[l0 + NOPE_DIM:l0 + QK_DIM])
        qt_ref[0, c0:c0 + NOPE_DIM, :] = q_t[l0:l0 + NOPE_DIM].astype(BF16)
        qt_ref[0, c0 + NOPE_DIM:c0 + NOPE_DIM + ROPE_HALF, :] = qr1.astype(BF16)
        qt_ref[0, c0 + NOPE_DIM + ROPE_HALF:c0 + QK_DIM, :] = qr2.astype(BF16)
        qt_ref[0, c0 + QK_DIM:c0 + FEAT, :] = q_pad

    half = N_HEADS // 2
    q_a = _dot_nt(wuqt_ref[:half * FEAT, :], qn) * MLA_QSCALE
    for h in range(half):
        fox_head(h)
    q_b = _dot_nt(wuqt_ref[half * FEAT:, :], qn) * MLA_QSCALE
    for h in range(half, N_HEADS):
        fox_head(h)
    k_nope = jnp.dot(kvn, wk_ref[...], preferred_element_type=F32)
    for h in range(half):
        mla_q_head(h, q_a, 0)
    v_t = _dot_nt(wvt_ref[...], kvn)
    for h in range(half, N_HEADS):
        mla_q_head(h, q_b, half)
    for h in range(N_HEADS):
        c0, d0 = h * FEAT, h * HEAD_DIM
        kp_ref[:, c0:c0 + LANES] = k_nope[:, d0:d0 + HEAD_DIM].astype(BF16)
        kp_ref[:, c0 + LANES:c0 + FEAT] = k_rope
        vt_ref[0, h * V_ROWS:h * V_ROWS + HEAD_DIM, :] = v_t[d0:d0 + HEAD_DIM].astype(BF16)
    pad = V_ROWS - HEAD_DIM
    ones_blk = jnp.where(lax.broadcasted_iota(jnp.int32, (pad, tm), 0) == 0,
                         1.0, 0.0).astype(BF16)
    for h in range(2 * N_HEADS):
        vt_ref[0, h * V_ROWS + HEAD_DIM:(h + 1) * V_ROWS, :] = ones_blk


def _prep(lat, small, rest, pos, g_q, g_kv, w_uq_t, w_k, w_v_t, b_f, inv_f, batch, seq):
    t = lat.shape[0]
    ns = seq // PREP_TM
    per = ATT_T // PREP_TM
    nt = t // ATT_T
    row = lambda b, i: (b * ns + i, 0)
    const = lambda b, i: (0, 0)
    tile_t = lambda b, i: ((b * ns + i) // per, 0, (b * ns + i) % per)
    return pl.pallas_call(
        _prep_kernel,
        grid=(batch, ns),
        in_specs=[
            pl.BlockSpec((PREP_TM, LAT_DIM), row),
            pl.BlockSpec((PREP_TM, SMALL_W), row),
            pl.BlockSpec((PREP_TM, BRANCH_WIDTH), lambda b, i: (b * ns + i, 1)),
            pl.BlockSpec((PREP_TM, BRANCH_WIDTH), lambda b, i: (b * ns + i, 2)),
            pl.BlockSpec((PREP_TM, BRANCH_WIDTH), lambda b, i: (b * ns + i, 3)),
            pl.BlockSpec((1, 1, PREP_TM), lambda b, i: (b * ns + i, 0, 0)),
            pl.BlockSpec((1, Q_RANK), const),
            pl.BlockSpec((1, KV_RANK), const),
            pl.BlockSpec((N_HEADS * FEAT, Q_RANK), const),
            pl.BlockSpec((KV_RANK, BRANCH_WIDTH), const),
            pl.BlockSpec((BRANCH_WIDTH, KV_RANK), const),
            pl.BlockSpec((1, LANES), const),
            pl.BlockSpec((ROPE_HALF, 1), const),
        ],
        out_specs=[
            pl.BlockSpec((1, 2 * N_HEADS * FEAT, PREP_TM), tile_t),
            pl.BlockSpec((PREP_TM, 2 * N_HEADS * FEAT), row),
            pl.BlockSpec((1, 2 * N_HEADS * V_ROWS, PREP_TM), tile_t),
        ],
        out_shape=[
            jax.ShapeDtypeStruct((nt, 2 * N_HEADS * FEAT, ATT_T), BF16),
            jax.ShapeDtypeStruct((t, 2 * N_HEADS * FEAT), BF16),
            jax.ShapeDtypeStruct((nt, 2 * N_HEADS * V_ROWS, ATT_T), BF16),
        ],
        scratch_shapes=[pltpu.VMEM((1, LANES), F32)],
        compiler_params=pltpu.CompilerParams(
            dimension_semantics=("arbitrary", "arbitrary"),
            vmem_limit_bytes=VMEM_LIMIT),
        name="prep",
    )(lat, small, rest, rest, rest, pos, g_q, g_kv, w_uq_t, w_k, w_v_t, b_f, inv_f)


def _attn_kernel(qt_ref, k_ref, vt_ref, o_ref, s0, s1, cm0, cm1, m_scr, acc_scr):
    t = ATT_T
    nq = qt_ref.shape[0]
    nsteps = nq * (nq + 1) // 2
    s_buf, cm_buf = (s0, s1), (cm0, cm1)

    def advance(ij):
        i, j = ij
        wrap = j >= i
        return jnp.where(wrap, i + 1, i), jnp.where(wrap, 0, j + 1)

    nblk = t // ATT_BLK
    blk = ATT_BLK

    def a_begin():
        return [[None] * nblk for _ in range(ATT_HB)]

    def a_piece(ij, slot, diagonal, hh, r, cmax):
        i, j = ij
        f0, r0 = hh * FEAT, r * blk
        k = k_ref[pl.ds(pl.multiple_of(j * t + r0, blk), blk), f0:f0 + FEAT]
        c_lo = r0 if diagonal else 0
        s = jnp.dot(k, qt_ref[i, f0:f0 + FEAT, c_lo:], preferred_element_type=F32)
        if diagonal:
            key = lax.broadcasted_iota(jnp.int32, (blk, blk), 0)
            qry = lax.broadcasted_iota(jnp.int32, (blk, blk), 1)
            parts = [jnp.where(key <= qry, s[:, :blk], -jnp.inf)]
            if s.shape[1] > blk:
                parts.append(s[:, blk:])
            s = jnp.concatenate(parts, axis=1)
        s_buf[slot][hh, r0:r0 + blk, c_lo:] = s
        for qb in range(c_lo // blk, nblk):
            part = jnp.max(s[:, qb * blk - c_lo:(qb + 1) * blk - c_lo], axis=0, keepdims=True)
            cmax[hh][qb] = part if cmax[hh][qb] is None else jnp.maximum(cmax[hh][qb], part)

    def a_end(slot, hh, cmax):
        for qb in range(nblk):
            cm_buf[slot][hh, :, qb * blk:(qb + 1) * blk] = cmax[hh][qb]

    def bc_begin(ij, slot, hh):
        i, j = ij
        m_old = jnp.where(j == 0, -jnp.inf, m_scr[hh])
        m_new = jnp.maximum(m_old, cm_buf[slot][hh])
        return dict(m_new=m_new, alpha=jnp.exp2(m_old - m_new), pv=None)

    def bc_piece(ij, slot, hh, r, st):
        i, j = ij
        d0, r0 = hh * V_ROWS, r * blk
        p = jnp.exp2(s_buf[slot][hh, r0:r0 + blk, :] - st["m_new"])
        pv = jnp.dot(vt_ref[j, d0:d0 + V_ROWS, r0:r0 + blk], p.astype(BF16),
                     preferred_element_type=F32)
        st["pv"] = pv if st["pv"] is None else st["pv"] + pv

    def bc_end(hh, st):
        m_scr[hh] = st["m_new"]
        acc_scr[hh] = st["alpha"] * acc_scr[hh] + st["pv"]

    def bcd_begin(ij, slot, hh):
        st = bc_begin(ij, slot, hh)
        st["pv"] = [None] * nblk
        return st

    def bcd_piece(ij, slot, hh, r, st):
        i, j = ij
        d0, r0 = hh * V_ROWS, r * blk
        p = jnp.exp2(s_buf[slot][hh, r0:r0 + blk, r0:] - st["m_new"][:, r0:])
        pv = jnp.dot(vt_ref[j, d0:d0 + V_ROWS, r0:r0 + blk], p.astype(BF16),
                     preferred_element_type=F32)
        for qb in range(r, nblk):
            part = pv[:, (qb - r) * blk:(qb - r + 1) * blk]
            st["pv"][qb] = part if st["pv"][qb] is None else st["pv"][qb] + part

    def bcd_end(ij, hh, st):
        i, j = ij
        for qb in range(nblk):
            cols = slice(qb * blk, (qb + 1) * blk)
            acc = st["alpha"][:, cols] * acc_scr[hh, :, cols] + st["pv"][qb]
            out = acc[:HEAD_DIM] / acc[HEAD_DIM:HEAD_DIM + 1]
            o_ref[pl.ds(pl.multiple_of(i * t + qb * blk, blk), blk),
                  hh * HEAD_DIM:(hh + 1) * HEAD_DIM] = out.T.astype(BF16)

    def run(slot_a, ija, ijb, a_is_diag, b_is_diag):
        slot_b = 1 - slot_a
        begin, piece = (bcd_begin, bcd_piece) if b_is_diag else (bc_begin, bc_piece)
        cmax = a_begin()
        sts = [begin(ijb, slot_b, hh) for hh in range(ATT_HB)]
        for r in range(nblk):
            for hh in range(ATT_HB):
                piece(ijb, slot_b, hh, r, sts[hh])
                if ija is not None:
                    a_piece(ija, slot_a, a_is_diag, hh, r, cmax)
        for hh in range(ATT_HB):
            if ija is not None:
                a_end(slot_a, hh, cmax)
            if b_is_diag:
                bcd_end(ijb, hh, sts[hh])
            else:
                bc_end(hh, sts[hh])

    def iteration(slot_a, ija, ijb):
        a_diag = ija[1] == ija[0]
        b_diag = ijb[1] == ijb[0]
        pl.when(b_diag)(lambda: run(slot_a, ija, ijb, False, True))
        pl.when(a_diag)(lambda: run(slot_a, ija, ijb, True, False))
        pl.when(jnp.logical_not(jnp.logical_or(a_diag, b_diag)))(
            lambda: run(slot_a, ija, ijb, False, False))

    zero = jnp.int32(0)
    m_scr[...] = jnp.full(m_scr.shape, -jnp.inf, F32)
    acc_scr[...] = jnp.zeros(acc_scr.shape, F32)
    cmax0 = a_begin()
    for hh in range(ATT_HB):
        for r in range(nblk):
            a_piece((zero, zero), 0, True, hh, r, cmax0)
        a_end(0, hh, cmax0)


    def pair(_, carry):
        ija, ijb = carry
        iteration(1, ija, ijb)
        ija2 = advance(ija)
        iteration(0, ija2, ija)
        return advance(ija2), ija2

    lax.fori_loop(0, nsteps // 2 - 1, pair, (advance((zero, zero)), (zero, zero)))
    last = (jnp.int32(nq - 1), jnp.int32(nq - 1))
    run(1, last, (jnp.int32(nq - 1), jnp.int32(nq - 2)), True, False)
    run(0, None, last, False, True)


def _attn(qt, kp, vt, batch, seq):
    t = kp.shape[0]
    nq = seq // ATT_T
    assert nq >= 2 and (nq * (nq + 1) // 2) % 2 == 0
    nh = 2 * N_HEADS // ATT_HB
    row = pltpu.VMEM((ATT_HB, 1, ATT_T), F32)
    tile = pltpu.VMEM((ATT_HB, ATT_T, ATT_T), F32)
    return pl.pallas_call(
        _attn_kernel,
        grid=(batch, nh),
        in_specs=[
            pl.BlockSpec((nq, ATT_HB * FEAT, ATT_T), lambda b, h: (b, h, 0)),
            pl.BlockSpec((seq, ATT_HB * FEAT), lambda b, h: (b, h)),
            pl.BlockSpec((nq, ATT_HB * V_ROWS, ATT_T), lambda b, h: (b, h, 0)),
        ],
        out_specs=pl.BlockSpec((seq, ATT_HB * HEAD_DIM), lambda b, h: (b, h)),
        out_shape=jax.ShapeDtypeStruct((t, D_MIX), BF16),
        scratch_shapes=[
            tile, tile, row, row, row,
            pltpu.VMEM((ATT_HB, V_ROWS, ATT_T), F32),
        ],
        compiler_params=pltpu.CompilerParams(
            dimension_semantics=("arbitrary", "arbitrary"),
            vmem_limit_bytes=VMEM_LIMIT),
        name="attn",
    )(qt, kp, vt)


def _silu(g):
    return g / (1.0 + jnp.exp(-g))


def _out_kernel(o_ref, gm_ref, gf_ref, x_ref, w_ref, g_ref, y_ref):
    o = o_ref[...].astype(F32)
    om = (o[:, :BRANCH_WIDTH] * _silu(gm_ref[...].astype(F32))).astype(BF16)
    of = (o[:, BRANCH_WIDTH:] * _silu(gf_ref[...].astype(F32))).astype(BF16)
    y = (jnp.dot(om, w_ref[:BRANCH_WIDTH, :], preferred_element_type=F32)
         + jnp.dot(of, w_ref[BRANCH_WIDTH:, :], preferred_element_type=F32))
    y_ref[...] = x_ref[...] + _rms(y, g_ref[...])


def _out(o, rest, x2, w_out, g_post):
    t = o.shape[0]
    return pl.pallas_call(
        _out_kernel,
        grid=(t // OUT_TM,),
        in_specs=[
            pl.BlockSpec((OUT_TM, D_MIX), lambda i: (i, 0)),
            pl.BlockSpec((OUT_TM, BRANCH_WIDTH), lambda i: (i, 0)),
            pl.BlockSpec((OUT_TM, BRANCH_WIDTH), lambda i: (i, 4)),
            pl.BlockSpec((OUT_TM, D_MODEL), lambda i: (i, 0)),
            pl.BlockSpec((D_MIX, D_MODEL), lambda i: (0, 0)),
            pl.BlockSpec((1, D_MODEL), lambda i: (0, 0)),
        ],
        out_specs=pl.BlockSpec((OUT_TM, D_MODEL), lambda i: (i, 0)),
        out_shape=jax.ShapeDtypeStruct((t, D_MODEL), F32),
        compiler_params=pltpu.CompilerParams(
            dimension_semantics=("arbitrary",),
            vmem_limit_bytes=VMEM_LIMIT),
        name="out",
    )(o, rest, rest, x2, w_out, g_post)


def _layer(x2, pos, inv_f, g_pre, w_in, g_q, w_uq, g_kv, w_ukv, b_forget,
           w_out, g_post, batch, seq):
    w_all = _pack_w_in(jnp.swapaxes(w_in, 0, 1))
    w_uq_t = jnp.pad(w_uq.T.reshape(N_HEADS, QK_DIM, Q_RANK),
                     ((0, 0), (0, FEAT - QK_DIM), (0, 0))).reshape(
                         N_HEADS * FEAT, Q_RANK).astype(BF16)
    w_kv = w_ukv.reshape(KV_RANK, N_HEADS, NOPE_DIM + HEAD_DIM)
    w_k = w_kv[:, :, :NOPE_DIM].reshape(KV_RANK, BRANCH_WIDTH).astype(BF16)
    w_v_t = w_kv[:, :, NOPE_DIM:].reshape(KV_RANK, BRANCH_WIDTH).T.astype(BF16)
    w_out_b = w_out.astype(BF16)
    b_f = jnp.pad(b_forget, (0, LANES - N_HEADS))[None, :]

    lat, small, rest = _proj(x2, g_pre[None, :], w_all)
    qt, kp, vt = _prep(lat, small, rest, pos, g_q[None, :], g_kv[None, :],
                       w_uq_t, w_k, w_v_t, b_f, inv_f, batch, seq)
    o = _attn(qt, kp, vt, batch, seq)
    return _out(o, rest, x2, w_out_b, g_post[None, :])


def kernel(x, positions, g_pre, w_in, g_q_latent, w_uq, g_kv_latent, w_ukv,
           b_forget, w_out, g_post):
    batch, seq, d = x.shape
    depth = g_pre.shape[0]
    x2 = x.reshape(batch * seq, d)
    pos = positions.reshape(batch * seq // PREP_TM, 1, PREP_TM)
    inv_f = (ROPE_THETA ** (-jnp.arange(0, ROPE_DIM, 2, dtype=F32) / ROPE_DIM))[:, None]
    for l in range(depth):
        x2 = _layer(x2, pos, inv_f, g_pre[l], w_in[l], g_q_latent[l], w_uq[l],
                    g_kv_latent[l], w_ukv[l], b_forget[l], w_out[l], g_post[l],
                    batch, seq)
    return x2.reshape(batch, seq, d)
```

```python
import math

import jax
import jax.numpy as jnp
from jax import lax
from jax.experimental import pallas as pl
from jax.experimental.pallas import tpu as pltpu

F32 = jnp.float32
BF16 = jnp.bfloat16

D_MODEL = 2048
N_HEADS = 8
HEAD_DIM = 128
NOPE_DIM = 128
ROPE_DIM = 64
ROPE_HALF = ROPE_DIM // 2
QK_DIM = NOPE_DIM + ROPE_DIM
Q_RANK = 768
KV_RANK = 512
LAT_DIM = Q_RANK + KV_RANK
BRANCH_WIDTH = N_HEADS * HEAD_DIM
D_MIX = 2 * BRANCH_WIDTH
ROPE_THETA = 10000.0
NORM_EPS = 1e-6
LOG2E = math.log2(math.e)

LANES = 128
FEAT = 256
SMALL_W = 2 * LANES
REST_W = 5 * BRANCH_WIDTH

MLA_QSCALE = QK_DIM ** -0.5 * LOG2E
FOX_QSCALE = HEAD_DIM ** -0.5 * LOG2E

PACK_TN = 256
PROJ_TM = 1024
PREP_TM = 512
ATT_T = 1024
ATT_BLK = 256
ATT_HB = 2
V_ROWS = HEAD_DIM + 16
OUT_TM = 512

V7X_VMEM_BYTES = 64 * 1024 * 1024
VMEM_LIMIT = V7X_VMEM_BYTES * 7 // 8


def _rms(x, g):
    return x * lax.rsqrt(jnp.mean(x * x, axis=-1, keepdims=True) + NORM_EPS) * g


def _pack_kernel(w_ref, o_ref):
    o_kr = LAT_DIM
    o_gm = o_kr + ROPE_DIM
    o_fl = o_gm + 4 * BRANCH_WIDTH
    o_gf = o_fl + N_HEADS
    cols = w_ref.shape[1]
    zeros = lambda n: jnp.zeros((n, cols), F32)
    o_ref[...] = jnp.concatenate([
        w_ref[o_gm:o_fl, :], w_ref[o_gf:o_gf + BRANCH_WIDTH, :],
        w_ref[:LAT_DIM, :],
        w_ref[o_kr:o_gm, :], zeros(LANES - ROPE_DIM),
        w_ref[o_fl:o_gf, :], zeros(LANES - N_HEADS),
    ], axis=0).astype(BF16)


def _pack_w_in(w_in_t):
    d_in, d = w_in_t.shape
    height = REST_W + LAT_DIM + SMALL_W
    return pl.pallas_call(
        _pack_kernel,
        grid=(d // PACK_TN,),
        in_specs=[pl.BlockSpec((d_in, PACK_TN), lambda i: (0, i))],
        out_specs=pl.BlockSpec((height, PACK_TN), lambda i: (0, i)),
        out_shape=jax.ShapeDtypeStruct((height, d), BF16),
        compiler_params=pltpu.CompilerParams(
            dimension_semantics=("arbitrary",),
            vmem_limit_bytes=VMEM_LIMIT),
        name="pack",
    )(w_in_t)


def _dot_nt(a, b):
    return lax.dot_general(a, b, (((1,), (1,)), ((), ())), preferred_element_type=F32)


def _proj_kernel(x_ref, g_ref, wlat_ref, wsm_ref, wrest_ref,
                 lat_ref, sm_ref, rest_ref, h_scr):
    j = pl.program_id(1)

    @pl.when(j == 0)
    def _():
        hb = _rms(x_ref[...], g_ref[...]).astype(BF16)
        h_scr[...] = hb
        lat_ref[...] = _dot_nt(hb, wlat_ref[...]).astype(BF16)
        sm_ref[...] = _dot_nt(hb, wsm_ref[...])

    acc = _dot_nt(h_scr[...], wrest_ref[...])
    scale = jnp.where(j == 1, FOX_QSCALE, 1.0).astype(F32)
    rest_ref[...] = (acc * scale).astype(BF16)


def _proj(x2, g_pre, w_all):
    t = x2.shape[0]
    n_rest = REST_W // BRANCH_WIDTH
    assert REST_W % LAT_DIM == 0 and (REST_W + LAT_DIM) % SMALL_W == 0
    once = pl.Buffered(1)
    return pl.pallas_call(
        _proj_kernel,
        grid=(t // PROJ_TM, n_rest),
        in_specs=[
            pl.BlockSpec((PROJ_TM, D_MODEL), lambda i, j: (i, 0)),
            pl.BlockSpec((1, D_MODEL), lambda i, j: (0, 0)),
            pl.BlockSpec((LAT_DIM, D_MODEL), lambda i, j: (REST_W // LAT_DIM, 0),
                         pipeline_mode=once),
            pl.BlockSpec((SMALL_W, D_MODEL), lambda i, j: ((REST_W + LAT_DIM) // SMALL_W, 0),
                         pipeline_mode=once),
            pl.BlockSpec((BRANCH_WIDTH, D_MODEL), lambda i, j: (j, 0)),
        ],
        out_specs=[
            pl.BlockSpec((PROJ_TM, LAT_DIM), lambda i, j: (i, 0)),
            pl.BlockSpec((PROJ_TM, SMALL_W), lambda i, j: (i, 0)),
            pl.BlockSpec((PROJ_TM, BRANCH_WIDTH), lambda i, j: (i, j)),
        ],
        out_shape=[
            jax.ShapeDtypeStruct((t, LAT_DIM), BF16),
            jax.ShapeDtypeStruct((t, SMALL_W), F32),
            jax.ShapeDtypeStruct((t, REST_W), BF16),
        ],
        scratch_shapes=[pltpu.VMEM((PROJ_TM, D_MODEL), BF16)],
        compiler_params=pltpu.CompilerParams(
            dimension_semantics=("arbitrary", "arbitrary"),
            vmem_limit_bytes=VMEM_LIMIT),
        name="proj",
    )(x2, g_pre, w_all, w_all, w_all)


def _split3(x):
    hi = x.astype(BF16).astype(F32)
    r = x - hi
    mid = r.astype(BF16).astype(F32)
    lo = (r - mid).astype(BF16).astype(F32)
    return hi, mid, lo


def _prep_kernel(lat_ref, sm_ref, fq_ref, fk_ref, fv_ref, pos_ref,
                 gq_ref, gkv_ref, wuqt_ref, wk_ref, wvt_ref, bf_ref, invf_ref,
                 qt_ref, kp_ref, vt_ref, carry_scr):
    tm = lat_ref.shape[0]

    def put_t(ref, r0, x):
        ref[0, r0:r0 + x.shape[1], :] = x.T

    @pl.when(pl.program_id(1) == 0)
    def _():
        carry_scr[...] = jnp.zeros_like(carry_scr)

    lat = lat_ref[...].astype(F32)
    qn = _rms(lat[:, :Q_RANK], gq_ref[...]).astype(BF16)
    kvn = _rms(lat[:, Q_RANK:], gkv_ref[...]).astype(BF16)

    ang = invf_ref[...] * pos_ref[0].astype(F32)
    cos_t = jnp.cos(ang)
    sin_t = jnp.sin(ang)

    def rope_t(x):
        x1, x2 = x[:ROPE_HALF], x[ROPE_HALF:ROPE_DIM]
        return x1 * cos_t - x2 * sin_t, x2 * cos_t + x1 * sin_t

    kr1, kr2 = rope_t(sm_ref[:, :LANES].T)
    k_rope = jnp.concatenate(
        [kr1, kr2, jnp.zeros((LANES - ROPE_DIM, tm), F32)], axis=0).T.astype(BF16)

    lane = lax.broadcasted_iota(jnp.int32, (tm, LANES), 1)
    fl = sm_ref[:, LANES:] + bf_ref[...]
    logf = jnp.minimum(fl, 0.0) - jnp.log1p(jnp.exp(-jnp.abs(fl)))
    row = lax.broadcasted_iota(jnp.int32, (tm, tm), 0)
    col = lax.broadcasted_iota(jnp.int32, (tm, tm), 1)
    tri = jnp.where(row >= col, 1.0, 0.0).astype(BF16)
    c = carry_scr[...]
    for piece in _split3(logf):
        c = c + jnp.dot(tri, piece.astype(BF16), preferred_element_type=F32)
    carry_scr[...] = c[tm - 1:tm, :]
    c2 = c * LOG2E
    c2_t = c2.T

    aug = 16
    srow = lax.broadcasted_iota(jnp.int32, (aug, tm), 0)
    q_zero = jnp.zeros((FEAT - HEAD_DIM - aug, tm), BF16)
    q_pad = jnp.zeros((FEAT - QK_DIM, tm), BF16)

    def fox_head(h):
        c0 = (N_HEADS + h) * FEAT
        s0 = h * HEAD_DIM
        hi, mid, lo = _split3(jnp.broadcast_to(c2[:, h:h + 1], (tm, LANES)))
        ck = jnp.where(lane < 3, 1.0, jnp.where(lane == 3, -hi, jnp.where(
            lane == 4, -mid, jnp.where(lane == 5, -lo, 0.0))))
        kp_ref[:, c0:c0 + LANES] = fk_ref[:, s0:s0 + HEAD_DIM]
        kp_ref[:, c0 + LANES:c0 + FEAT] = ck.astype(BF16)
        hi_t, mid_t, lo_t = _split3(c2_t[h:h + 1, :])
        cq_t = jnp.where(srow == 0, hi_t, jnp.where(srow == 1, mid_t, jnp.where(
            srow == 2, lo_t, jnp.where(srow < 6, 1.0, 0.0))))
        put_t(qt_ref, c0, fq_ref[:, s0:s0 + HEAD_DIM])
        qt_ref[0, c0 + HEAD_DIM:c0 + HEAD_DIM + aug, :] = cq_t.astype(BF16)
        qt_ref[0, c0 + HEAD_DIM + aug:c0 + FEAT, :] = q_zero
        put_t(vt_ref, (N_HEADS + h) * V_ROWS, fv_ref[:, s0:s0 + HEAD_DIM])

    def mla_q_head(h, q_nope, base, q_rope):
        c0, l0 = h * FEAT, (h - base) * NOPE_DIM
        qr1, qr2 = rope_t(q_rope[h * ROPE_DIM:(h + 1) * ROPE_DIM])
        qt_ref[0, c0:c0 + NOPE_DIM, :] = q_nope[l0:l0 + NOPE_DIM].astype(BF16)
        qt_ref[0, c0 + NOPE_DIM:c0 + NOPE_DIM + ROPE_HALF, :] = qr1.astype(BF16)
        qt_ref[0, c0 + NOPE_DIM + ROPE_HALF:c0 + QK_DIM, :] = qr2.astype(BF16)
        qt_ref[0, c0 + QK_DIM:c0 + FEAT, :] = q_pad

    half = N_HEADS // 2
    n_nope = N_HEADS * NOPE_DIM
    q_a = _dot_nt(wuqt_ref[:n_nope // 2, :], qn) * MLA_QSCALE
    for h in range(half):
        fox_head(h)
    q_b = _dot_nt(wuqt_ref[n_nope // 2:n_nope, :], qn) * MLA_QSCALE
    for h in range(half, N_HEADS):
        fox_head(h)
    q_r = _dot_nt(wuqt_ref[n_nope:, :], qn) * MLA_QSCALE
    k_nope = jnp.dot(kvn, wk_ref[...], preferred_element_type=F32)
    for h in range(half):
        mla_q_head(h, q_a, 0, q_r)
    v_t = _dot_nt(wvt_ref[...], kvn)
    for h in range(half, N_HEADS):
        mla_q_head(h, q_b, half, q_r)
    for h in range(N_HEADS):
        c0, d0 = h * FEAT, h * HEAD_DIM
        kp_ref[:, c0:c0 + LANES] = k_nope[:, d0:d0 + HEAD_DIM].astype(BF16)
        kp_ref[:, c0 + LANES:c0 + FEAT] = k_rope
        vt_ref[0, h * V_ROWS:h * V_ROWS + HEAD_DIM, :] = v_t[d0:d0 + HEAD_DIM].astype(BF16)
    pad = V_ROWS - HEAD_DIM
    ones_blk = jnp.where(lax.broadcasted_iota(jnp.int32, (pad, tm), 0) == 0,
                         1.0, 0.0).astype(BF16)
    for h in range(2 * N_HEADS):
        vt_ref[0, h * V_ROWS + HEAD_DIM:(h + 1) * V_ROWS, :] = ones_blk


def _prep(lat, small, rest, pos, g_q, g_kv, w_uq_t, w_k, w_v_t, b_f, inv_f, batch, seq):
    t = lat.shape[0]
    ns = seq // PREP_TM
    per = ATT_T // PREP_TM
    nt = t // ATT_T
    row = lambda b, i: (b * ns + i, 0)
    const = lambda b, i: (0, 0)
    tile_t = lambda b, i: ((b * ns + i) // per, 0, (b * ns + i) % per)
    return pl.pallas_call(
        _prep_kernel,
        grid=(batch, ns),
        in_specs=[
            pl.BlockSpec((PREP_TM, LAT_DIM), row),
            pl.BlockSpec((PREP_TM, SMALL_W), row),
            pl.BlockSpec((PREP_TM, BRANCH_WIDTH), lambda b, i: (b * ns + i, 1)),
            pl.BlockSpec((PREP_TM, BRANCH_WIDTH), lambda b, i: (b * ns + i, 2)),
            pl.BlockSpec((PREP_TM, BRANCH_WIDTH), lambda b, i: (b * ns + i, 3)),
            pl.BlockSpec((1, 1, PREP_TM), lambda b, i: (b * ns + i, 0, 0)),
            pl.BlockSpec((1, Q_RANK), const),
            pl.BlockSpec((1, KV_RANK), const),
            pl.BlockSpec((N_HEADS * QK_DIM, Q_RANK), const),
            pl.BlockSpec((KV_RANK, BRANCH_WIDTH), const),
            pl.BlockSpec((BRANCH_WIDTH, KV_RANK), const),
            pl.BlockSpec((1, LANES), const),
            pl.BlockSpec((ROPE_HALF, 1), const),
        ],
        out_specs=[
            pl.BlockSpec((1, 2 * N_HEADS * FEAT, PREP_TM), tile_t),
            pl.BlockSpec((PREP_TM, 2 * N_HEADS * FEAT), row),
            pl.BlockSpec((1, 2 * N_HEADS * V_ROWS, PREP_TM), tile_t),
        ],
        out_shape=[
            jax.ShapeDtypeStruct((nt, 2 * N_HEADS * FEAT, ATT_T), BF16),
            jax.ShapeDtypeStruct((t, 2 * N_HEADS * FEAT), BF16),
            jax.ShapeDtypeStruct((nt, 2 * N_HEADS * V_ROWS, ATT_T), BF16),
        ],
        scratch_shapes=[pltpu.VMEM((1, LANES), F32)],
        compiler_params=pltpu.CompilerParams(
            dimension_semantics=("arbitrary", "arbitrary"),
            vmem_limit_bytes=VMEM_LIMIT),
        name="prep",
    )(lat, small, rest, rest, rest, pos, g_q, g_kv, w_uq_t, w_k, w_v_t, b_f, inv_f)


def _attn_kernel(qt_ref, k_ref, vt_ref, o_ref, s0, s1, cm0, cm1, m_scr, acc_scr):
    t = ATT_T
    nq = qt_ref.shape[0]
    nsteps = nq * (nq + 1) // 2
    s_buf, cm_buf = (s0, s1), (cm0, cm1)

    def advance(ij):
        i, j = ij
        wrap = j >= i
        return jnp.where(wrap, i + 1, i), jnp.where(wrap, 0, j + 1)

    nblk = t // ATT_BLK
    blk = ATT_BLK

    def a_begin():
        return [[None] * nblk for _ in range(ATT_HB)]

    def a_piece(ij, slot, diagonal, hh, r, cmax):
        i, j = ij
        f0, r0 = hh * FEAT, r * blk
        k = k_ref[pl.ds(pl.multiple_of(j * t + r0, blk), blk), f0:f0 + FEAT]
        c_lo = r0 if diagonal else 0
        s = jnp.dot(k, qt_ref[i, f0:f0 + FEAT, c_lo:], preferred_element_type=F32)
        if diagonal:
            key = lax.broadcasted_iota(jnp.int32, (blk, blk), 0)
            qry = lax.broadcasted_iota(jnp.int32, (blk, blk), 1)
            parts = [jnp.where(key <= qry, s[:, :blk], -jnp.inf)]
            if s.shape[1] > blk:
                parts.append(s[:, blk:])
            s = jnp.concatenate(parts, axis=1)
        s_buf[slot][hh, r0:r0 + blk, c_lo:] = s
        for qb in range(c_lo // blk, nblk):
            part = jnp.max(s[:, qb * blk - c_lo:(qb + 1) * blk - c_lo], axis=0, keepdims=True)
            cmax[hh][qb] = part if cmax[hh][qb] is None else jnp.maximum(cmax[hh][qb], part)

    def a_end(slot, hh, cmax):
        for qb in range(nblk):
            cm_buf[slot][hh, :, qb * blk:(qb + 1) * blk] = cmax[hh][qb]

    def bc_begin(ij, slot, hh):
        i, j = ij
        m_old = jnp.where(j == 0, -jnp.inf, m_scr[hh])
        m_new = jnp.maximum(m_old, cm_buf[slot][hh])
        return dict(m_new=m_new, alpha=jnp.exp2(m_old - m_new), pv=None)

    def bc_piece(ij, slot, hh, r, st):
        i, j = ij
        d0, r0 = hh * V_ROWS, r * blk
        p = jnp.exp2(s_buf[slot][hh, r0:r0 + blk, :] - st["m_new"])
        pv = jnp.dot(vt_ref[j, d0:d0 + V_ROWS, r0:r0 + blk], p.astype(BF16),
                     preferred_element_type=F32)
        st["pv"] = pv if st["pv"] is None else st["pv"] + pv

    def bc_end(hh, st):
        m_scr[hh] = st["m_new"]
        acc_scr[hh] = st["alpha"] * acc_scr[hh] + st["pv"]

    def bcd_begin(ij, slot, hh):
        st = bc_begin(ij, slot, hh)
        st["pv"] = [None] * nblk
        return st

    def bcd_piece(ij, slot, hh, r, st):
        i, j = ij
        d0, r0 = hh * V_ROWS, r * blk
        p = jnp.exp2(s_buf[slot][hh, r0:r0 + blk, r0:] - st["m_new"][:, r0:])
        pv = jnp.dot(vt_ref[j, d0:d0 + V_ROWS, r0:r0 + blk], p.astype(BF16),
                     preferred_element_type=F32)
        for qb in range(r, nblk):
            part = pv[:, (qb - r) * blk:(qb - r + 1) * blk]
            st["pv"][qb] = part if st["pv"][qb] is None else st["pv"][qb] + part

    def bcd_end(ij, hh, st):
        i, j = ij
        for qb in range(nblk):
            cols = slice(qb * blk, (qb + 1) * blk)
            acc = st["alpha"][:, cols] * acc_scr[hh, :, cols] + st["pv"][qb]
            out = acc[:HEAD_DIM] / acc[HEAD_DIM:HEAD_DIM + 1]
            o_ref[pl.ds(pl.multiple_of(i * t + qb * blk, blk), blk),
                  hh * HEAD_DIM:(hh + 1) * HEAD_DIM] = out.T.astype(BF16)

    def run(slot_a, ija, ijb, a_is_diag, b_is_diag):
        slot_b = 1 - slot_a
        begin, piece = (bcd_begin, bcd_piece) if b_is_diag else (bc_begin, bc_piece)
        cmax = a_begin()
        sts = [begin(ijb, slot_b, hh) for hh in range(ATT_HB)]
        for r in range(nblk):
            for hh in range(ATT_HB):
                piece(ijb, slot_b, hh, r, sts[hh])
                if ija is not None:
                    a_piece(ija, slot_a, a_is_diag, hh, r, cmax)
        for hh in range(ATT_HB):
            if ija is not None:
                a_end(slot_a, hh, cmax)
            if b_is_diag:
                bcd_end(ijb, hh, sts[hh])
            else:
                bc_end(hh, sts[hh])

    def iteration(slot_a, ija, ijb):
        a_diag = ija[1] == ija[0]
        b_diag = ijb[1] == ijb[0]
        pl.when(b_diag)(lambda: run(slot_a, ija, ijb, False, True))
        pl.when(a_diag)(lambda: run(slot_a, ija, ijb, True, False))
        pl.when(jnp.logical_not(jnp.logical_or(a_diag, b_diag)))(
            lambda: run(slot_a, ija, ijb, False, False))

    zero = jnp.int32(0)
    m_scr[...] = jnp.full(m_scr.shape, -jnp.inf, F32)
    acc_scr[...] = jnp.zeros(acc_scr.shape, F32)
    cmax0 = a_begin()
    for hh in range(ATT_HB):
        for r in range(nblk):
            a_piece((zero, zero), 0, True, hh, r, cmax0)
        a_end(0, hh, cmax0)


    def pair(_, carry):
        ija, ijb = carry
        iteration(1, ija, ijb)
        ija2 = advance(ija)
        iteration(0, ija2, ija)
        return advance(ija2), ija2

    lax.fori_loop(0, nsteps // 2 - 1, pair, (advance((zero, zero)), (zero, zero)))
    last = (jnp.int32(nq - 1), jnp.int32(nq - 1))
    run(1, last, (jnp.int32(nq - 1), jnp.int32(nq - 2)), True, False)
    run(0, None, last, False, True)


def _attn(qt, kp, vt, batch, seq):
    t = kp.shape[0]
    nq = seq // ATT_T
    assert nq >= 2 and (nq * (nq + 1) // 2) % 2 == 0
    nh = 2 * N_HEADS // ATT_HB
    row = pltpu.VMEM((ATT_HB, 1, ATT_T), F32)
    tile = pltpu.VMEM((ATT_HB, ATT_T, ATT_T), F32)
    return pl.pallas_call(
        _attn_kernel,
        grid=(batch, nh),
        in_specs=[
            pl.BlockSpec((nq, ATT_HB * FEAT, ATT_T), lambda b, h: (b, h, 0)),
            pl.BlockSpec((seq, ATT_HB * FEAT), lambda b, h: (b, h)),
            pl.BlockSpec((nq, ATT_HB * V_ROWS, ATT_T), lambda b, h: (b, h, 0)),
        ],
        out_specs=pl.BlockSpec((seq, ATT_HB * HEAD_DIM), lambda b, h: (b, h)),
        out_shape=jax.ShapeDtypeStruct((t, D_MIX), BF16),
        scratch_shapes=[
            tile, tile, row, row, row,
            pltpu.VMEM((ATT_HB, V_ROWS, ATT_T), F32),
        ],
        compiler_params=pltpu.CompilerParams(
            dimension_semantics=("arbitrary", "arbitrary"),
            vmem_limit_bytes=VMEM_LIMIT),
        name="attn",
    )(qt, kp, vt)


def _silu(g):
    return g / (1.0 + jnp.exp(-g))


def _out_kernel(o_ref, gm_ref, gf_ref, x_ref, w_ref, g_ref, y_ref):
    o = o_ref[...].astype(F32)
    om = (o[:, :BRANCH_WIDTH] * _silu(gm_ref[...].astype(F32))).astype(BF16)
    of = (o[:, BRANCH_WIDTH:] * _silu(gf_ref[...].astype(F32))).astype(BF16)
    y = (jnp.dot(om, w_ref[:BRANCH_WIDTH, :], preferred_element_type=F32)
         + jnp.dot(of, w_ref[BRANCH_WIDTH:, :], preferred_element_type=F32))
    y_ref[...] = x_ref[...] + _rms(y, g_ref[...])


def _out(o, rest, x2, w_out, g_post):
    t = o.shape[0]
    return pl.pallas_call(
        _out_kernel,
        grid=(t // OUT_TM,),
        in_specs=[
            pl.BlockSpec((OUT_TM, D_MIX), lambda i: (i, 0)),
            pl.BlockSpec((OUT_TM, BRANCH_WIDTH), lambda i: (i, 0)),
            pl.BlockSpec((OUT_TM, BRANCH_WIDTH), lambda i: (i, 4)),
            pl.BlockSpec((OUT_TM, D_MODEL), lambda i: (i, 0)),
            pl.BlockSpec((D_MIX, D_MODEL), lambda i: (0, 0)),
            pl.BlockSpec((1, D_MODEL), lambda i: (0, 0)),
        ],
        out_specs=pl.BlockSpec((OUT_TM, D_MODEL), lambda i: (i, 0)),
        out_shape=jax.ShapeDtypeStruct((t, D_MODEL), F32),
        compiler_params=pltpu.CompilerParams(
            dimension_semantics=("arbitrary",),
            vmem_limit_bytes=VMEM_LIMIT),
        name="out",
    )(o, rest, rest, x2, w_out, g_post)


def _layer(x2, pos, inv_f, g_pre, w_in, g_q, w_uq, g_kv, w_ukv, b_forget,
           w_out, g_post, batch, seq):
    w_all = _pack_w_in(jnp.swapaxes(w_in, 0, 1))
    w_q = w_uq.T.reshape(N_HEADS, QK_DIM, Q_RANK)
    w_uq_t = jnp.concatenate([
        w_q[:, :NOPE_DIM].reshape(N_HEADS * NOPE_DIM, Q_RANK),
        w_q[:, NOPE_DIM:].reshape(N_HEADS * ROPE_DIM, Q_RANK)], axis=0).astype(BF16)
    w_kv = w_ukv.reshape(KV_RANK, N_HEADS, NOPE_DIM + HEAD_DIM)
    w_k = w_kv[:, :, :NOPE_DIM].reshape(KV_RANK, BRANCH_WIDTH).astype(BF16)
    w_v_t = w_kv[:, :, NOPE_DIM:].reshape(KV_RANK, BRANCH_WIDTH).T.astype(BF16)
    w_out_b = w_out.astype(BF16)
    b_f = jnp.pad(b_forget, (0, LANES - N_HEADS))[None, :]

    lat, small, rest = _proj(x2, g_pre[None, :], w_all)
    qt, kp, vt = _prep(lat, small, rest, pos, g_q[None, :], g_kv[None, :],
                       w_uq_t, w_k, w_v_t, b_f, inv_f, batch, seq)
    o = _attn(qt, kp, vt, batch, seq)
    return _out(o, rest, x2, w_out_b, g_post[None, :])


def kernel(x, positions, g_pre, w_in, g_q_latent, w_uq, g_kv_latent, w_ukv,
           b_forget, w_out, g_post):
    batch, seq, d = x.shape
    depth = g_pre.shape[0]
    x2 = x.reshape(batch * seq, d)
    pos = positions.reshape(batch * seq // PREP_TM, 1, PREP_TM)
    inv_f = (ROPE_THETA ** (-jnp.arange(0, ROPE_DIM, 2, dtype=F32) / ROPE_DIM))[:, None]
    for l in range(depth):
        x2 = _layer(x2, pos, inv_f, g_pre[l], w_in[l], g_q_latent[l], w_uq[l],
                    g_kv_latent[l], w_ukv[l], b_forget[l], w_out[l], g_post[l],
                    batch, seq)
    return x2.reshape(batch, seq, d)
```

```python
import math

import jax
import jax.numpy as jnp
from jax import lax
from jax.experimental import pallas as pl
from jax.experimental.pallas import tpu as pltpu

F32 = jnp.float32
BF16 = jnp.bfloat16

D_MODEL = 2048
N_HEADS = 8
HEAD_DIM = 128
NOPE_DIM = 128
ROPE_DIM = 64
ROPE_HALF = ROPE_DIM // 2
QK_DIM = NOPE_DIM + ROPE_DIM
Q_RANK = 768
KV_RANK = 512
LAT_DIM = Q_RANK + KV_RANK
BRANCH_WIDTH = N_HEADS * HEAD_DIM
D_MIX = 2 * BRANCH_WIDTH
ROPE_THETA = 10000.0
NORM_EPS = 1e-6
LOG2E = math.log2(math.e)

LANES = 128
FEAT = 256
SMALL_W = 2 * LANES
REST_W = 5 * BRANCH_WIDTH

MLA_QSCALE = QK_DIM ** -0.5 * LOG2E
FOX_QSCALE = HEAD_DIM ** -0.5 * LOG2E

PACK_TN = 256
PROJ_TM = 1024
PREP_TM = 512
ATT_T = 1024
ATT_BLK = 256
ATT_HB = 2
ATT_GROUP = 6
V_ROWS = HEAD_DIM + 16
OUT_TM = 512

V7X_VMEM_BYTES = 64 * 1024 * 1024
VMEM_LIMIT = V7X_VMEM_BYTES * 7 // 8


def _rms(x, g):
    return x * lax.rsqrt(jnp.mean(x * x, axis=-1, keepdims=True) + NORM_EPS) * g


def _pack_kernel(w_ref, o_ref):
    o_kr = LAT_DIM
    o_gm = o_kr + ROPE_DIM
    o_fl = o_gm + 4 * BRANCH_WIDTH
    o_gf = o_fl + N_HEADS
    cols = w_ref.shape[1]
    zeros = lambda n: jnp.zeros((n, cols), F32)
    o_ref[...] = jnp.concatenate([
        w_ref[o_gm:o_fl, :], w_ref[o_gf:o_gf + BRANCH_WIDTH, :],
        w_ref[:LAT_DIM, :],
        w_ref[o_kr:o_gm, :], zeros(LANES - ROPE_DIM),
        w_ref[o_fl:o_gf, :], zeros(LANES - N_HEADS),
    ], axis=0).astype(BF16)


def _pack_w_in(w_in_t):
    d_in, d = w_in_t.shape
    height = REST_W + LAT_DIM + SMALL_W
    return pl.pallas_call(
        _pack_kernel,
        grid=(d // PACK_TN,),
        in_specs=[pl.BlockSpec((d_in, PACK_TN), lambda i: (0, i))],
        out_specs=pl.BlockSpec((height, PACK_TN), lambda i: (0, i)),
        out_shape=jax.ShapeDtypeStruct((height, d), BF16),
        compiler_params=pltpu.CompilerParams(
            dimension_semantics=("arbitrary",),
            vmem_limit_bytes=VMEM_LIMIT),
        name="pack",
    )(w_in_t)


def _dot_nt(a, b):
    return lax.dot_general(a, b, (((1,), (1,)), ((), ())), preferred_element_type=F32)


def _proj_kernel(x_ref, g_ref, wlat_ref, wsm_ref, wrest_ref,
                 lat_ref, sm_ref, rest_ref, h_scr):
    j = pl.program_id(1)

    @pl.when(j == 0)
    def _():
        hb = _rms(x_ref[...], g_ref[...]).astype(BF16)
        h_scr[...] = hb
        lat_ref[...] = _dot_nt(hb, wlat_ref[...]).astype(BF16)
        sm_ref[...] = _dot_nt(hb, wsm_ref[...])

    acc = _dot_nt(h_scr[...], wrest_ref[...])
    scale = jnp.where(j == 1, FOX_QSCALE, 1.0).astype(F32)
    rest_ref[...] = (acc * scale).astype(BF16)


def _proj(x2, g_pre, w_all):
    t = x2.shape[0]
    n_rest = REST_W // BRANCH_WIDTH
    assert REST_W % LAT_DIM == 0 and (REST_W + LAT_DIM) % SMALL_W == 0
    once = pl.Buffered(1)
    return pl.pallas_call(
        _proj_kernel,
        grid=(t // PROJ_TM, n_rest),
        in_specs=[
            pl.BlockSpec((PROJ_TM, D_MODEL), lambda i, j: (i, 0)),
            pl.BlockSpec((1, D_MODEL), lambda i, j: (0, 0)),
            pl.BlockSpec((LAT_DIM, D_MODEL), lambda i, j: (REST_W // LAT_DIM, 0),
                         pipeline_mode=once),
            pl.BlockSpec((SMALL_W, D_MODEL), lambda i, j: ((REST_W + LAT_DIM) // SMALL_W, 0),
                         pipeline_mode=once),
            pl.BlockSpec((BRANCH_WIDTH, D_MODEL), lambda i, j: (j, 0)),
        ],
        out_specs=[
            pl.BlockSpec((PROJ_TM, LAT_DIM), lambda i, j: (i, 0)),
            pl.BlockSpec((PROJ_TM, SMALL_W), lambda i, j: (i, 0)),
            pl.BlockSpec((PROJ_TM, BRANCH_WIDTH), lambda i, j: (i, j)),
        ],
        out_shape=[
            jax.ShapeDtypeStruct((t, LAT_DIM), BF16),
            jax.ShapeDtypeStruct((t, SMALL_W), F32),
            jax.ShapeDtypeStruct((t, REST_W), BF16),
        ],
        scratch_shapes=[pltpu.VMEM((PROJ_TM, D_MODEL), BF16)],
        compiler_params=pltpu.CompilerParams(
            dimension_semantics=("arbitrary", "arbitrary"),
            vmem_limit_bytes=VMEM_LIMIT),
        name="proj",
    )(x2, g_pre, w_all, w_all, w_all)


def _split3(x):
    hi = x.astype(BF16).astype(F32)
    r = x - hi
    mid = r.astype(BF16).astype(F32)
    lo = (r - mid).astype(BF16).astype(F32)
    return hi, mid, lo


def _prep_kernel(lat_ref, sm_ref, fq_ref, fk_ref, fv_ref, pos_ref,
                 gq_ref, gkv_ref, wuqt_ref, wk_ref, wvt_ref, bf_ref, invf_ref,
                 qt_ref, kp_ref, vt_ref, carry_scr):
    tm = lat_ref.shape[0]

    def put_t(ref, r0, x):
        ref[0, r0:r0 + x.shape[1], :] = x.T

    @pl.when(pl.program_id(1) == 0)
    def _():
        carry_scr[...] = jnp.zeros_like(carry_scr)

    lat = lat_ref[...].astype(F32)
    qn = _rms(lat[:, :Q_RANK], gq_ref[...]).astype(BF16)
    kvn = _rms(lat[:, Q_RANK:], gkv_ref[...]).astype(BF16)

    ang = invf_ref[...] * pos_ref[0].astype(F32)
    cos_t = jnp.cos(ang)
    sin_t = jnp.sin(ang)

    def rope_t(x):
        x1, x2 = x[:ROPE_HALF], x[ROPE_HALF:ROPE_DIM]
        return x1 * cos_t - x2 * sin_t, x2 * cos_t + x1 * sin_t

    kr1, kr2 = rope_t(sm_ref[:, :LANES].T)
    k_rope = jnp.concatenate(
        [kr1, kr2, jnp.zeros((LANES - ROPE_DIM, tm), F32)], axis=0).T.astype(BF16)

    lane = lax.broadcasted_iota(jnp.int32, (tm, LANES), 1)
    fl = sm_ref[:, LANES:] + bf_ref[...]
    logf = jnp.minimum(fl, 0.0) - jnp.log1p(jnp.exp(-jnp.abs(fl)))
    row = lax.broadcasted_iota(jnp.int32, (tm, tm), 0)
    col = lax.broadcasted_iota(jnp.int32, (tm, tm), 1)
    tri = jnp.where(row >= col, 1.0, 0.0).astype(BF16)
    c = carry_scr[...]
    for piece in _split3(logf):
        c = c + jnp.dot(tri, piece.astype(BF16), preferred_element_type=F32)
    carry_scr[...] = c[tm - 1:tm, :]
    c2 = c * LOG2E
    c2_t = c2.T

    aug = 16
    srow = lax.broadcasted_iota(jnp.int32, (aug, tm), 0)
    q_zero = jnp.zeros((FEAT - HEAD_DIM - aug, tm), BF16)
    q_pad = jnp.zeros((FEAT - QK_DIM, tm), BF16)

    def fox_head(h):
        c0 = (N_HEADS + h) * FEAT
        s0 = h * HEAD_DIM
        hi, mid, lo = _split3(jnp.broadcast_to(c2[:, h:h + 1], (tm, LANES)))
        ck = jnp.where(lane < 3, 1.0, jnp.where(lane == 3, -hi, jnp.where(
            lane == 4, -mid, jnp.where(lane == 5, -lo, 0.0))))
        kp_ref[:, c0:c0 + LANES] = fk_ref[:, s0:s0 + HEAD_DIM]
        kp_ref[:, c0 + LANES:c0 + FEAT] = ck.astype(BF16)
        hi_t, mid_t, lo_t = _split3(c2_t[h:h + 1, :])
        cq_t = jnp.where(srow == 0, hi_t, jnp.where(srow == 1, mid_t, jnp.where(
            srow == 2, lo_t, jnp.where(srow < 6, 1.0, 0.0))))
        put_t(qt_ref, c0, fq_ref[:, s0:s0 + HEAD_DIM])
        qt_ref[0, c0 + HEAD_DIM:c0 + HEAD_DIM + aug, :] = cq_t.astype(BF16)
        qt_ref[0, c0 + HEAD_DIM + aug:c0 + FEAT, :] = q_zero
        put_t(vt_ref, (N_HEADS + h) * V_ROWS, fv_ref[:, s0:s0 + HEAD_DIM])

    def mla_q_head(h, q_nope, base, q_rope):
        c0, l0 = h * FEAT, (h - base) * NOPE_DIM
        qr1, qr2 = rope_t(q_rope[h * ROPE_DIM:(h + 1) * ROPE_DIM])
        qt_ref[0, c0:c0 + NOPE_DIM, :] = q_nope[l0:l0 + NOPE_DIM].astype(BF16)
        qt_ref[0, c0 + NOPE_DIM:c0 + NOPE_DIM + ROPE_HALF, :] = qr1.astype(BF16)
        qt_ref[0, c0 + NOPE_DIM + ROPE_HALF:c0 + QK_DIM, :] = qr2.astype(BF16)
        qt_ref[0, c0 + QK_DIM:c0 + FEAT, :] = q_pad

    half = N_HEADS // 2
    n_nope = N_HEADS * NOPE_DIM
    q_a = _dot_nt(wuqt_ref[:n_nope // 2, :], qn) * MLA_QSCALE
    for h in range(half):
        fox_head(h)
    q_b = _dot_nt(wuqt_ref[n_nope // 2:n_nope, :], qn) * MLA_QSCALE
    for h in range(half, N_HEADS):
        fox_head(h)
    q_r = _dot_nt(wuqt_ref[n_nope:, :], qn) * MLA_QSCALE
    k_nope = jnp.dot(kvn, wk_ref[...], preferred_element_type=F32)
    for h in range(half):
        mla_q_head(h, q_a, 0, q_r)
    v_t = _dot_nt(wvt_ref[...], kvn)
    for h in range(half, N_HEADS):
        mla_q_head(h, q_b, half, q_r)
    for h in range(N_HEADS):
        c0, d0 = h * FEAT, h * HEAD_DIM
        kp_ref[:, c0:c0 + LANES] = k_nope[:, d0:d0 + HEAD_DIM].astype(BF16)
        kp_ref[:, c0 + LANES:c0 + FEAT] = k_rope
        vt_ref[0, h * V_ROWS:h * V_ROWS + HEAD_DIM, :] = v_t[d0:d0 + HEAD_DIM].astype(BF16)
    pad = V_ROWS - HEAD_DIM
    ones_blk = jnp.where(lax.broadcasted_iota(jnp.int32, (pad, tm), 0) == 0,
                         1.0, 0.0).astype(BF16)
    for h in range(2 * N_HEADS):
        vt_ref[0, h * V_ROWS + HEAD_DIM:(h + 1) * V_ROWS, :] = ones_blk


def _prep(lat, small, rest, pos, g_q, g_kv, w_uq_t, w_k, w_v_t, b_f, inv_f, batch, seq):
    t = lat.shape[0]
    ns = seq // PREP_TM
    per = ATT_T // PREP_TM
    nt = t // ATT_T
    row = lambda b, i: (b * ns + i, 0)
    const = lambda b, i: (0, 0)
    tile_t = lambda b, i: ((b * ns + i) // per, 0, (b * ns + i) % per)
    return pl.pallas_call(
        _prep_kernel,
        grid=(batch, ns),
        in_specs=[
            pl.BlockSpec((PREP_TM, LAT_DIM), row),
            pl.BlockSpec((PREP_TM, SMALL_W), row),
            pl.BlockSpec((PREP_TM, BRANCH_WIDTH), lambda b, i: (b * ns + i, 1)),
            pl.BlockSpec((PREP_TM, BRANCH_WIDTH), lambda b, i: (b * ns + i, 2)),
            pl.BlockSpec((PREP_TM, BRANCH_WIDTH), lambda b, i: (b * ns + i, 3)),
            pl.BlockSpec((1, 1, PREP_TM), lambda b, i: (b * ns + i, 0, 0)),
            pl.BlockSpec((1, Q_RANK), const),
            pl.BlockSpec((1, KV_RANK), const),
            pl.BlockSpec((N_HEADS * QK_DIM, Q_RANK), const),
            pl.BlockSpec((KV_RANK, BRANCH_WIDTH), const),
            pl.BlockSpec((BRANCH_WIDTH, KV_RANK), const),
            pl.BlockSpec((1, LANES), const),
            pl.BlockSpec((ROPE_HALF, 1), const),
        ],
        out_specs=[
            pl.BlockSpec((1, 2 * N_HEADS * FEAT, PREP_TM), tile_t),
            pl.BlockSpec((PREP_TM, 2 * N_HEADS * FEAT), row),
            pl.BlockSpec((1, 2 * N_HEADS * V_ROWS, PREP_TM), tile_t),
        ],
        out_shape=[
            jax.ShapeDtypeStruct((nt, 2 * N_HEADS * FEAT, ATT_T), BF16),
            jax.ShapeDtypeStruct((t, 2 * N_HEADS * FEAT), BF16),
            jax.ShapeDtypeStruct((nt, 2 * N_HEADS * V_ROWS, ATT_T), BF16),
        ],
        scratch_shapes=[pltpu.VMEM((1, LANES), F32)],
        compiler_params=pltpu.CompilerParams(
            dimension_semantics=("arbitrary", "arbitrary"),
            vmem_limit_bytes=VMEM_LIMIT),
        name="prep",
    )(lat, small, rest, rest, rest, pos, g_q, g_kv, w_uq_t, w_k, w_v_t, b_f, inv_f)


def _attn_kernel(qt_ref, k_ref, vt_ref, o_ref, s0, s1, cm0, cm1, m_scr, acc_scr):
    t = ATT_T
    nq = qt_ref.shape[0]
    steps = [(i, j) for i in range(nq) for j in range(i + 1)]
    nsteps = len(steps)
    s_buf, cm_buf = (s0, s1), (cm0, cm1)

    nblk = t // ATT_BLK
    blk = ATT_BLK

    def a_begin():
        return [[None] * nblk for _ in range(ATT_HB)]

    def a_piece(ij, slot, diagonal, hh, r, cmax):
        i, j = ij
        f0, r0 = hh * FEAT, r * blk
        k = k_ref[j * t + r0:j * t + r0 + blk, f0:f0 + FEAT]
        c_lo = r0 if diagonal else 0
        s = jnp.dot(k, qt_ref[i, f0:f0 + FEAT, c_lo:], preferred_element_type=F32)
        if diagonal:
            key = lax.broadcasted_iota(jnp.int32, (blk, blk), 0)
            qry = lax.broadcasted_iota(jnp.int32, (blk, blk), 1)
            parts = [jnp.where(key <= qry, s[:, :blk], -jnp.inf)]
            if s.shape[1] > blk:
                parts.append(s[:, blk:])
            s = jnp.concatenate(parts, axis=1)
        s_buf[slot][hh, r0:r0 + blk, c_lo:] = s
        for qb in range(c_lo // blk, nblk):
            part = jnp.max(s[:, qb * blk - c_lo:(qb + 1) * blk - c_lo], axis=0, keepdims=True)
            cmax[hh][qb] = part if cmax[hh][qb] is None else jnp.maximum(cmax[hh][qb], part)

    def a_end(slot, hh, cmax):
        for qb in range(nblk):
            cm_buf[slot][hh, :, qb * blk:(qb + 1) * blk] = cmax[hh][qb]

    def bc_begin(ij, slot, hh):
        i, j = ij
        m_old = jnp.full((1, t), -jnp.inf, F32) if j == 0 else m_scr[hh]
        m_new = jnp.maximum(m_old, cm_buf[slot][hh])
        return dict(m_new=m_new, alpha=jnp.exp2(m_old - m_new), pv=None)

    def bc_piece(ij, slot, hh, r, st):
        i, j = ij
        d0, r0 = hh * V_ROWS, r * blk
        p = jnp.exp2(s_buf[slot][hh, r0:r0 + blk, :] - st["m_new"])
        pv = jnp.dot(vt_ref[j, d0:d0 + V_ROWS, r0:r0 + blk], p.astype(BF16),
                     preferred_element_type=F32)
        st["pv"] = pv if st["pv"] is None else st["pv"] + pv

    def bc_end(hh, st):
        m_scr[hh] = st["m_new"]
        acc_scr[hh] = st["alpha"] * acc_scr[hh] + st["pv"]

    def bcd_begin(ij, slot, hh):
        st = bc_begin(ij, slot, hh)
        st["pv"] = [None] * nblk
        return st

    def bcd_piece(ij, slot, hh, r, st):
        i, j = ij
        d0, r0 = hh * V_ROWS, r * blk
        p = jnp.exp2(s_buf[slot][hh, r0:r0 + blk, r0:] - st["m_new"][:, r0:])
        pv = jnp.dot(vt_ref[j, d0:d0 + V_ROWS, r0:r0 + blk], p.astype(BF16),
                     preferred_element_type=F32)
        for qb in range(r, nblk):
            part = pv[:, (qb - r) * blk:(qb - r + 1) * blk]
            st["pv"][qb] = part if st["pv"][qb] is None else st["pv"][qb] + part

    def bcd_end(ij, hh, st):
        i, j = ij
        for qb in range(nblk):
            cols = slice(qb * blk, (qb + 1) * blk)
            acc = st["alpha"][:, cols] * acc_scr[hh, :, cols] + st["pv"][qb]
            out = acc[:HEAD_DIM] / acc[HEAD_DIM:HEAD_DIM + 1]
            o_ref[i * t + qb * blk:i * t + (qb + 1) * blk,
                  hh * HEAD_DIM:(hh + 1) * HEAD_DIM] = out.T.astype(BF16)

    def run(slot_a, ija, ijb, a_is_diag, b_is_diag):
        slot_b = 1 - slot_a
        begin, piece = (bcd_begin, bcd_piece) if b_is_diag else (bc_begin, bc_piece)
        cmax = a_begin()
        sts = [begin(ijb, slot_b, hh) for hh in range(ATT_HB)]
        for r in range(nblk):
            for hh in range(ATT_HB):
                piece(ijb, slot_b, hh, r, sts[hh])
                if ija is not None:
                    a_piece(ija, slot_a, a_is_diag, hh, r, cmax)
        for hh in range(ATT_HB):
            if ija is not None:
                a_end(slot_a, hh, cmax)
            if b_is_diag:
                bcd_end(ijb, hh, sts[hh])
            else:
                bc_end(hh, sts[hh])

    m_scr[...] = jnp.full(m_scr.shape, -jnp.inf, F32)
    acc_scr[...] = jnp.zeros(acc_scr.shape, F32)

    def iteration(m):
        if m == 0:
            cmax0 = a_begin()
            for hh in range(ATT_HB):
                for r in range(nblk):
                    a_piece(steps[0], 0, True, hh, r, cmax0)
                a_end(0, hh, cmax0)
        elif m < nsteps:
            sa, sb = steps[m], steps[m - 1]
            run(m % 2, sa, sb, sa[0] == sa[1], sb[0] == sb[1])
        else:
            run(m % 2, None, steps[-1], False, True)

    def block(first):
        def emit():
            for m in range(first, min(first + ATT_GROUP, nsteps + 1)):
                iteration(m)
        return emit

    firsts = list(range(0, nsteps + 1, ATT_GROUP))

    def trip(p, carry):
        for q, first in enumerate(firsts):
            pl.when(p == q)(block(first))
        return carry

    lax.fori_loop(0, len(firsts), trip, 0)


def _attn(qt, kp, vt, batch, seq):
    t = kp.shape[0]
    nq = seq // ATT_T
    nh = 2 * N_HEADS // ATT_HB
    row = pltpu.VMEM((ATT_HB, 1, ATT_T), F32)
    tile = pltpu.VMEM((ATT_HB, ATT_T, ATT_T), F32)
    return pl.pallas_call(
        _attn_kernel,
        grid=(batch, nh),
        in_specs=[
            pl.BlockSpec((nq, ATT_HB * FEAT, ATT_T), lambda b, h: (b, h, 0)),
            pl.BlockSpec((seq, ATT_HB * FEAT), lambda b, h: (b, h)),
            pl.BlockSpec((nq, ATT_HB * V_ROWS, ATT_T), lambda b, h: (b, h, 0)),
        ],
        out_specs=pl.BlockSpec((seq, ATT_HB * HEAD_DIM), lambda b, h: (b, h)),
        out_shape=jax.ShapeDtypeStruct((t, D_MIX), BF16),
        scratch_shapes=[
            tile, tile, row, row, row,
            pltpu.VMEM((ATT_HB, V_ROWS, ATT_T), F32),
        ],
        compiler_params=pltpu.CompilerParams(
            dimension_semantics=("arbitrary", "arbitrary"),
            vmem_limit_bytes=VMEM_LIMIT),
        name="attn",
    )(qt, kp, vt)


def _silu(g):
    return g / (1.0 + jnp.exp(-g))


def _out_kernel(o_ref, gm_ref, gf_ref, x_ref, w_ref, g_ref, y_ref):
    o = o_ref[...].astype(F32)
    om = (o[:, :BRANCH_WIDTH] * _silu(gm_ref[...].astype(F32))).astype(BF16)
    of = (o[:, BRANCH_WIDTH:] * _silu(gf_ref[...].astype(F32))).astype(BF16)
    y = (jnp.dot(om, w_ref[:BRANCH_WIDTH, :], preferred_element_type=F32)
         + jnp.dot(of, w_ref[BRANCH_WIDTH:, :], preferred_element_type=F32))
    y_ref[...] = x_ref[...] + _rms(y, g_ref[...])


def _out(o, rest, x2, w_out, g_post):
    t = o.shape[0]
    return pl.pallas_call(
        _out_kernel,
        grid=(t // OUT_TM,),
        in_specs=[
            pl.BlockSpec((OUT_TM, D_MIX), lambda i: (i, 0)),
            pl.BlockSpec((OUT_TM, BRANCH_WIDTH), lambda i: (i, 0)),
            pl.BlockSpec((OUT_TM, BRANCH_WIDTH), lambda i: (i, 4)),
            pl.BlockSpec((OUT_TM, D_MODEL), lambda i: (i, 0)),
            pl.BlockSpec((D_MIX, D_MODEL), lambda i: (0, 0)),
            pl.BlockSpec((1, D_MODEL), lambda i: (0, 0)),
        ],
        out_specs=pl.BlockSpec((OUT_TM, D_MODEL), lambda i: (i, 0)),
        out_shape=jax.ShapeDtypeStruct((t, D_MODEL), F32),
        compiler_params=pltpu.CompilerParams(
            dimension_semantics=("arbitrary",),
            vmem_limit_bytes=VMEM_LIMIT),
        name="out",
    )(o, rest, rest, x2, w_out, g_post)


def _layer(x2, pos, inv_f, g_pre, w_in, g_q, w_uq, g_kv, w_ukv, b_forget,
           w_out, g_post, batch, seq):
    w_all = _pack_w_in(jnp.swapaxes(w_in, 0, 1))
    w_q = w_uq.T.reshape(N_HEADS, QK_DIM, Q_RANK)
    w_uq_t = jnp.concatenate([
        w_q[:, :NOPE_DIM].reshape(N_HEADS * NOPE_DIM, Q_RANK),
        w_q[:, NOPE_DIM:].reshape(N_HEADS * ROPE_DIM, Q_RANK)], axis=0).astype(BF16)
    w_kv = w_ukv.reshape(KV_RANK, N_HEADS, NOPE_DIM + HEAD_DIM)
    w_k = w_kv[:, :, :NOPE_DIM].reshape(KV_RANK, BRANCH_WIDTH).astype(BF16)
    w_v_t = w_kv[:, :, NOPE_DIM:].reshape(KV_RANK, BRANCH_WIDTH).T.astype(BF16)
    w_out_b = w_out.astype(BF16)
    b_f = jnp.pad(b_forget, (0, LANES - N_HEADS))[None, :]

    lat, small, rest = _proj(x2, g_pre[None, :], w_all)
    qt, kp, vt = _prep(lat, small, rest, pos, g_q[None, :], g_kv[None, :],
                       w_uq_t, w_k, w_v_t, b_f, inv_f, batch, seq)
    o = _attn(qt, kp, vt, batch, seq)
    return _out(o, rest, x2, w_out_b, g_post[None, :])


def kernel(x, positions, g_pre, w_in, g_q_latent, w_uq, g_kv_latent, w_ukv,
           b_forget, w_out, g_post):
    batch, seq, d = x.shape
    depth = g_pre.shape[0]
    x2 = x.reshape(batch * seq, d)
    pos = positions.reshape(batch * seq // PREP_TM, 1, PREP_TM)
    inv_f = (ROPE_THETA ** (-jnp.arange(0, ROPE_DIM, 2, dtype=F32) / ROPE_DIM))[:, None]
    for l in range(depth):
        x2 = _layer(x2, pos, inv_f, g_pre[l], w_in[l], g_q_latent[l], w_uq[l],
                    g_kv_latent[l], w_ukv[l], b_forget[l], w_out[l], g_post[l],
                    batch, seq)
    return x2.reshape(batch, seq, d)
```

```python
import math

import jax
import jax.numpy as jnp
from jax import lax
from jax.experimental import pallas as pl
from jax.experimental.pallas import tpu as pltpu

F32 = jnp.float32
BF16 = jnp.bfloat16

D_MODEL = 2048
N_HEADS = 8
HEAD_DIM = 128
NOPE_DIM = 128
ROPE_DIM = 64
ROPE_HALF = ROPE_DIM // 2
QK_DIM = NOPE_DIM + ROPE_DIM
Q_RANK = 768
KV_RANK = 512
LAT_DIM = Q_RANK + KV_RANK
BRANCH_WIDTH = N_HEADS * HEAD_DIM
D_MIX = 2 * BRANCH_WIDTH
ROPE_THETA = 10000.0
NORM_EPS = 1e-6
LOG2E = math.log2(math.e)

LANES = 128
FEAT = 256
SMALL_W = 2 * LANES
REST_W = 5 * BRANCH_WIDTH

MLA_QSCALE = QK_DIM ** -0.5 * LOG2E
FOX_QSCALE = HEAD_DIM ** -0.5 * LOG2E

PACK_TN = 256
PROJ_TM = 1024
PREP_TM = 512
ATT_T = 1024
ATT_BLK = 256
ATT_HB = 2
ATT_GROUP = 6
V_ROWS = HEAD_DIM + 16
OUT_TM = 512

V7X_VMEM_BYTES = 64 * 1024 * 1024
VMEM_LIMIT = V7X_VMEM_BYTES * 7 // 8


def _rms(x, g):
    return x * lax.rsqrt(jnp.mean(x * x, axis=-1, keepdims=True) + NORM_EPS) * g


def _pack_kernel(w_ref, o_ref):
    o_kr = LAT_DIM
    o_gm = o_kr + ROPE_DIM
    o_fl = o_gm + 4 * BRANCH_WIDTH
    o_gf = o_fl + N_HEADS
    cols = w_ref.shape[1]
    zeros = lambda n: jnp.zeros((n, cols), F32)
    o_ref[...] = jnp.concatenate([
        w_ref[o_gm:o_fl, :], w_ref[o_gf:o_gf + BRANCH_WIDTH, :],
        w_ref[:LAT_DIM, :],
        w_ref[o_kr:o_gm, :], zeros(LANES - ROPE_DIM),
        w_ref[o_fl:o_gf, :], zeros(LANES - N_HEADS),
    ], axis=0).astype(BF16)


def _pack_w_in(w_in_t):
    d_in, d = w_in_t.shape
    height = REST_W + LAT_DIM + SMALL_W
    return pl.pallas_call(
        _pack_kernel,
        grid=(d // PACK_TN,),
        in_specs=[pl.BlockSpec((d_in, PACK_TN), lambda i: (0, i))],
        out_specs=pl.BlockSpec((height, PACK_TN), lambda i: (0, i)),
        out_shape=jax.ShapeDtypeStruct((height, d), BF16),
        compiler_params=pltpu.CompilerParams(
            dimension_semantics=("arbitrary",),
            vmem_limit_bytes=VMEM_LIMIT),
        name="pack",
    )(w_in_t)


def _dot_nt(a, b):
    return lax.dot_general(a, b, (((1,), (1,)), ((), ())), preferred_element_type=F32)


def _proj_kernel(x_ref, g_ref, wlat_ref, wsm_ref, wrest_ref,
                 lat_ref, sm_ref, rest_ref, h_scr):
    j = pl.program_id(1)

    @pl.when(j == 0)
    def _():
        hb = _rms(x_ref[...], g_ref[...]).astype(BF16)
        h_scr[...] = hb
        lat_ref[...] = _dot_nt(hb, wlat_ref[...]).astype(BF16)
        sm_ref[...] = _dot_nt(hb, wsm_ref[...])

    acc = _dot_nt(h_scr[...], wrest_ref[...])
    scale = jnp.where(j == 1, FOX_QSCALE, 1.0).astype(F32)
    rest_ref[...] = (acc * scale).astype(BF16)


def _proj(x2, g_pre, w_all):
    t = x2.shape[0]
    n_rest = REST_W // BRANCH_WIDTH
    assert REST_W % LAT_DIM == 0 and (REST_W + LAT_DIM) % SMALL_W == 0
    once = pl.Buffered(1)
    return pl.pallas_call(
        _proj_kernel,
        grid=(t // PROJ_TM, n_rest),
        in_specs=[
            pl.BlockSpec((PROJ_TM, D_MODEL), lambda i, j: (i, 0)),
            pl.BlockSpec((1, D_MODEL), lambda i, j: (0, 0)),
            pl.BlockSpec((LAT_DIM, D_MODEL), lambda i, j: (REST_W // LAT_DIM, 0),
                         pipeline_mode=once),
            pl.BlockSpec((SMALL_W, D_MODEL), lambda i, j: ((REST_W + LAT_DIM) // SMALL_W, 0),
                         pipeline_mode=once),
            pl.BlockSpec((BRANCH_WIDTH, D_MODEL), lambda i, j: (j, 0)),
        ],
        out_specs=[
            pl.BlockSpec((PROJ_TM, LAT_DIM), lambda i, j: (i, 0)),
            pl.BlockSpec((PROJ_TM, SMALL_W), lambda i, j: (i, 0)),
            pl.BlockSpec((PROJ_TM, BRANCH_WIDTH), lambda i, j: (i, j)),
        ],
        out_shape=[
            jax.ShapeDtypeStruct((t, LAT_DIM), BF16),
            jax.ShapeDtypeStruct((t, SMALL_W), F32),
            jax.ShapeDtypeStruct((t, REST_W), BF16),
        ],
        scratch_shapes=[pltpu.VMEM((PROJ_TM, D_MODEL), BF16)],
        compiler_params=pltpu.CompilerParams(
            dimension_semantics=("arbitrary", "arbitrary"),
            vmem_limit_bytes=VMEM_LIMIT),
        name="proj",
    )(x2, g_pre, w_all, w_all, w_all)


def _split3(x):
    hi = x.astype(BF16).astype(F32)
    r = x - hi
    mid = r.astype(BF16).astype(F32)
    lo = (r - mid).astype(BF16).astype(F32)
    return hi, mid, lo


def _prep_kernel(lat_ref, sm_ref, fq_ref, fk_ref, fv_ref, pos_ref,
                 gq_ref, gkv_ref, wuqt_ref, wk_ref, wvt_ref, bf_ref, invf_ref,
                 qt_ref, kp_ref, vt_ref, carry_scr):
    tm = lat_ref.shape[0]

    def put_t(ref, r0, x):
        ref[0, r0:r0 + x.shape[1], :] = x.T

    @pl.when(pl.program_id(1) == 0)
    def _():
        carry_scr[...] = jnp.zeros_like(carry_scr)

    lat = lat_ref[...].astype(F32)
    qn = _rms(lat[:, :Q_RANK], gq_ref[...]).astype(BF16)
    kvn = _rms(lat[:, Q_RANK:], gkv_ref[...]).astype(BF16)

    ang = invf_ref[...] * pos_ref[0].astype(F32)
    cos_t = jnp.cos(ang)
    sin_t = jnp.sin(ang)

    def rope_t(x):
        x1, x2 = x[:ROPE_HALF], x[ROPE_HALF:ROPE_DIM]
        return x1 * cos_t - x2 * sin_t, x2 * cos_t + x1 * sin_t

    kr1, kr2 = rope_t(sm_ref[:, :LANES].T)
    k_rope = jnp.concatenate(
        [kr1, kr2, jnp.zeros((LANES - ROPE_DIM, tm), F32)], axis=0).T.astype(BF16)

    lane = lax.broadcasted_iota(jnp.int32, (tm, LANES), 1)
    fl = sm_ref[:, LANES:] + bf_ref[...]
    logf = jnp.minimum(fl, 0.0) - jnp.log1p(jnp.exp(-jnp.abs(fl)))
    row = lax.broadcasted_iota(jnp.int32, (tm, tm), 0)
    col = lax.broadcasted_iota(jnp.int32, (tm, tm), 1)
    tri = jnp.where(row >= col, 1.0, 0.0).astype(BF16)
    c = carry_scr[...]
    for piece in _split3(logf):
        c = c + jnp.dot(tri, piece.astype(BF16), preferred_element_type=F32)
    carry_scr[...] = c[tm - 1:tm, :]
    c2 = c * LOG2E
    c2_t = c2.T

    aug = 16
    srow = lax.broadcasted_iota(jnp.int32, (aug, tm), 0)
    q_zero = jnp.zeros((FEAT - HEAD_DIM - aug, tm), BF16)
    q_pad = jnp.zeros((FEAT - QK_DIM, tm), BF16)

    def fox_head(h):
        c0 = (N_HEADS + h) * FEAT
        s0 = h * HEAD_DIM
        hi, mid, lo = _split3(jnp.broadcast_to(c2[:, h:h + 1], (tm, LANES)))
        ck = jnp.where(lane < 3, 1.0, jnp.where(lane == 3, -hi, jnp.where(
            lane == 4, -mid, jnp.where(lane == 5, -lo, 0.0))))
        kp_ref[:, c0:c0 + LANES] = fk_ref[:, s0:s0 + HEAD_DIM]
        kp_ref[:, c0 + LANES:c0 + FEAT] = ck.astype(BF16)
        hi_t, mid_t, lo_t = _split3(c2_t[h:h + 1, :])
        cq_t = jnp.where(srow == 0, hi_t, jnp.where(srow == 1, mid_t, jnp.where(
            srow == 2, lo_t, jnp.where(srow < 6, 1.0, 0.0))))
        put_t(qt_ref, c0, fq_ref[:, s0:s0 + HEAD_DIM])
        qt_ref[0, c0 + HEAD_DIM:c0 + HEAD_DIM + aug, :] = cq_t.astype(BF16)
        qt_ref[0, c0 + HEAD_DIM + aug:c0 + FEAT, :] = q_zero
        put_t(vt_ref, (N_HEADS + h) * V_ROWS, fv_ref[:, s0:s0 + HEAD_DIM])

    def mla_q_head(h, q_nope, base, q_rope):
        c0, l0 = h * FEAT, (h - base) * NOPE_DIM
        qr1, qr2 = rope_t(q_rope[h * ROPE_DIM:(h + 1) * ROPE_DIM])
        qt_ref[0, c0:c0 + NOPE_DIM, :] = q_nope[l0:l0 + NOPE_DIM].astype(BF16)
        qt_ref[0, c0 + NOPE_DIM:c0 + NOPE_DIM + ROPE_HALF, :] = qr1.astype(BF16)
        qt_ref[0, c0 + NOPE_DIM + ROPE_HALF:c0 + QK_DIM, :] = qr2.astype(BF16)
        qt_ref[0, c0 + QK_DIM:c0 + FEAT, :] = q_pad

    half = N_HEADS // 2
    n_nope = N_HEADS * NOPE_DIM
    q_a = _dot_nt(wuqt_ref[:n_nope // 2, :], qn) * MLA_QSCALE
    for h in range(half):
        fox_head(h)
    q_b = _dot_nt(wuqt_ref[n_nope // 2:n_nope, :], qn) * MLA_QSCALE
    for h in range(half, N_HEADS):
        fox_head(h)
    q_r = _dot_nt(wuqt_ref[n_nope:, :], qn) * MLA_QSCALE
    k_nope = jnp.dot(kvn, wk_ref[...], preferred_element_type=F32)
    for h in range(half):
        mla_q_head(h, q_a, 0, q_r)
    v_t = _dot_nt(wvt_ref[...], kvn)
    for h in range(half, N_HEADS):
        mla_q_head(h, q_b, half, q_r)
    for h in range(N_HEADS):
        c0, d0 = h * FEAT, h * HEAD_DIM
        kp_ref[:, c0:c0 + LANES] = k_nope[:, d0:d0 + HEAD_DIM].astype(BF16)
        kp_ref[:, c0 + LANES:c0 + FEAT] = k_rope
        vt_ref[0, h * V_ROWS:h * V_ROWS + HEAD_DIM, :] = v_t[d0:d0 + HEAD_DIM].astype(BF16)
    pad = V_ROWS - HEAD_DIM
    ones_blk = jnp.where(lax.broadcasted_iota(jnp.int32, (pad, tm), 0) == 0,
                         1.0, 0.0).astype(BF16)
    for h in range(2 * N_HEADS):
        vt_ref[0, h * V_ROWS + HEAD_DIM:(h + 1) * V_ROWS, :] = ones_blk


def _prep(lat, small, rest, pos, g_q, g_kv, w_uq_t, w_k, w_v_t, b_f, inv_f, batch, seq):
    t = lat.shape[0]
    ns = seq // PREP_TM
    per = ATT_T // PREP_TM
    nt = t // ATT_T
    row = lambda b, i: (b * ns + i, 0)
    const = lambda b, i: (0, 0)
    tile_t = lambda b, i: ((b * ns + i) // per, 0, (b * ns + i) % per)
    return pl.pallas_call(
        _prep_kernel,
        grid=(batch, ns),
        in_specs=[
            pl.BlockSpec((PREP_TM, LAT_DIM), row),
            pl.BlockSpec((PREP_TM, SMALL_W), row),
            pl.BlockSpec((PREP_TM, BRANCH_WIDTH), lambda b, i: (b * ns + i, 1)),
            pl.BlockSpec((PREP_TM, BRANCH_WIDTH), lambda b, i: (b * ns + i, 2)),
            pl.BlockSpec((PREP_TM, BRANCH_WIDTH), lambda b, i: (b * ns + i, 3)),
            pl.BlockSpec((1, 1, PREP_TM), lambda b, i: (b * ns + i, 0, 0)),
            pl.BlockSpec((1, Q_RANK), const),
            pl.BlockSpec((1, KV_RANK), const),
            pl.BlockSpec((N_HEADS * QK_DIM, Q_RANK), const),
            pl.BlockSpec((KV_RANK, BRANCH_WIDTH), const),
            pl.BlockSpec((BRANCH_WIDTH, KV_RANK), const),
            pl.BlockSpec((1, LANES), const),
            pl.BlockSpec((ROPE_HALF, 1), const),
        ],
        out_specs=[
            pl.BlockSpec((1, 2 * N_HEADS * FEAT, PREP_TM), tile_t),
            pl.BlockSpec((PREP_TM, 2 * N_HEADS * FEAT), row),
            pl.BlockSpec((1, 2 * N_HEADS * V_ROWS, PREP_TM), tile_t),
        ],
        out_shape=[
            jax.ShapeDtypeStruct((nt, 2 * N_HEADS * FEAT, ATT_T), BF16),
            jax.ShapeDtypeStruct((t, 2 * N_HEADS * FEAT), BF16),
            jax.ShapeDtypeStruct((nt, 2 * N_HEADS * V_ROWS, ATT_T), BF16),
        ],
        scratch_shapes=[pltpu.VMEM((1, LANES), F32)],
        compiler_params=pltpu.CompilerParams(
            dimension_semantics=("arbitrary", "arbitrary"),
            vmem_limit_bytes=VMEM_LIMIT),
        name="prep",
    )(lat, small, rest, rest, rest, pos, g_q, g_kv, w_uq_t, w_k, w_v_t, b_f, inv_f)


def _attn_kernel(qt_ref, k_ref, vt_ref, o_ref, s0, s1, cm0, cm1, m_scr, acc_scr):
    t = ATT_T
    nq = qt_ref.shape[0]
    steps = [(i, j) for i in range(nq) for j in range(i + 1)]
    nsteps = len(steps)
    s_buf, cm_buf = (s0, s1), (cm0, cm1)

    nblk = t // ATT_BLK
    blk = ATT_BLK

    def a_begin():
        return [[None] * nblk for _ in range(ATT_HB)]

    def a_piece(ij, slot, diagonal, hh, r, cmax):
        i, j = ij
        f0, r0 = hh * FEAT, r * blk
        k = k_ref[j * t + r0:j * t + r0 + blk, f0:f0 + FEAT]
        c_lo = r0 if diagonal else 0
        s = jnp.dot(k, qt_ref[i, f0:f0 + FEAT, c_lo:], preferred_element_type=F32)
        if diagonal:
            key = lax.broadcasted_iota(jnp.int32, (blk, blk), 0)
            qry = lax.broadcasted_iota(jnp.int32, (blk, blk), 1)
            parts = [jnp.where(key <= qry, s[:, :blk], -jnp.inf)]
            if s.shape[1] > blk:
                parts.append(s[:, blk:])
            s = jnp.concatenate(parts, axis=1)
        s_buf[slot][hh, r0:r0 + blk, c_lo:] = s
        for qb in range(c_lo // blk, nblk):
            part = jnp.max(s[:, qb * blk - c_lo:(qb + 1) * blk - c_lo], axis=0, keepdims=True)
            cmax[hh][qb] = part if cmax[hh][qb] is None else jnp.maximum(cmax[hh][qb], part)

    def a_end(slot, hh, cmax):
        for qb in range(nblk):
            cm_buf[slot][hh, :, qb * blk:(qb + 1) * blk] = cmax[hh][qb]

    def bc_begin(ij, slot, hh):
        i, j = ij
        if j == 0:
            return dict(m_new=cm_buf[slot][hh], alpha=None, pv=None)
        m_old = m_scr[hh]
        m_new = jnp.maximum(m_old, cm_buf[slot][hh])
        return dict(m_new=m_new, alpha=jnp.exp2(m_old - m_new), pv=None)

    def rescaled(st, hh, cols, pv):
        if st["alpha"] is None:
            return pv
        return st["alpha"][:, cols] * acc_scr[hh, :, cols] + pv

    def bc_piece(ij, slot, hh, r, st):
        i, j = ij
        d0, r0 = hh * V_ROWS, r * blk
        p = jnp.exp2(s_buf[slot][hh, r0:r0 + blk, :] - st["m_new"])
        pv = jnp.dot(vt_ref[j, d0:d0 + V_ROWS, r0:r0 + blk], p.astype(BF16),
                     preferred_element_type=F32)
        st["pv"] = pv if st["pv"] is None else st["pv"] + pv

    def bc_end(hh, st):
        m_scr[hh] = st["m_new"]
        acc_scr[hh] = rescaled(st, hh, slice(None), st["pv"])

    def bcd_begin(ij, slot, hh):
        st = bc_begin(ij, slot, hh)
        st["pv"] = [None] * nblk
        return st

    def bcd_piece(ij, slot, hh, r, st):
        i, j = ij
        d0, r0 = hh * V_ROWS, r * blk
        p = jnp.exp2(s_buf[slot][hh, r0:r0 + blk, r0:] - st["m_new"][:, r0:])
        pv = jnp.dot(vt_ref[j, d0:d0 + V_ROWS, r0:r0 + blk], p.astype(BF16),
                     preferred_element_type=F32)
        for qb in range(r, nblk):
            part = pv[:, (qb - r) * blk:(qb - r + 1) * blk]
            st["pv"][qb] = part if st["pv"][qb] is None else st["pv"][qb] + part

    def bcd_end(ij, hh, st):
        i, j = ij
        for qb in range(nblk):
            cols = slice(qb * blk, (qb + 1) * blk)
            acc = rescaled(st, hh, cols, st["pv"][qb])
            out = acc[:HEAD_DIM] / acc[HEAD_DIM:HEAD_DIM + 1]
            o_ref[i * t + qb * blk:i * t + (qb + 1) * blk,
                  hh * HEAD_DIM:(hh + 1) * HEAD_DIM] = out.T.astype(BF16)

    def run(slot_a, ija, ijb, a_is_diag, b_is_diag):
        slot_b = 1 - slot_a
        begin, piece = (bcd_begin, bcd_piece) if b_is_diag else (bc_begin, bc_piece)
        cmax = a_begin()
        sts = [begin(ijb, slot_b, hh) for hh in range(ATT_HB)]
        for r in range(nblk):
            for hh in range(ATT_HB):
                piece(ijb, slot_b, hh, r, sts[hh])
                if ija is not None:
                    a_piece(ija, slot_a, a_is_diag, hh, r, cmax)
        for hh in range(ATT_HB):
            if ija is not None:
                a_end(slot_a, hh, cmax)
            if b_is_diag:
                bcd_end(ijb, hh, sts[hh])
            else:
                bc_end(hh, sts[hh])

    def iteration(m):
        if m == 0:
            cmax0 = a_begin()
            for hh in range(ATT_HB):
                for r in range(nblk):
                    a_piece(steps[0], 0, True, hh, r, cmax0)
                a_end(0, hh, cmax0)
        elif m < nsteps:
            sa, sb = steps[m], steps[m - 1]
            run(m % 2, sa, sb, sa[0] == sa[1], sb[0] == sb[1])
        else:
            run(m % 2, None, steps[-1], False, True)

    def block(first):
        def emit():
            for m in range(first, min(first + ATT_GROUP, nsteps + 1)):
                iteration(m)
        return emit

    firsts = list(range(0, nsteps + 1, ATT_GROUP))

    def trip(p, carry):
        for q, first in enumerate(firsts):
            pl.when(p == q)(block(first))
        return carry

    lax.fori_loop(0, len(firsts), trip, 0)


def _attn(qt, kp, vt, batch, seq):
    t = kp.shape[0]
    nq = seq // ATT_T
    nh = 2 * N_HEADS // ATT_HB
    row = pltpu.VMEM((ATT_HB, 1, ATT_T), F32)
    tile = pltpu.VMEM((ATT_HB, ATT_T, ATT_T), F32)
    return pl.pallas_call(
        _attn_kernel,
        grid=(batch, nh),
        in_specs=[
            pl.BlockSpec((nq, ATT_HB * FEAT, ATT_T), lambda b, h: (b, h, 0)),
            pl.BlockSpec((seq, ATT_HB * FEAT), lambda b, h: (b, h)),
            pl.BlockSpec((nq, ATT_HB * V_ROWS, ATT_T), lambda b, h: (b, h, 0)),
        ],
        out_specs=pl.BlockSpec((seq, ATT_HB * HEAD_DIM), lambda b, h: (b, h)),
        out_shape=jax.ShapeDtypeStruct((t, D_MIX), BF16),
        scratch_shapes=[
            tile, tile, row, row, row,
            pltpu.VMEM((ATT_HB, V_ROWS, ATT_T), F32),
        ],
        compiler_params=pltpu.CompilerParams(
            dimension_semantics=("arbitrary", "arbitrary"),
            vmem_limit_bytes=VMEM_LIMIT),
        name="attn",
    )(qt, kp, vt)


def _silu(g):
    return g / (1.0 + jnp.exp(-g))


def _out_kernel(o_ref, gm_ref, gf_ref, x_ref, w_ref, g_ref, y_ref):
    o = o_ref[...].astype(F32)
    om = (o[:, :BRANCH_WIDTH] * _silu(gm_ref[...].astype(F32))).astype(BF16)
    of = (o[:, BRANCH_WIDTH:] * _silu(gf_ref[...].astype(F32))).astype(BF16)
    y = (jnp.dot(om, w_ref[:BRANCH_WIDTH, :], preferred_element_type=F32)
         + jnp.dot(of, w_ref[BRANCH_WIDTH:, :], preferred_element_type=F32))
    y_ref[...] = x_ref[...] + _rms(y, g_ref[...])


def _out(o, rest, x2, w_out, g_post):
    t = o.shape[0]
    return pl.pallas_call(
        _out_kernel,
        grid=(t // OUT_TM,),
        in_specs=[
            pl.BlockSpec((OUT_TM, D_MIX), lambda i: (i, 0)),
            pl.BlockSpec((OUT_TM, BRANCH_WIDTH), lambda i: (i, 0)),
            pl.BlockSpec((OUT_TM, BRANCH_WIDTH), lambda i: (i, 4)),
            pl.BlockSpec((OUT_TM, D_MODEL), lambda i: (i, 0)),
            pl.BlockSpec((D_MIX, D_MODEL), lambda i: (0, 0)),
            pl.BlockSpec((1, D_MODEL), lambda i: (0, 0)),
        ],
        out_specs=pl.BlockSpec((OUT_TM, D_MODEL), lambda i: (i, 0)),
        out_shape=jax.ShapeDtypeStruct((t, D_MODEL), F32),
        compiler_params=pltpu.CompilerParams(
            dimension_semantics=("arbitrary",),
            vmem_limit_bytes=VMEM_LIMIT),
        name="out",
    )(o, rest, rest, x2, w_out, g_post)


def _layer(x2, pos, inv_f, g_pre, w_in, g_q, w_uq, g_kv, w_ukv, b_forget,
           w_out, g_post, batch, seq):
    w_all = _pack_w_in(jnp.swapaxes(w_in, 0, 1))
    w_q = w_uq.T.reshape(N_HEADS, QK_DIM, Q_RANK)
    w_uq_t = jnp.concatenate([
        w_q[:, :NOPE_DIM].reshape(N_HEADS * NOPE_DIM, Q_RANK),
        w_q[:, NOPE_DIM:].reshape(N_HEADS * ROPE_DIM, Q_RANK)], axis=0).astype(BF16)
    w_kv = w_ukv.reshape(KV_RANK, N_HEADS, NOPE_DIM + HEAD_DIM)
    w_k = w_kv[:, :, :NOPE_DIM].reshape(KV_RANK, BRANCH_WIDTH).astype(BF16)
    w_v_t = w_kv[:, :, NOPE_DIM:].reshape(KV_RANK, BRANCH_WIDTH).T.astype(BF16)
    w_out_b = w_out.astype(BF16)
    b_f = jnp.pad(b_forget, (0, LANES - N_HEADS))[None, :]

    lat, small, rest = _proj(x2, g_pre[None, :], w_all)
    qt, kp, vt = _prep(lat, small, rest, pos, g_q[None, :], g_kv[None, :],
                       w_uq_t, w_k, w_v_t, b_f, inv_f, batch, seq)
    o = _attn(qt, kp, vt, batch, seq)
    return _out(o, rest, x2, w_out_b, g_post[None, :])


def kernel(x, positions, g_pre, w_in, g_q_latent, w_uq, g_kv_latent, w_ukv,
           b_forget, w_out, g_post):
    batch, seq, d = x.shape
    depth = g_pre.shape[0]
    x2 = x.reshape(batch * seq, d)
    pos = positions.reshape(batch * seq // PREP_TM, 1, PREP_TM)
    inv_f = (ROPE_THETA ** (-jnp.arange(0, ROPE_DIM, 2, dtype=F32) / ROPE_DIM))[:, None]
    for l in range(depth):
        x2 = _layer(x2, pos, inv_f, g_pre[l], w_in[l], g_q_latent[l], w_uq[l],
                    g_kv_latent[l], w_ukv[l], b_forget[l], w_out[l], g_post[l],
                    batch, seq)
    return x2.reshape(batch, seq, d)
```

```python
import math

import jax
import jax.numpy as jnp
from jax import lax
from jax.experimental import pallas as pl
from jax.experimental.pallas import tpu as pltpu

F32 = jnp.float32
BF16 = jnp.bfloat16

D_MODEL = 2048
N_HEADS = 8
HEAD_DIM = 128
NOPE_DIM = 128
ROPE_DIM = 64
ROPE_HALF = ROPE_DIM // 2
QK_DIM = NOPE_DIM + ROPE_DIM
Q_RANK = 768
KV_RANK = 512
LAT_DIM = Q_RANK + KV_RANK
BRANCH_WIDTH = N_HEADS * HEAD_DIM
D_MIX = 2 * BRANCH_WIDTH
ROPE_THETA = 10000.0
NORM_EPS = 1e-6
LOG2E = math.log2(math.e)

LANES = 128
FEAT = 256
SMALL_W = 2 * LANES
REST_W = 5 * BRANCH_WIDTH

MLA_QSCALE = QK_DIM ** -0.5 * LOG2E
FOX_QSCALE = HEAD_DIM ** -0.5 * LOG2E

PACK_TN = 256
PROJ_TM = 1024
PREP_TM = 512
ATT_T = 1024
ATT_BLK = 256
ATT_HB = 2
ATT_GROUP = 8
V_ROWS = HEAD_DIM + 16
OUT_TM = 512

V7X_VMEM_BYTES = 64 * 1024 * 1024
VMEM_LIMIT = V7X_VMEM_BYTES * 7 // 8


def _rms(x, g):
    return x * lax.rsqrt(jnp.mean(x * x, axis=-1, keepdims=True) + NORM_EPS) * g


def _pack_kernel(w_ref, o_ref):
    o_kr = LAT_DIM
    o_gm = o_kr + ROPE_DIM
    o_fl = o_gm + 4 * BRANCH_WIDTH
    o_gf = o_fl + N_HEADS
    cols = w_ref.shape[1]
    zeros = lambda n: jnp.zeros((n, cols), F32)
    o_ref[...] = jnp.concatenate([
        w_ref[o_gm:o_fl, :], w_ref[o_gf:o_gf + BRANCH_WIDTH, :],
        w_ref[:LAT_DIM, :],
        w_ref[o_kr:o_gm, :], zeros(LANES - ROPE_DIM),
        w_ref[o_fl:o_gf, :], zeros(LANES - N_HEADS),
    ], axis=0).astype(BF16)


def _pack_w_in(w_in_t):
    d_in, d = w_in_t.shape
    height = REST_W + LAT_DIM + SMALL_W
    return pl.pallas_call(
        _pack_kernel,
        grid=(d // PACK_TN,),
        in_specs=[pl.BlockSpec((d_in, PACK_TN), lambda i: (0, i))],
        out_specs=pl.BlockSpec((height, PACK_TN), lambda i: (0, i)),
        out_shape=jax.ShapeDtypeStruct((height, d), BF16),
        compiler_params=pltpu.CompilerParams(
            dimension_semantics=("arbitrary",),
            vmem_limit_bytes=VMEM_LIMIT),
        name="pack",
    )(w_in_t)


def _dot_nt(a, b):
    return lax.dot_general(a, b, (((1,), (1,)), ((), ())), preferred_element_type=F32)


def _proj_kernel(x_ref, g_ref, wlat_ref, wsm_ref, wrest_ref,
                 lat_ref, sm_ref, rest_ref, h_scr):
    j = pl.program_id(1)

    @pl.when(j == 0)
    def _():
        hb = _rms(x_ref[...], g_ref[...]).astype(BF16)
        h_scr[...] = hb
        lat_ref[...] = _dot_nt(hb, wlat_ref[...]).astype(BF16)
        sm_ref[...] = _dot_nt(hb, wsm_ref[...])

    acc = _dot_nt(h_scr[...], wrest_ref[...])
    scale = jnp.where(j == 1, FOX_QSCALE, 1.0).astype(F32)
    rest_ref[...] = (acc * scale).astype(BF16)


def _proj(x2, g_pre, w_all):
    t = x2.shape[0]
    n_rest = REST_W // BRANCH_WIDTH
    assert REST_W % LAT_DIM == 0 and (REST_W + LAT_DIM) % SMALL_W == 0
    once = pl.Buffered(1)
    return pl.pallas_call(
        _proj_kernel,
        grid=(t // PROJ_TM, n_rest),
        in_specs=[
            pl.BlockSpec((PROJ_TM, D_MODEL), lambda i, j: (i, 0)),
            pl.BlockSpec((1, D_MODEL), lambda i, j: (0, 0)),
            pl.BlockSpec((LAT_DIM, D_MODEL), lambda i, j: (REST_W // LAT_DIM, 0),
                         pipeline_mode=once),
            pl.BlockSpec((SMALL_W, D_MODEL), lambda i, j: ((REST_W + LAT_DIM) // SMALL_W, 0),
                         pipeline_mode=once),
            pl.BlockSpec((BRANCH_WIDTH, D_MODEL), lambda i, j: (j, 0)),
        ],
        out_specs=[
            pl.BlockSpec((PROJ_TM, LAT_DIM), lambda i, j: (i, 0)),
            pl.BlockSpec((PROJ_TM, SMALL_W), lambda i, j: (i, 0)),
            pl.BlockSpec((PROJ_TM, BRANCH_WIDTH), lambda i, j: (i, j)),
        ],
        out_shape=[
            jax.ShapeDtypeStruct((t, LAT_DIM), BF16),
            jax.ShapeDtypeStruct((t, SMALL_W), F32),
            jax.ShapeDtypeStruct((t, REST_W), BF16),
        ],
        scratch_shapes=[pltpu.VMEM((PROJ_TM, D_MODEL), BF16)],
        compiler_params=pltpu.CompilerParams(
            dimension_semantics=("arbitrary", "arbitrary"),
            vmem_limit_bytes=VMEM_LIMIT),
        name="proj",
    )(x2, g_pre, w_all, w_all, w_all)


def _split3(x):
    hi = x.astype(BF16).astype(F32)
    r = x - hi
    mid = r.astype(BF16).astype(F32)
    lo = (r - mid).astype(BF16).astype(F32)
    return hi, mid, lo


def _prep_kernel(lat_ref, sm_ref, fq_ref, fk_ref, fv_ref, pos_ref,
                 gq_ref, gkv_ref, wuqt_ref, wk_ref, wvt_ref, bf_ref, invf_ref,
                 qt_ref, kp_ref, vt_ref, carry_scr):
    tm = lat_ref.shape[0]

    def put_t(ref, r0, x):
        ref[0, r0:r0 + x.shape[1], :] = x.T

    @pl.when(pl.program_id(1) == 0)
    def _():
        carry_scr[...] = jnp.zeros_like(carry_scr)

    lat = lat_ref[...].astype(F32)
    qn = _rms(lat[:, :Q_RANK], gq_ref[...]).astype(BF16)
    kvn = _rms(lat[:, Q_RANK:], gkv_ref[...]).astype(BF16)

    ang = invf_ref[...] * pos_ref[0].astype(F32)
    cos_t = jnp.cos(ang)
    sin_t = jnp.sin(ang)

    def rope_t(x):
        x1, x2 = x[:ROPE_HALF], x[ROPE_HALF:ROPE_DIM]
        return x1 * cos_t - x2 * sin_t, x2 * cos_t + x1 * sin_t

    kr1, kr2 = rope_t(sm_ref[:, :LANES].T)
    k_rope = jnp.concatenate(
        [kr1, kr2, jnp.zeros((LANES - ROPE_DIM, tm), F32)], axis=0).T.astype(BF16)

    lane = lax.broadcasted_iota(jnp.int32, (tm, LANES), 1)
    fl = sm_ref[:, LANES:] + bf_ref[...]
    logf = jnp.minimum(fl, 0.0) - jnp.log1p(jnp.exp(-jnp.abs(fl)))
    row = lax.broadcasted_iota(jnp.int32, (tm, tm), 0)
    col = lax.broadcasted_iota(jnp.int32, (tm, tm), 1)
    tri = jnp.where(row >= col, 1.0, 0.0).astype(BF16)
    c = carry_scr[...]
    for piece in _split3(logf):
        c = c + jnp.dot(tri, piece.astype(BF16), preferred_element_type=F32)
    carry_scr[...] = c[tm - 1:tm, :]
    c2 = c * LOG2E
    c2_t = c2.T

    aug = 16
    srow = lax.broadcasted_iota(jnp.int32, (aug, tm), 0)
    q_zero = jnp.zeros((FEAT - HEAD_DIM - aug, tm), BF16)
    q_pad = jnp.zeros((FEAT - QK_DIM, tm), BF16)

    def fox_head(h):
        c0 = (N_HEADS + h) * FEAT
        s0 = h * HEAD_DIM
        hi, mid, lo = _split3(jnp.broadcast_to(c2[:, h:h + 1], (tm, LANES)))
        ck = jnp.where(lane < 3, 1.0, jnp.where(lane == 3, -hi, jnp.where(
            lane == 4, -mid, jnp.where(lane == 5, -lo, 0.0))))
        kp_ref[:, c0:c0 + LANES] = fk_ref[:, s0:s0 + HEAD_DIM]
        kp_ref[:, c0 + LANES:c0 + FEAT] = ck.astype(BF16)
        hi_t, mid_t, lo_t = _split3(c2_t[h:h + 1, :])
        cq_t = jnp.where(srow == 0, hi_t, jnp.where(srow == 1, mid_t, jnp.where(
            srow == 2, lo_t, jnp.where(srow < 6, 1.0, 0.0))))
        put_t(qt_ref, c0, fq_ref[:, s0:s0 + HEAD_DIM])
        qt_ref[0, c0 + HEAD_DIM:c0 + HEAD_DIM + aug, :] = cq_t.astype(BF16)
        qt_ref[0, c0 + HEAD_DIM + aug:c0 + FEAT, :] = q_zero
        put_t(vt_ref, (N_HEADS + h) * V_ROWS, fv_ref[:, s0:s0 + HEAD_DIM])

    def mla_q_head(h, q_nope, base, q_rope):
        c0, l0 = h * FEAT, (h - base) * NOPE_DIM
        qr1, qr2 = rope_t(q_rope[h * ROPE_DIM:(h + 1) * ROPE_DIM])
        qt_ref[0, c0:c0 + NOPE_DIM, :] = q_nope[l0:l0 + NOPE_DIM].astype(BF16)
        qt_ref[0, c0 + NOPE_DIM:c0 + NOPE_DIM + ROPE_HALF, :] = qr1.astype(BF16)
        qt_ref[0, c0 + NOPE_DIM + ROPE_HALF:c0 + QK_DIM, :] = qr2.astype(BF16)
        qt_ref[0, c0 + QK_DIM:c0 + FEAT, :] = q_pad

    half = N_HEADS // 2
    n_nope = N_HEADS * NOPE_DIM
    q_a = _dot_nt(wuqt_ref[:n_nope // 2, :], qn) * MLA_QSCALE
    for h in range(half):
        fox_head(h)
    q_b = _dot_nt(wuqt_ref[n_nope // 2:n_nope, :], qn) * MLA_QSCALE
    for h in range(half, N_HEADS):
        fox_head(h)
    q_r = _dot_nt(wuqt_ref[n_nope:, :], qn) * MLA_QSCALE
    k_nope = jnp.dot(kvn, wk_ref[...], preferred_element_type=F32)
    for h in range(half):
        mla_q_head(h, q_a, 0, q_r)
    v_t = _dot_nt(wvt_ref[...], kvn)
    for h in range(half, N_HEADS):
        mla_q_head(h, q_b, half, q_r)
    for h in range(N_HEADS):
        c0, d0 = h * FEAT, h * HEAD_DIM
        kp_ref[:, c0:c0 + LANES] = k_nope[:, d0:d0 + HEAD_DIM].astype(BF16)
        kp_ref[:, c0 + LANES:c0 + FEAT] = k_rope
        vt_ref[0, h * V_ROWS:h * V_ROWS + HEAD_DIM, :] = v_t[d0:d0 + HEAD_DIM].astype(BF16)
    pad = V_ROWS - HEAD_DIM
    ones_blk = jnp.where(lax.broadcasted_iota(jnp.int32, (pad, tm), 0) == 0,
                         1.0, 0.0).astype(BF16)
    for h in range(2 * N_HEADS):
        vt_ref[0, h * V_ROWS + HEAD_DIM:(h + 1) * V_ROWS, :] = ones_blk


def _prep(lat, small, rest, pos, g_q, g_kv, w_uq_t, w_k, w_v_t, b_f, inv_f, batch, seq):
    t = lat.shape[0]
    ns = seq // PREP_TM
    per = ATT_T // PREP_TM
    nt = t // ATT_T
    row = lambda b, i: (b * ns + i, 0)
    const = lambda b, i: (0, 0)
    tile_t = lambda b, i: ((b * ns + i) // per, 0, (b * ns + i) % per)
    return pl.pallas_call(
        _prep_kernel,
        grid=(batch, ns),
        in_specs=[
            pl.BlockSpec((PREP_TM, LAT_DIM), row),
            pl.BlockSpec((PREP_TM, SMALL_W), row),
            pl.BlockSpec((PREP_TM, BRANCH_WIDTH), lambda b, i: (b * ns + i, 1)),
            pl.BlockSpec((PREP_TM, BRANCH_WIDTH), lambda b, i: (b * ns + i, 2)),
            pl.BlockSpec((PREP_TM, BRANCH_WIDTH), lambda b, i: (b * ns + i, 3)),
            pl.BlockSpec((1, 1, PREP_TM), lambda b, i: (b * ns + i, 0, 0)),
            pl.BlockSpec((1, Q_RANK), const),
            pl.BlockSpec((1, KV_RANK), const),
            pl.BlockSpec((N_HEADS * QK_DIM, Q_RANK), const),
            pl.BlockSpec((KV_RANK, BRANCH_WIDTH), const),
            pl.BlockSpec((BRANCH_WIDTH, KV_RANK), const),
            pl.BlockSpec((1, LANES), const),
            pl.BlockSpec((ROPE_HALF, 1), const),
        ],
        out_specs=[
            pl.BlockSpec((1, 2 * N_HEADS * FEAT, PREP_TM), tile_t),
            pl.BlockSpec((PREP_TM, 2 * N_HEADS * FEAT), row),
            pl.BlockSpec((1, 2 * N_HEADS * V_ROWS, PREP_TM), tile_t),
        ],
        out_shape=[
            jax.ShapeDtypeStruct((nt, 2 * N_HEADS * FEAT, ATT_T), BF16),
            jax.ShapeDtypeStruct((t, 2 * N_HEADS * FEAT), BF16),
            jax.ShapeDtypeStruct((nt, 2 * N_HEADS * V_ROWS, ATT_T), BF16),
        ],
        scratch_shapes=[pltpu.VMEM((1, LANES), F32)],
        compiler_params=pltpu.CompilerParams(
            dimension_semantics=("arbitrary", "arbitrary"),
            vmem_limit_bytes=VMEM_LIMIT),
        name="prep",
    )(lat, small, rest, rest, rest, pos, g_q, g_kv, w_uq_t, w_k, w_v_t, b_f, inv_f)


def _attn_kernel(qt_ref, k_ref, vt_ref, o_ref, s0, s1, cm0, cm1, m_scr, acc_scr):
    t = ATT_T
    nq = qt_ref.shape[0]
    steps = [(i, j) for i in range(nq) for j in range(i + 1)]
    nsteps = len(steps)
    s_buf, cm_buf = (s0, s1), (cm0, cm1)

    nblk = t // ATT_BLK
    blk = ATT_BLK

    def a_begin():
        return [[None] * nblk for _ in range(ATT_HB)]

    def a_piece(ij, slot, diagonal, hh, r, cmax):
        i, j = ij
        f0, r0 = hh * FEAT, r * blk
        k = k_ref[j * t + r0:j * t + r0 + blk, f0:f0 + FEAT]
        c_lo = r0 if diagonal else 0
        s = jnp.dot(k, qt_ref[i, f0:f0 + FEAT, c_lo:], preferred_element_type=F32)
        if diagonal:
            key = lax.broadcasted_iota(jnp.int32, (blk, blk), 0)
            qry = lax.broadcasted_iota(jnp.int32, (blk, blk), 1)
            parts = [jnp.where(key <= qry, s[:, :blk], -jnp.inf)]
            if s.shape[1] > blk:
                parts.append(s[:, blk:])
            s = jnp.concatenate(parts, axis=1)
        s_buf[slot][hh, r0:r0 + blk, c_lo:] = s
        for qb in range(c_lo // blk, nblk):
            part = jnp.max(s[:, qb * blk - c_lo:(qb + 1) * blk - c_lo], axis=0, keepdims=True)
            cmax[hh][qb] = part if cmax[hh][qb] is None else jnp.maximum(cmax[hh][qb], part)

    def a_end(slot, hh, cmax):
        for qb in range(nblk):
            cm_buf[slot][hh, :, qb * blk:(qb + 1) * blk] = cmax[hh][qb]

    def bc_begin(ij, slot, hh):
        i, j = ij
        if j == 0:
            return dict(m_new=cm_buf[slot][hh], alpha=None, pv=None)
        m_old = m_scr[hh]
        m_new = jnp.maximum(m_old, cm_buf[slot][hh])
        return dict(m_new=m_new, alpha=jnp.exp2(m_old - m_new), pv=None)

    def rescaled(st, hh, cols, pv):
        if st["alpha"] is None:
            return pv
        return st["alpha"][:, cols] * acc_scr[hh, :, cols] + pv

    def bc_piece(ij, slot, hh, r, st):
        i, j = ij
        d0, r0 = hh * V_ROWS, r * blk
        p = jnp.exp2(s_buf[slot][hh, r0:r0 + blk, :] - st["m_new"])
        pv = jnp.dot(vt_ref[j, d0:d0 + V_ROWS, r0:r0 + blk], p.astype(BF16),
                     preferred_element_type=F32)
        st["pv"] = pv if st["pv"] is None else st["pv"] + pv

    def bc_end(hh, st):
        m_scr[hh] = st["m_new"]
        acc_scr[hh] = rescaled(st, hh, slice(None), st["pv"])

    def bcd_begin(ij, slot, hh):
        st = bc_begin(ij, slot, hh)
        st["pv"] = [None] * nblk
        return st

    def bcd_piece(ij, slot, hh, r, st):
        i, j = ij
        d0, r0 = hh * V_ROWS, r * blk
        p = jnp.exp2(s_buf[slot][hh, r0:r0 + blk, r0:] - st["m_new"][:, r0:])
        pv = jnp.dot(vt_ref[j, d0:d0 + V_ROWS, r0:r0 + blk], p.astype(BF16),
                     preferred_element_type=F32)
        for qb in range(r, nblk):
            part = pv[:, (qb - r) * blk:(qb - r + 1) * blk]
            st["pv"][qb] = part if st["pv"][qb] is None else st["pv"][qb] + part

    def bcd_end(ij, hh, st):
        i, j = ij
        for qb in range(nblk):
            cols = slice(qb * blk, (qb + 1) * blk)
            acc = rescaled(st, hh, cols, st["pv"][qb])
            out = acc[:HEAD_DIM] / acc[HEAD_DIM:HEAD_DIM + 1]
            o_ref[i * t + qb * blk:i * t + (qb + 1) * blk,
                  hh * HEAD_DIM:(hh + 1) * HEAD_DIM] = out.T.astype(BF16)

    def run(slot_a, ija, ijb, a_is_diag, b_is_diag):
        slot_b = 1 - slot_a
        begin, piece = (bcd_begin, bcd_piece) if b_is_diag else (bc_begin, bc_piece)
        cmax = a_begin()
        sts = [begin(ijb, slot_b, hh) for hh in range(ATT_HB)]
        for r in range(nblk):
            for hh in range(ATT_HB):
                piece(ijb, slot_b, hh, r, sts[hh])
                if ija is not None:
                    a_piece(ija, slot_a, a_is_diag, hh, r, cmax)
        for hh in range(ATT_HB):
            if ija is not None:
                a_end(slot_a, hh, cmax)
            if b_is_diag:
                bcd_end(ijb, hh, sts[hh])
            else:
                bc_end(hh, sts[hh])

    def iteration(m):
        if m == 0:
            cmax0 = a_begin()
            for hh in range(ATT_HB):
                for r in range(nblk):
                    a_piece(steps[0], 0, True, hh, r, cmax0)
                a_end(0, hh, cmax0)
        elif m < nsteps:
            sa, sb = steps[m], steps[m - 1]
            run(m % 2, sa, sb, sa[0] == sa[1], sb[0] == sb[1])
        else:
            run(m % 2, None, steps[-1], False, True)

    def block(first):
        def emit():
            for m in range(first, min(first + ATT_GROUP, nsteps + 1)):
                iteration(m)
        return emit

    firsts = list(range(0, nsteps + 1, ATT_GROUP))

    def trip(p, carry):
        for q, first in enumerate(firsts):
            pl.when(p == q)(block(first))
        return carry

    lax.fori_loop(0, len(firsts), trip, 0)


def _attn(qt, kp, vt, batch, seq):
    t = kp.shape[0]
    nq = seq // ATT_T
    nh = 2 * N_HEADS // ATT_HB
    row = pltpu.VMEM((ATT_HB, 1, ATT_T), F32)
    tile = pltpu.VMEM((ATT_HB, ATT_T, ATT_T), F32)
    return pl.pallas_call(
        _attn_kernel,
        grid=(batch, nh),
        in_specs=[
            pl.BlockSpec((nq, ATT_HB * FEAT, ATT_T), lambda b, h: (b, h, 0)),
            pl.BlockSpec((seq, ATT_HB * FEAT), lambda b, h: (b, h)),
            pl.BlockSpec((nq, ATT_HB * V_ROWS, ATT_T), lambda b, h: (b, h, 0)),
        ],
        out_specs=pl.BlockSpec((seq, ATT_HB * HEAD_DIM), lambda b, h: (b, h)),
        out_shape=jax.ShapeDtypeStruct((t, D_MIX), BF16),
        scratch_shapes=[
            tile, tile, row, row, row,
            pltpu.VMEM((ATT_HB, V_ROWS, ATT_T), F32),
        ],
        compiler_params=pltpu.CompilerParams(
            dimension_semantics=("arbitrary", "arbitrary"),
            vmem_limit_bytes=VMEM_LIMIT),
        name="attn",
    )(qt, kp, vt)


def _silu(g):
    return g / (1.0 + jnp.exp(-g))


def _out_kernel(o_ref, gm_ref, gf_ref, x_ref, w_ref, g_ref, y_ref):
    o = o_ref[...].astype(F32)
    om = (o[:, :BRANCH_WIDTH] * _silu(gm_ref[...].astype(F32))).astype(BF16)
    of = (o[:, BRANCH_WIDTH:] * _silu(gf_ref[...].astype(F32))).astype(BF16)
    y = (jnp.dot(om, w_ref[:BRANCH_WIDTH, :], preferred_element_type=F32)
         + jnp.dot(of, w_ref[BRANCH_WIDTH:, :], preferred_element_type=F32))
    y_ref[...] = x_ref[...] + _rms(y, g_ref[...])


def _out(o, rest, x2, w_out, g_post):
    t = o.shape[0]
    return pl.pallas_call(
        _out_kernel,
        grid=(t // OUT_TM,),
        in_specs=[
            pl.BlockSpec((OUT_TM, D_MIX), lambda i: (i, 0)),
            pl.BlockSpec((OUT_TM, BRANCH_WIDTH), lambda i: (i, 0)),
            pl.BlockSpec((OUT_TM, BRANCH_WIDTH), lambda i: (i, 4)),
            pl.BlockSpec((OUT_TM, D_MODEL), lambda i: (i, 0)),
            pl.BlockSpec((D_MIX, D_MODEL), lambda i: (0, 0)),
            pl.BlockSpec((1, D_MODEL), lambda i: (0, 0)),
        ],
        out_specs=pl.BlockSpec((OUT_TM, D_MODEL), lambda i: (i, 0)),
        out_shape=jax.ShapeDtypeStruct((t, D_MODEL), F32),
        compiler_params=pltpu.CompilerParams(
            dimension_semantics=("arbitrary",),
            vmem_limit_bytes=VMEM_LIMIT),
        name="out",
    )(o, rest, rest, x2, w_out, g_post)


def _layer(x2, pos, inv_f, g_pre, w_in, g_q, w_uq, g_kv, w_ukv, b_forget,
           w_out, g_post, batch, seq):
    w_all = _pack_w_in(jnp.swapaxes(w_in, 0, 1))
    w_q = w_uq.T.reshape(N_HEADS, QK_DIM, Q_RANK)
    w_uq_t = jnp.concatenate([
        w_q[:, :NOPE_DIM].reshape(N_HEADS * NOPE_DIM, Q_RANK),
        w_q[:, NOPE_DIM:].reshape(N_HEADS * ROPE_DIM, Q_RANK)], axis=0).astype(BF16)
    w_kv = w_ukv.reshape(KV_RANK, N_HEADS, NOPE_DIM + HEAD_DIM)
    w_k = w_kv[:, :, :NOPE_DIM].reshape(KV_RANK, BRANCH_WIDTH).astype(BF16)
    w_v_t = w_kv[:, :, NOPE_DIM:].reshape(KV_RANK, BRANCH_WIDTH).T.astype(BF16)
    w_out_b = w_out.astype(BF16)
    b_f = jnp.pad(b_forget, (0, LANES - N_HEADS))[None, :]

    lat, small, rest = _proj(x2, g_pre[None, :], w_all)
    qt, kp, vt = _prep(lat, small, rest, pos, g_q[None, :], g_kv[None, :],
                       w_uq_t, w_k, w_v_t, b_f, inv_f, batch, seq)
    o = _attn(qt, kp, vt, batch, seq)
    return _out(o, rest, x2, w_out_b, g_post[None, :])


def kernel(x, positions, g_pre, w_in, g_q_latent, w_uq, g_kv_latent, w_ukv,
           b_forget, w_out, g_post):
    batch, seq, d = x.shape
    depth = g_pre.shape[0]
    x2 = x.reshape(batch * seq, d)
    pos = positions.reshape(batch * seq // PREP_TM, 1, PREP_TM)
    inv_f = (ROPE_THETA ** (-jnp.arange(0, ROPE_DIM, 2, dtype=F32) / ROPE_DIM))[:, None]
    for l in range(depth):
        x2 = _layer(x2, pos, inv_f, g_pre[l], w_in[l], g_q_latent[l], w_uq[l],
                    g_kv_latent[l], w_ukv[l], b_forget[l], w_out[l], g_post[l],
                    batch, seq)
    return x2.reshape(batch, seq, d)
```

```python
import math

import jax
import jax.numpy as jnp
from jax import lax
from jax.experimental import pallas as pl
from jax.experimental.pallas import tpu as pltpu

F32 = jnp.float32
BF16 = jnp.bfloat16

D_MODEL = 2048
N_HEADS = 8
HEAD_DIM = 128
NOPE_DIM = 128
ROPE_DIM = 64
ROPE_HALF = ROPE_DIM // 2
QK_DIM = NOPE_DIM + ROPE_DIM
Q_RANK = 768
KV_RANK = 512
LAT_DIM = Q_RANK + KV_RANK
BRANCH_WIDTH = N_HEADS * HEAD_DIM
D_MIX = 2 * BRANCH_WIDTH
ROPE_THETA = 10000.0
NORM_EPS = 1e-6
LOG2E = math.log2(math.e)

LANES = 128
FEAT = 256
SMALL_W = 2 * LANES
REST_W = 5 * BRANCH_WIDTH
T_FQ, T_FV, T_FK, T_GM, T_GF = range(5)

MLA_QSCALE = QK_DIM ** -0.5 * LOG2E
FOX_QSCALE = HEAD_DIM ** -0.5 * LOG2E

PACK_TN = 256
PROJ_TM = 1024
PREP_TM = 512
ATT_T = 1024
ATT_BLK = 256
ATT_HB = 2
ATT_GROUP = 8
V_ROWS = HEAD_DIM + 16
OUT_TM = 512

V7X_VMEM_BYTES = 64 * 1024 * 1024
VMEM_LIMIT = V7X_VMEM_BYTES * 7 // 8


def _rms(x, g):
    return x * lax.rsqrt(jnp.mean(x * x, axis=-1, keepdims=True) + NORM_EPS) * g


def _pack_kernel(w_ref, o_ref):
    o_kr = LAT_DIM
    o_gm = o_kr + ROPE_DIM
    o_fq = o_gm + BRANCH_WIDTH
    o_fk = o_fq + BRANCH_WIDTH
    o_fv = o_fk + BRANCH_WIDTH
    o_fl = o_fv + BRANCH_WIDTH
    o_gf = o_fl + N_HEADS
    cols = w_ref.shape[1]
    zeros = lambda n: jnp.zeros((n, cols), F32)
    o_ref[...] = jnp.concatenate([
        w_ref[o_fq:o_fk, :], w_ref[o_fv:o_fl, :], w_ref[o_fk:o_fv, :],
        w_ref[o_gm:o_fq, :], w_ref[o_gf:o_gf + BRANCH_WIDTH, :],
        w_ref[:LAT_DIM, :],
        w_ref[o_kr:o_gm, :], zeros(LANES - ROPE_DIM),
        w_ref[o_fl:o_gf, :], zeros(LANES - N_HEADS),
    ], axis=0).astype(BF16)


def _pack_w_in(w_in_t):
    d_in, d = w_in_t.shape
    height = REST_W + LAT_DIM + SMALL_W
    return pl.pallas_call(
        _pack_kernel,
        grid=(d // PACK_TN,),
        in_specs=[pl.BlockSpec((d_in, PACK_TN), lambda i: (0, i))],
        out_specs=pl.BlockSpec((height, PACK_TN), lambda i: (0, i)),
        out_shape=jax.ShapeDtypeStruct((height, d), BF16),
        compiler_params=pltpu.CompilerParams(
            dimension_semantics=("arbitrary",),
            vmem_limit_bytes=VMEM_LIMIT),
        name="pack",
    )(w_in_t)


def _dot_nt(a, b):
    return lax.dot_general(a, b, (((1,), (1,)), ((), ())), preferred_element_type=F32)


def _proj_kernel(x_ref, g_ref, wlat_ref, wsm_ref, wrest_ref,
                 lat_ref, sm_ref, rest_ref, h_scr):
    j = pl.program_id(1)

    @pl.when(j == 0)
    def _():
        hb = _rms(x_ref[...], g_ref[...]).astype(BF16)
        h_scr[...] = hb
        lat_ref[...] = _dot_nt(hb, wlat_ref[...]).astype(BF16)
        sm_ref[...] = _dot_nt(hb, wsm_ref[...])

    acc = _dot_nt(h_scr[...], wrest_ref[...])
    scale = jnp.where(j == T_FQ, FOX_QSCALE, 1.0).astype(F32)
    rest_ref[...] = (acc * scale).astype(BF16)


def _proj(x2, g_pre, w_all):
    t = x2.shape[0]
    n_rest = REST_W // BRANCH_WIDTH
    assert REST_W % LAT_DIM == 0 and (REST_W + LAT_DIM) % SMALL_W == 0
    once = pl.Buffered(1)
    return pl.pallas_call(
        _proj_kernel,
        grid=(t // PROJ_TM, n_rest),
        in_specs=[
            pl.BlockSpec((PROJ_TM, D_MODEL), lambda i, j: (i, 0)),
            pl.BlockSpec((1, D_MODEL), lambda i, j: (0, 0)),
            pl.BlockSpec((LAT_DIM, D_MODEL), lambda i, j: (REST_W // LAT_DIM, 0),
                         pipeline_mode=once),
            pl.BlockSpec((SMALL_W, D_MODEL), lambda i, j: ((REST_W + LAT_DIM) // SMALL_W, 0),
                         pipeline_mode=once),
            pl.BlockSpec((BRANCH_WIDTH, D_MODEL), lambda i, j: (j, 0)),
        ],
        out_specs=[
            pl.BlockSpec((PROJ_TM, LAT_DIM), lambda i, j: (i, 0)),
            pl.BlockSpec((PROJ_TM, SMALL_W), lambda i, j: (i, 0)),
            pl.BlockSpec((PROJ_TM, BRANCH_WIDTH), lambda i, j: (i, j)),
        ],
        out_shape=[
            jax.ShapeDtypeStruct((t, LAT_DIM), BF16),
            jax.ShapeDtypeStruct((t, SMALL_W), F32),
            jax.ShapeDtypeStruct((t, REST_W), BF16),
        ],
        scratch_shapes=[pltpu.VMEM((PROJ_TM, D_MODEL), BF16)],
        compiler_params=pltpu.CompilerParams(
            dimension_semantics=("arbitrary", "arbitrary"),
            vmem_limit_bytes=VMEM_LIMIT),
        name="proj",
    )(x2, g_pre, w_all, w_all, w_all)


def _split3(x):
    hi = x.astype(BF16).astype(F32)
    r = x - hi
    mid = r.astype(BF16).astype(F32)
    lo = (r - mid).astype(BF16).astype(F32)
    return hi, mid, lo


def _prep_kernel(lat_ref, sm_ref, fqv_ref, pos_ref,
                 gq_ref, gkv_ref, wuqt_ref, wk_ref, wvt_ref, bf_ref, invf_ref,
                 qt_ref, kn_ref, ke_ref, vt_ref, carry_scr):
    tm = lat_ref.shape[0]

    def put_t(ref, r0, x):
        ref[0, r0:r0 + x.shape[1], :] = x.T

    @pl.when(pl.program_id(1) == 0)
    def _():
        carry_scr[...] = jnp.zeros_like(carry_scr)

    lat = lat_ref[...].astype(F32)
    qn = _rms(lat[:, :Q_RANK], gq_ref[...]).astype(BF16)
    kvn = _rms(lat[:, Q_RANK:], gkv_ref[...]).astype(BF16)

    ang = invf_ref[...] * pos_ref[0].astype(F32)
    cos_t = jnp.cos(ang)
    sin_t = jnp.sin(ang)

    def rope_t(x):
        x1, x2 = x[:ROPE_HALF], x[ROPE_HALF:ROPE_DIM]
        return x1 * cos_t - x2 * sin_t, x2 * cos_t + x1 * sin_t

    kr1, kr2 = rope_t(sm_ref[:, :LANES].T)
    k_rope = jnp.concatenate(
        [kr1, kr2, jnp.zeros((LANES - ROPE_DIM, tm), F32)], axis=0).T.astype(BF16)

    lane = lax.broadcasted_iota(jnp.int32, (tm, LANES), 1)
    fl = sm_ref[:, LANES:] + bf_ref[...]
    logf = jnp.minimum(fl, 0.0) - jnp.log1p(jnp.exp(-jnp.abs(fl)))
    row = lax.broadcasted_iota(jnp.int32, (tm, tm), 0)
    col = lax.broadcasted_iota(jnp.int32, (tm, tm), 1)
    tri = jnp.where(row >= col, 1.0, 0.0).astype(BF16)
    c = carry_scr[...]
    for piece in _split3(logf):
        c = c + jnp.dot(tri, piece.astype(BF16), preferred_element_type=F32)
    carry_scr[...] = c[tm - 1:tm, :]
    c2 = c * LOG2E
    c2_t = c2.T

    aug = 16
    srow = lax.broadcasted_iota(jnp.int32, (aug, tm), 0)
    q_zero = jnp.zeros((FEAT - HEAD_DIM - aug, tm), BF16)
    q_pad = jnp.zeros((FEAT - QK_DIM, tm), BF16)

    def fox_head(h):
        c0 = (N_HEADS + h) * FEAT
        s0 = h * HEAD_DIM
        hi, mid, lo = _split3(jnp.broadcast_to(c2[:, h:h + 1], (tm, LANES)))
        ck = jnp.where(lane < 3, 1.0, jnp.where(lane == 3, -hi, jnp.where(
            lane == 4, -mid, jnp.where(lane == 5, -lo, 0.0))))
        ke_ref[:, BRANCH_WIDTH + s0:BRANCH_WIDTH + s0 + HEAD_DIM] = ck.astype(BF16)
        hi_t, mid_t, lo_t = _split3(c2_t[h:h + 1, :])
        cq_t = jnp.where(srow == 0, hi_t, jnp.where(srow == 1, mid_t, jnp.where(
            srow == 2, lo_t, jnp.where(srow < 6, 1.0, 0.0))))
        put_t(qt_ref, c0, fqv_ref[:, s0:s0 + HEAD_DIM])
        qt_ref[0, c0 + HEAD_DIM:c0 + HEAD_DIM + aug, :] = cq_t.astype(BF16)
        qt_ref[0, c0 + HEAD_DIM + aug:c0 + FEAT, :] = q_zero
        put_t(vt_ref, (N_HEADS + h) * V_ROWS,
              fqv_ref[:, BRANCH_WIDTH + s0:BRANCH_WIDTH + s0 + HEAD_DIM])

    def mla_q_head(h, q_nope, base, q_rope):
        c0, l0 = h * FEAT, (h - base) * NOPE_DIM
        qr1, qr2 = rope_t(q_rope[h * ROPE_DIM:(h + 1) * ROPE_DIM])
        qt_ref[0, c0:c0 + NOPE_DIM, :] = q_nope[l0:l0 + NOPE_DIM].astype(BF16)
        qt_ref[0, c0 + NOPE_DIM:c0 + NOPE_DIM + ROPE_HALF, :] = qr1.astype(BF16)
        qt_ref[0, c0 + NOPE_DIM + ROPE_HALF:c0 + QK_DIM, :] = qr2.astype(BF16)
        qt_ref[0, c0 + QK_DIM:c0 + FEAT, :] = q_pad

    half = N_HEADS // 2
    n_nope = N_HEADS * NOPE_DIM
    q_a = _dot_nt(wuqt_ref[:n_nope // 2, :], qn) * MLA_QSCALE
    for h in range(half):
        fox_head(h)
    q_b = _dot_nt(wuqt_ref[n_nope // 2:n_nope, :], qn) * MLA_QSCALE
    for h in range(half, N_HEADS):
        fox_head(h)
    q_r = _dot_nt(wuqt_ref[n_nope:, :], qn) * MLA_QSCALE
    k_nope = jnp.dot(kvn, wk_ref[...], preferred_element_type=F32)
    for h in range(half):
        mla_q_head(h, q_a, 0, q_r)
    v_t = _dot_nt(wvt_ref[...], kvn)
    for h in range(half, N_HEADS):
        mla_q_head(h, q_b, half, q_r)
    for h in range(N_HEADS):
        c0, d0 = h * FEAT, h * HEAD_DIM
        kn_ref[:, d0:d0 + HEAD_DIM] = k_nope[:, d0:d0 + HEAD_DIM].astype(BF16)
        ke_ref[:, d0:d0 + HEAD_DIM] = k_rope
        vt_ref[0, h * V_ROWS:h * V_ROWS + HEAD_DIM, :] = v_t[d0:d0 + HEAD_DIM].astype(BF16)
    pad = V_ROWS - HEAD_DIM
    ones_blk = jnp.where(lax.broadcasted_iota(jnp.int32, (pad, tm), 0) == 0,
                         1.0, 0.0).astype(BF16)
    for h in range(2 * N_HEADS):
        vt_ref[0, h * V_ROWS + HEAD_DIM:(h + 1) * V_ROWS, :] = ones_blk


def _prep(lat, small, rest, pos, g_q, g_kv, w_uq_t, w_k, w_v_t, b_f, inv_f, batch, seq):
    t = lat.shape[0]
    ns = seq // PREP_TM
    per = ATT_T // PREP_TM
    nt = t // ATT_T
    row = lambda b, i: (b * ns + i, 0)
    const = lambda b, i: (0, 0)
    tile_t = lambda b, i: ((b * ns + i) // per, 0, (b * ns + i) % per)
    assert (T_FQ, T_FV) == (0, 1)
    return pl.pallas_call(
        _prep_kernel,
        grid=(batch, ns),
        in_specs=[
            pl.BlockSpec((PREP_TM, LAT_DIM), row),
            pl.BlockSpec((PREP_TM, SMALL_W), row),
            pl.BlockSpec((PREP_TM, 2 * BRANCH_WIDTH), row),
            pl.BlockSpec((1, 1, PREP_TM), lambda b, i: (b * ns + i, 0, 0)),
            pl.BlockSpec((1, Q_RANK), const),
            pl.BlockSpec((1, KV_RANK), const),
            pl.BlockSpec((N_HEADS * QK_DIM, Q_RANK), const),
            pl.BlockSpec((KV_RANK, BRANCH_WIDTH), const),
            pl.BlockSpec((BRANCH_WIDTH, KV_RANK), const),
            pl.BlockSpec((1, LANES), const),
            pl.BlockSpec((ROPE_HALF, 1), const),
        ],
        out_specs=[
            pl.BlockSpec((1, 2 * N_HEADS * FEAT, PREP_TM), tile_t),
            pl.BlockSpec((PREP_TM, BRANCH_WIDTH), lambda b, i: (b * ns + i, T_FQ)),
            pl.BlockSpec((PREP_TM, D_MIX), row),
            pl.BlockSpec((1, 2 * N_HEADS * V_ROWS, PREP_TM), tile_t),
        ],
        out_shape=[
            jax.ShapeDtypeStruct((nt, 2 * N_HEADS * FEAT, ATT_T), BF16),
            jax.ShapeDtypeStruct(rest.shape, BF16),
            jax.ShapeDtypeStruct((t, D_MIX), BF16),
            jax.ShapeDtypeStruct((nt, 2 * N_HEADS * V_ROWS, ATT_T), BF16),
        ],
        scratch_shapes=[pltpu.VMEM((1, LANES), F32)],
        input_output_aliases={2: 1},
        compiler_params=pltpu.CompilerParams(
            dimension_semantics=("arbitrary", "arbitrary"),
            vmem_limit_bytes=VMEM_LIMIT),
        name="prep",
    )(lat, small, rest, pos, g_q, g_kv, w_uq_t, w_k, w_v_t, b_f, inv_f)


def _attn_kernel(qt_ref, km_ref, ke_ref, vt_ref, o_ref, s0, s1, cm0, cm1, m_scr, acc_scr):
    t = ATT_T
    nq = qt_ref.shape[0]
    steps = [(i, j) for i in range(nq) for j in range(i + 1)]
    nsteps = len(steps)
    s_buf, cm_buf = (s0, s1), (cm0, cm1)

    nblk = t // ATT_BLK
    blk = ATT_BLK

    def a_begin():
        return [[None] * nblk for _ in range(ATT_HB)]

    def a_piece(ij, slot, diagonal, hh, r, cmax):
        i, j = ij
        f0, r0 = hh * FEAT, r * blk
        krows = slice(j * t + r0, j * t + r0 + blk)
        kcols = slice(hh * HEAD_DIM, (hh + 1) * HEAD_DIM)
        k = jnp.concatenate([km_ref[krows, kcols], ke_ref[krows, kcols]], axis=1)
        c_lo = r0 if diagonal else 0
        s = jnp.dot(k, qt_ref[i, f0:f0 + FEAT, c_lo:], preferred_element_type=F32)
        if diagonal:
            key = lax.broadcasted_iota(jnp.int32, (blk, blk), 0)
            qry = lax.broadcasted_iota(jnp.int32, (blk, blk), 1)
            parts = [jnp.where(key <= qry, s[:, :blk], -jnp.inf)]
            if s.shape[1] > blk:
                parts.append(s[:, blk:])
            s = jnp.concatenate(parts, axis=1)
        s_buf[slot][hh, r0:r0 + blk, c_lo:] = s
        for qb in range(c_lo // blk, nblk):
            part = jnp.max(s[:, qb * blk - c_lo:(qb + 1) * blk - c_lo], axis=0, keepdims=True)
            cmax[hh][qb] = part if cmax[hh][qb] is None else jnp.maximum(cmax[hh][qb], part)

    def a_end(slot, hh, cmax):
        for qb in range(nblk):
            cm_buf[slot][hh, :, qb * blk:(qb + 1) * blk] = cmax[hh][qb]

    def bc_begin(ij, slot, hh):
        i, j = ij
        if j == 0:
            return dict(m_new=cm_buf[slot][hh], alpha=None, pv=None)
        m_old = m_scr[hh]
        m_new = jnp.maximum(m_old, cm_buf[slot][hh])
        return dict(m_new=m_new, alpha=jnp.exp2(m_old - m_new), pv=None)

    def rescaled(st, hh, cols, pv):
        if st["alpha"] is None:
            return pv
        return st["alpha"][:, cols] * acc_scr[hh, :, cols] + pv

    def bc_piece(ij, slot, hh, r, st):
        i, j = ij
        d0, r0 = hh * V_ROWS, r * blk
        p = jnp.exp2(s_buf[slot][hh, r0:r0 + blk, :] - st["m_new"])
        pv = jnp.dot(vt_ref[j, d0:d0 + V_ROWS, r0:r0 + blk], p.astype(BF16),
                     preferred_element_type=F32)
        st["pv"] = pv if st["pv"] is None else st["pv"] + pv

    def bc_end(hh, st):
        m_scr[hh] = st["m_new"]
        acc_scr[hh] = rescaled(st, hh, slice(None), st["pv"])

    def bcd_begin(ij, slot, hh):
        st = bc_begin(ij, slot, hh)
        st["pv"] = [None] * nblk
        return st

    def bcd_piece(ij, slot, hh, r, st):
        i, j = ij
        d0, r0 = hh * V_ROWS, r * blk
        p = jnp.exp2(s_buf[slot][hh, r0:r0 + blk, r0:] - st["m_new"][:, r0:])
        pv = jnp.dot(vt_ref[j, d0:d0 + V_ROWS, r0:r0 + blk], p.astype(BF16),
                     preferred_element_type=F32)
        for qb in range(r, nblk):
            part = pv[:, (qb - r) * blk:(qb - r + 1) * blk]
            st["pv"][qb] = part if st["pv"][qb] is None else st["pv"][qb] + part

    def bcd_end(ij, hh, st):
        i, j = ij
        for qb in range(nblk):
            cols = slice(qb * blk, (qb + 1) * blk)
            acc = rescaled(st, hh, cols, st["pv"][qb])
            out = acc[:HEAD_DIM] / acc[HEAD_DIM:HEAD_DIM + 1]
            o_ref[i * t + qb * blk:i * t + (qb + 1) * blk,
                  hh * HEAD_DIM:(hh + 1) * HEAD_DIM] = out.T.astype(BF16)

    def run(slot_a, ija, ijb, a_is_diag, b_is_diag):
        slot_b = 1 - slot_a
        begin, piece = (bcd_begin, bcd_piece) if b_is_diag else (bc_begin, bc_piece)
        cmax = a_begin()
        sts = [begin(ijb, slot_b, hh) for hh in range(ATT_HB)]
        for r in range(nblk):
            for hh in range(ATT_HB):
                piece(ijb, slot_b, hh, r, sts[hh])
                if ija is not None:
                    a_piece(ija, slot_a, a_is_diag, hh, r, cmax)
        for hh in range(ATT_HB):
            if ija is not None:
                a_end(slot_a, hh, cmax)
            if b_is_diag:
                bcd_end(ijb, hh, sts[hh])
            else:
                bc_end(hh, sts[hh])

    def iteration(m):
        if m == 0:
            cmax0 = a_begin()
            for hh in range(ATT_HB):
                for r in range(nblk):
                    a_piece(steps[0], 0, True, hh, r, cmax0)
                a_end(0, hh, cmax0)
        elif m < nsteps:
            sa, sb = steps[m], steps[m - 1]
            run(m % 2, sa, sb, sa[0] == sa[1], sb[0] == sb[1])
        else:
            run(m % 2, None, steps[-1], False, True)

    def block(first):
        def emit():
            for m in range(first, min(first + ATT_GROUP, nsteps + 1)):
                iteration(m)
        return emit

    firsts = list(range(0, nsteps + 1, ATT_GROUP))

    def trip(p, carry):
        for q, first in enumerate(firsts):
            pl.when(p == q)(block(first))
        return carry

    lax.fori_loop(0, len(firsts), trip, 0)


def _attn(qt, rest, ke, vt, batch, seq):
    t = ke.shape[0]
    per_tile = BRANCH_WIDTH // (ATT_HB * HEAD_DIM)
    km_block = lambda h: jnp.where(h < per_tile, T_FQ * per_tile + h,
                                   T_FK * per_tile + h - per_tile)
    nq = seq // ATT_T
    nh = 2 * N_HEADS // ATT_HB
    row = pltpu.VMEM((ATT_HB, 1, ATT_T), F32)
    tile = pltpu.VMEM((ATT_HB, ATT_T, ATT_T), F32)
    return pl.pallas_call(
        _attn_kernel,
        grid=(batch, nh),
        in_specs=[
            pl.BlockSpec((nq, ATT_HB * FEAT, ATT_T), lambda b, h: (b, h, 0)),
            pl.BlockSpec((seq, ATT_HB * HEAD_DIM), lambda b, h: (b, km_block(h))),
            pl.BlockSpec((seq, ATT_HB * HEAD_DIM), lambda b, h: (b, h)),
            pl.BlockSpec((nq, ATT_HB * V_ROWS, ATT_T), lambda b, h: (b, h, 0)),
        ],
        out_specs=pl.BlockSpec((seq, ATT_HB * HEAD_DIM), lambda b, h: (b, h)),
        out_shape=jax.ShapeDtypeStruct((t, D_MIX), BF16),
        scratch_shapes=[
            tile, tile, row, row, row,
            pltpu.VMEM((ATT_HB, V_ROWS, ATT_T), F32),
        ],
        compiler_params=pltpu.CompilerParams(
            dimension_semantics=("arbitrary", "arbitrary"),
            vmem_limit_bytes=VMEM_LIMIT),
        name="attn",
    )(qt, rest, ke, vt)


def _silu(g):
    return g / (1.0 + jnp.exp(-g))


def _out_kernel(o_ref, gm_ref, gf_ref, x_ref, w_ref, g_ref, y_ref):
    o = o_ref[...].astype(F32)
    om = (o[:, :BRANCH_WIDTH] * _silu(gm_ref[...].astype(F32))).astype(BF16)
    of = (o[:, BRANCH_WIDTH:] * _silu(gf_ref[...].astype(F32))).astype(BF16)
    y = (jnp.dot(om, w_ref[:BRANCH_WIDTH, :], preferred_element_type=F32)
         + jnp.dot(of, w_ref[BRANCH_WIDTH:, :], preferred_element_type=F32))
    y_ref[...] = x_ref[...] + _rms(y, g_ref[...])


def _out(o, rest, x2, w_out, g_post):
    t = o.shape[0]
    return pl.pallas_call(
        _out_kernel,
        grid=(t // OUT_TM,),
        in_specs=[
            pl.BlockSpec((OUT_TM, D_MIX), lambda i: (i, 0)),
            pl.BlockSpec((OUT_TM, BRANCH_WIDTH), lambda i: (i, T_GM)),
            pl.BlockSpec((OUT_TM, BRANCH_WIDTH), lambda i: (i, T_GF)),
            pl.BlockSpec((OUT_TM, D_MODEL), lambda i: (i, 0)),
            pl.BlockSpec((D_MIX, D_MODEL), lambda i: (0, 0)),
            pl.BlockSpec((1, D_MODEL), lambda i: (0, 0)),
        ],
        out_specs=pl.BlockSpec((OUT_TM, D_MODEL), lambda i: (i, 0)),
        out_shape=jax.ShapeDtypeStruct((t, D_MODEL), F32),
        compiler_params=pltpu.CompilerParams(
            dimension_semantics=("arbitrary",),
            vmem_limit_bytes=VMEM_LIMIT),
        name="out",
    )(o, rest, rest, x2, w_out, g_post)


def _layer(x2, pos, inv_f, g_pre, w_in, g_q, w_uq, g_kv, w_ukv, b_forget,
           w_out, g_post, batch, seq):
    w_all = _pack_w_in(jnp.swapaxes(w_in, 0, 1))
    w_q = w_uq.T.reshape(N_HEADS, QK_DIM, Q_RANK)
    w_uq_t = jnp.concatenate([
        w_q[:, :NOPE_DIM].reshape(N_HEADS * NOPE_DIM, Q_RANK),
        w_q[:, NOPE_DIM:].reshape(N_HEADS * ROPE_DIM, Q_RANK)], axis=0).astype(BF16)
    w_kv = w_ukv.reshape(KV_RANK, N_HEADS, NOPE_DIM + HEAD_DIM)
    w_k = w_kv[:, :, :NOPE_DIM].reshape(KV_RANK, BRANCH_WIDTH).astype(BF16)
    w_v_t = w_kv[:, :, NOPE_DIM:].reshape(KV_RANK, BRANCH_WIDTH).T.astype(BF16)
    w_out_b = w_out.astype(BF16)
    b_f = jnp.pad(b_forget, (0, LANES - N_HEADS))[None, :]

    lat, small, rest = _proj(x2, g_pre[None, :], w_all)
    qt, rest, ke, vt = _prep(lat, small, rest, pos, g_q[None, :], g_kv[None, :],
                       w_uq_t, w_k, w_v_t, b_f, inv_f, batch, seq)
    o = _attn(qt, rest, ke, vt, batch, seq)
    return _out(o, rest, x2, w_out_b, g_post[None, :])


def kernel(x, positions, g_pre, w_in, g_q_latent, w_uq, g_kv_latent, w_ukv,
           b_forget, w_out, g_post):
    batch, seq, d = x.shape
    depth = g_pre.shape[0]
    x2 = x.reshape(batch * seq, d)
    pos = positions.reshape(batch * seq // PREP_TM, 1, PREP_TM)
    inv_f = (ROPE_THETA ** (-jnp.arange(0, ROPE_DIM, 2, dtype=F32) / ROPE_DIM))[:, None]
    for l in range(depth):
        x2 = _layer(x2, pos, inv_f, g_pre[l], w_in[l], g_q_latent[l], w_uq[l],
                    g_kv_latent[l], w_ukv[l], b_forget[l], w_out[l], g_post[l],
                    batch, seq)
    return x2.reshape(batch, seq, d)
```

```python
import math

import jax
import jax.numpy as jnp
from jax import lax
from jax.experimental import pallas as pl
from jax.experimental.pallas import tpu as pltpu

F32 = jnp.float32
BF16 = jnp.bfloat16

D_MODEL = 2048
N_HEADS = 8
HEAD_DIM = 128
NOPE_DIM = 128
ROPE_DIM = 64
ROPE_HALF = ROPE_DIM // 2
QK_DIM = NOPE_DIM + ROPE_DIM
Q_RANK = 768
KV_RANK = 512
LAT_DIM = Q_RANK + KV_RANK
BRANCH_WIDTH = N_HEADS * HEAD_DIM
D_MIX = 2 * BRANCH_WIDTH
ROPE_THETA = 10000.0
NORM_EPS = 1e-6
LOG2E = math.log2(math.e)

LANES = 128
FEAT = 256
SMALL_W = 2 * LANES
REST_W = 5 * BRANCH_WIDTH
T_FQ, T_FV, T_FK, T_GM, T_GF = range(5)

MLA_QSCALE = QK_DIM ** -0.5 * LOG2E
FOX_QSCALE = HEAD_DIM ** -0.5 * LOG2E

PACK_TN = 256
PROJ_TM = 1024
PREP_TM = 512
ATT_T = 1024
ATT_BLK = 256
ATT_HB = 2
ATT_GROUP = 8
V_ROWS = HEAD_DIM + 16
KE_SHARED = ATT_HB * HEAD_DIM
KE_W = KE_SHARED + BRANCH_WIDTH
OUT_TM = 512

V7X_VMEM_BYTES = 64 * 1024 * 1024
VMEM_LIMIT = V7X_VMEM_BYTES * 7 // 8


def _rms(x, g):
    return x * lax.rsqrt(jnp.mean(x * x, axis=-1, keepdims=True) + NORM_EPS) * g


def _pack_kernel(w_ref, o_ref):
    o_kr = LAT_DIM
    o_gm = o_kr + ROPE_DIM
    o_fq = o_gm + BRANCH_WIDTH
    o_fk = o_fq + BRANCH_WIDTH
    o_fv = o_fk + BRANCH_WIDTH
    o_fl = o_fv + BRANCH_WIDTH
    o_gf = o_fl + N_HEADS
    cols = w_ref.shape[1]
    zeros = lambda n: jnp.zeros((n, cols), F32)
    o_ref[...] = jnp.concatenate([
        w_ref[o_fq:o_fk, :], w_ref[o_fv:o_fl, :], w_ref[o_fk:o_fv, :],
        w_ref[o_gm:o_fq, :], w_ref[o_gf:o_gf + BRANCH_WIDTH, :],
        w_ref[:LAT_DIM, :],
        w_ref[o_kr:o_gm, :], zeros(LANES - ROPE_DIM),
        w_ref[o_fl:o_gf, :], zeros(LANES - N_HEADS),
    ], axis=0).astype(BF16)


def _pack_w_in(w_in_t):
    d_in, d = w_in_t.shape
    height = REST_W + LAT_DIM + SMALL_W
    return pl.pallas_call(
        _pack_kernel,
        grid=(d // PACK_TN,),
        in_specs=[pl.BlockSpec((d_in, PACK_TN), lambda i: (0, i))],
        out_specs=pl.BlockSpec((height, PACK_TN), lambda i: (0, i)),
        out_shape=jax.ShapeDtypeStruct((height, d), BF16),
        compiler_params=pltpu.CompilerParams(
            dimension_semantics=("arbitrary",),
            vmem_limit_bytes=VMEM_LIMIT),
        name="pack",
    )(w_in_t)


def _dot_nt(a, b):
    return lax.dot_general(a, b, (((1,), (1,)), ((), ())), preferred_element_type=F32)


def _proj_kernel(x_ref, g_ref, wlat_ref, wsm_ref, wrest_ref,
                 lat_ref, sm_ref, rest_ref, h_scr):
    i, j = pl.program_id(0), pl.program_id(1)
    last = pl.num_programs(1) - 1
    p = i % 2

    def normalise(slot):
        h_scr[slot] = _rms(x_ref[...], g_ref[...]).astype(BF16)

    def rest_tile():
        acc = _dot_nt(h_scr[p], wrest_ref[...])
        scale = jnp.where(j == T_FQ, FOX_QSCALE, 1.0).astype(F32)
        rest_ref[...] = (acc * scale).astype(BF16)

    @pl.when(jnp.logical_and(i == 0, j == 0))
    def _():
        normalise(0)

    @pl.when(j == 0)
    def _():
        hb = h_scr[p]
        lat_ref[...] = _dot_nt(hb, wlat_ref[...]).astype(BF16)
        sm_ref[...] = _dot_nt(hb, wsm_ref[...])

    @pl.when(j < last)
    def _():
        rest_tile()

    @pl.when(j == last)
    def _():
        normalise(1 - p)
        rest_tile()


def _proj(x2, g_pre, w_all):
    t = x2.shape[0]
    n_rest = REST_W // BRANCH_WIDTH
    assert REST_W % LAT_DIM == 0 and (REST_W + LAT_DIM) % SMALL_W == 0
    once = pl.Buffered(1)
    return pl.pallas_call(
        _proj_kernel,
        grid=(t // PROJ_TM, n_rest),
        in_specs=[
            pl.BlockSpec((PROJ_TM, D_MODEL),
                         lambda i, j: (jnp.minimum(i + j // (n_rest - 1), t // PROJ_TM - 1), 0)),
            pl.BlockSpec((1, D_MODEL), lambda i, j: (0, 0)),
            pl.BlockSpec((LAT_DIM, D_MODEL), lambda i, j: (REST_W // LAT_DIM, 0),
                         pipeline_mode=once),
            pl.BlockSpec((SMALL_W, D_MODEL), lambda i, j: ((REST_W + LAT_DIM) // SMALL_W, 0),
                         pipeline_mode=once),
            pl.BlockSpec((BRANCH_WIDTH, D_MODEL), lambda i, j: (j, 0)),
        ],
        out_specs=[
            pl.BlockSpec((PROJ_TM, LAT_DIM), lambda i, j: (i, 0)),
            pl.BlockSpec((PROJ_TM, SMALL_W), lambda i, j: (i, 0)),
            pl.BlockSpec((PROJ_TM, BRANCH_WIDTH), lambda i, j: (i, j)),
        ],
        out_shape=[
            jax.ShapeDtypeStruct((t, LAT_DIM), BF16),
            jax.ShapeDtypeStruct((t, SMALL_W), F32),
            jax.ShapeDtypeStruct((t, REST_W), BF16),
        ],
        scratch_shapes=[pltpu.VMEM((2, PROJ_TM, D_MODEL), BF16)],
        compiler_params=pltpu.CompilerParams(
            dimension_semantics=("arbitrary", "arbitrary"),
            vmem_limit_bytes=VMEM_LIMIT),
        name="proj",
    )(x2, g_pre, w_all, w_all, w_all)


def _split3(x):
    hi = x.astype(BF16).astype(F32)
    r = x - hi
    mid = r.astype(BF16).astype(F32)
    lo = (r - mid).astype(BF16).astype(F32)
    return hi, mid, lo


def _prep_kernel(lat_ref, sm_ref, fqv_ref, pos_ref,
                 gq_ref, gkv_ref, wuqt_ref, wk_ref, wvt_ref, bf_ref, invf_ref,
                 qt_ref, kn_ref, ke_ref, vt_ref, carry_scr):
    tm = lat_ref.shape[0]

    def put_t(ref, r0, x):
        ref[0, r0:r0 + x.shape[1], :] = x.T

    @pl.when(pl.program_id(1) == 0)
    def _():
        carry_scr[...] = jnp.zeros_like(carry_scr)

    lat = lat_ref[...].astype(F32)
    qn = _rms(lat[:, :Q_RANK], gq_ref[...]).astype(BF16)
    kvn = _rms(lat[:, Q_RANK:], gkv_ref[...]).astype(BF16)

    ang = invf_ref[...] * pos_ref[0].astype(F32)
    cos_t = jnp.cos(ang)
    sin_t = jnp.sin(ang)

    def rope_t(x):
        x1, x2 = x[:ROPE_HALF], x[ROPE_HALF:ROPE_DIM]
        return x1 * cos_t - x2 * sin_t, x2 * cos_t + x1 * sin_t

    kr1, kr2 = rope_t(sm_ref[:, :LANES].T)
    k_rope = jnp.concatenate(
        [kr1, kr2, jnp.zeros((LANES - ROPE_DIM, tm), F32)], axis=0).T.astype(BF16)

    lane = lax.broadcasted_iota(jnp.int32, (tm, LANES), 1)
    fl = sm_ref[:, LANES:] + bf_ref[...]
    logf = jnp.minimum(fl, 0.0) - jnp.log1p(jnp.exp(-jnp.abs(fl)))
    row = lax.broadcasted_iota(jnp.int32, (tm, tm), 0)
    col = lax.broadcasted_iota(jnp.int32, (tm, tm), 1)
    tri = jnp.where(row >= col, 1.0, 0.0).astype(BF16)
    c = carry_scr[...]
    for piece in _split3(logf):
        c = c + jnp.dot(tri, piece.astype(BF16), preferred_element_type=F32)
    carry_scr[...] = c[tm - 1:tm, :]
    c2 = c * LOG2E
    c2_t = c2.T

    aug = 16
    srow = lax.broadcasted_iota(jnp.int32, (aug, tm), 0)
    q_zero = jnp.zeros((FEAT - HEAD_DIM - aug, tm), BF16)
    q_pad = jnp.zeros((FEAT - QK_DIM, tm), BF16)

    def fox_head(h):
        c0 = (N_HEADS + h) * FEAT
        s0 = h * HEAD_DIM
        hi, mid, lo = _split3(jnp.broadcast_to(c2[:, h:h + 1], (tm, LANES)))
        ck = jnp.where(lane < 3, 1.0, jnp.where(lane == 3, -hi, jnp.where(
            lane == 4, -mid, jnp.where(lane == 5, -lo, 0.0))))
        ke_ref[:, KE_SHARED + s0:KE_SHARED + s0 + HEAD_DIM] = ck.astype(BF16)
        hi_t, mid_t, lo_t = _split3(c2_t[h:h + 1, :])
        cq_t = jnp.where(srow == 0, hi_t, jnp.where(srow == 1, mid_t, jnp.where(
            srow == 2, lo_t, jnp.where(srow < 6, 1.0, 0.0))))
        put_t(qt_ref, c0, fqv_ref[:, s0:s0 + HEAD_DIM])
        qt_ref[0, c0 + HEAD_DIM:c0 + HEAD_DIM + aug, :] = cq_t.astype(BF16)
        qt_ref[0, c0 + HEAD_DIM + aug:c0 + FEAT, :] = q_zero
        put_t(vt_ref, (N_HEADS + h) * V_ROWS,
              fqv_ref[:, BRANCH_WIDTH + s0:BRANCH_WIDTH + s0 + HEAD_DIM])

    def mla_q_head(h, q_nope, base, q_rope):
        c0, l0 = h * FEAT, (h - base) * NOPE_DIM
        qr1, qr2 = rope_t(q_rope[h * ROPE_DIM:(h + 1) * ROPE_DIM])
        qt_ref[0, c0:c0 + NOPE_DIM, :] = q_nope[l0:l0 + NOPE_DIM].astype(BF16)
        qt_ref[0, c0 + NOPE_DIM:c0 + NOPE_DIM + ROPE_HALF, :] = qr1.astype(BF16)
        qt_ref[0, c0 + NOPE_DIM + ROPE_HALF:c0 + QK_DIM, :] = qr2.astype(BF16)
        qt_ref[0, c0 + QK_DIM:c0 + FEAT, :] = q_pad

    half = N_HEADS // 2
    n_nope = N_HEADS * NOPE_DIM
    q_a = _dot_nt(wuqt_ref[:n_nope // 2, :], qn) * MLA_QSCALE
    for h in range(half):
        fox_head(h)
    q_b = _dot_nt(wuqt_ref[n_nope // 2:n_nope, :], qn) * MLA_QSCALE
    for h in range(half, N_HEADS):
        fox_head(h)
    q_r = _dot_nt(wuqt_ref[n_nope:, :], qn) * MLA_QSCALE
    k_nope = jnp.dot(kvn, wk_ref[...], preferred_element_type=F32)
    for h in range(half):
        mla_q_head(h, q_a, 0, q_r)
    v_t = _dot_nt(wvt_ref[...], kvn)
    for h in range(half, N_HEADS):
        mla_q_head(h, q_b, half, q_r)
    for h in range(N_HEADS):
        c0, d0 = h * FEAT, h * HEAD_DIM
        kn_ref[:, d0:d0 + HEAD_DIM] = k_nope[:, d0:d0 + HEAD_DIM].astype(BF16)
        vt_ref[0, h * V_ROWS:h * V_ROWS + HEAD_DIM, :] = v_t[d0:d0 + HEAD_DIM].astype(BF16)
    for hh in range(ATT_HB):
        ke_ref[:, hh * HEAD_DIM:(hh + 1) * HEAD_DIM] = k_rope
    pad = V_ROWS - HEAD_DIM
    ones_blk = jnp.where(lax.broadcasted_iota(jnp.int32, (pad, tm), 0) == 0,
                         1.0, 0.0).astype(BF16)
    for h in range(2 * N_HEADS):
        vt_ref[0, h * V_ROWS + HEAD_DIM:(h + 1) * V_ROWS, :] = ones_blk


def _prep(lat, small, rest, pos, g_q, g_kv, w_uq_t, w_k, w_v_t, b_f, inv_f, batch, seq):
    t = lat.shape[0]
    ns = seq // PREP_TM
    per = ATT_T // PREP_TM
    nt = t // ATT_T
    row = lambda b, i: (b * ns + i, 0)
    const = lambda b, i: (0, 0)
    tile_t = lambda b, i: ((b * ns + i) // per, 0, (b * ns + i) % per)
    assert (T_FQ, T_FV) == (0, 1)
    return pl.pallas_call(
        _prep_kernel,
        grid=(batch, ns),
        in_specs=[
            pl.BlockSpec((PREP_TM, LAT_DIM), row),
            pl.BlockSpec((PREP_TM, SMALL_W), row),
            pl.BlockSpec((PREP_TM, 2 * BRANCH_WIDTH), row),
            pl.BlockSpec((1, 1, PREP_TM), lambda b, i: (b * ns + i, 0, 0)),
            pl.BlockSpec((1, Q_RANK), const),
            pl.BlockSpec((1, KV_RANK), const),
            pl.BlockSpec((N_HEADS * QK_DIM, Q_RANK), const),
            pl.BlockSpec((KV_RANK, BRANCH_WIDTH), const),
            pl.BlockSpec((BRANCH_WIDTH, KV_RANK), const),
            pl.BlockSpec((1, LANES), const),
            pl.BlockSpec((ROPE_HALF, 1), const),
        ],
        out_specs=[
            pl.BlockSpec((1, 2 * N_HEADS * FEAT, PREP_TM), tile_t),
            pl.BlockSpec((PREP_TM, BRANCH_WIDTH), lambda b, i: (b * ns + i, T_FQ)),
            pl.BlockSpec((PREP_TM, KE_W), row),
            pl.BlockSpec((1, 2 * N_HEADS * V_ROWS, PREP_TM), tile_t),
        ],
        out_shape=[
            jax.ShapeDtypeStruct((nt, 2 * N_HEADS * FEAT, ATT_T), BF16),
            jax.ShapeDtypeStruct(rest.shape, BF16),
            jax.ShapeDtypeStruct((t, KE_W), BF16),
            jax.ShapeDtypeStruct((nt, 2 * N_HEADS * V_ROWS, ATT_T), BF16),
        ],
        scratch_shapes=[pltpu.VMEM((1, LANES), F32)],
        input_output_aliases={2: 1},
        compiler_params=pltpu.CompilerParams(
            dimension_semantics=("arbitrary", "arbitrary"),
            vmem_limit_bytes=VMEM_LIMIT),
        name="prep",
    )(lat, small, rest, pos, g_q, g_kv, w_uq_t, w_k, w_v_t, b_f, inv_f)


def _attn_kernel(qt_ref, km_ref, ke_ref, vt_ref, o_ref, s0, s1, cm0, cm1, m_scr, acc_scr):
    t = ATT_T
    nq = qt_ref.shape[0]
    steps = [(i, j) for i in range(nq) for j in range(i + 1)]
    nsteps = len(steps)
    s_buf, cm_buf = (s0, s1), (cm0, cm1)

    nblk = t // ATT_BLK
    blk = ATT_BLK

    def a_begin():
        return [[None] * nblk for _ in range(ATT_HB)]

    def a_piece(ij, slot, diagonal, hh, r, cmax):
        i, j = ij
        f0, r0 = hh * FEAT, r * blk
        krows = slice(j * t + r0, j * t + r0 + blk)
        kcols = slice(hh * HEAD_DIM, (hh + 1) * HEAD_DIM)
        k = jnp.concatenate([km_ref[krows, kcols], ke_ref[krows, kcols]], axis=1)
        c_lo = r0 if diagonal else 0
        s = jnp.dot(k, qt_ref[i, f0:f0 + FEAT, c_lo:], preferred_element_type=F32)
        if diagonal:
            key = lax.broadcasted_iota(jnp.int32, (blk, blk), 0)
            qry = lax.broadcasted_iota(jnp.int32, (blk, blk), 1)
            parts = [jnp.where(key <= qry, s[:, :blk], -jnp.inf)]
            if s.shape[1] > blk:
                parts.append(s[:, blk:])
            s = jnp.concatenate(parts, axis=1)
        s_buf[slot][hh, r0:r0 + blk, c_lo:] = s
        for qb in range(c_lo // blk, nblk):
            part = jnp.max(s[:, qb * blk - c_lo:(qb + 1) * blk - c_lo], axis=0, keepdims=True)
            cmax[hh][qb] = part if cmax[hh][qb] is None else jnp.maximum(cmax[hh][qb], part)

    def a_end(slot, hh, cmax):
        for qb in range(nblk):
            cm_buf[slot][hh, :, qb * blk:(qb + 1) * blk] = cmax[hh][qb]

    def bc_begin(ij, slot, hh):
        i, j = ij
        if j == 0:
            return dict(m_new=cm_buf[slot][hh], alpha=None, pv=None)
        m_old = m_scr[hh]
        m_new = jnp.maximum(m_old, cm_buf[slot][hh])
        return dict(m_new=m_new, alpha=jnp.exp2(m_old - m_new), pv=None)

    def rescaled(st, hh, cols, pv):
        if st["alpha"] is None:
            return pv
        return st["alpha"][:, cols] * acc_scr[hh, :, cols] + pv

    def bc_piece(ij, slot, hh, r, st):
        i, j = ij
        d0, r0 = hh * V_ROWS, r * blk
        p = jnp.exp2(s_buf[slot][hh, r0:r0 + blk, :] - st["m_new"])
        pv = jnp.dot(vt_ref[j, d0:d0 + V_ROWS, r0:r0 + blk], p.astype(BF16),
                     preferred_element_type=F32)
        st["pv"] = pv if st["pv"] is None else st["pv"] + pv

    def bc_end(hh, st):
        m_scr[hh] = st["m_new"]
        acc_scr[hh] = rescaled(st, hh, slice(None), st["pv"])

    def bcd_begin(ij, slot, hh):
        st = bc_begin(ij, slot, hh)
        st["pv"] = [None] * nblk
        return st

    def bcd_piece(ij, slot, hh, r, st):
        i, j = ij
        d0, r0 = hh * V_ROWS, r * blk
        p = jnp.exp2(s_buf[slot][hh, r0:r0 + blk, r0:] - st["m_new"][:, r0:])
        pv = jnp.dot(vt_ref[j, d0:d0 + V_ROWS, r0:r0 + blk], p.astype(BF16),
                     preferred_element_type=F32)
        for qb in range(r, nblk):
            part = pv[:, (qb - r) * blk:(qb - r + 1) * blk]
            st["pv"][qb] = part if st["pv"][qb] is None else st["pv"][qb] + part

    def bcd_end(ij, hh, st):
        i, j = ij
        for qb in range(nblk):
            cols = slice(qb * blk, (qb + 1) * blk)
            acc = rescaled(st, hh, cols, st["pv"][qb])
            out = acc[:HEAD_DIM] / acc[HEAD_DIM:HEAD_DIM + 1]
            o_ref[i * t + qb * blk:i * t + (qb + 1) * blk,
                  hh * HEAD_DIM:(hh + 1) * HEAD_DIM] = out.T.astype(BF16)

    def run(slot_a, ija, ijb, a_is_diag, b_is_diag):
        slot_b = 1 - slot_a
        begin, piece = (bcd_begin, bcd_piece) if b_is_diag else (bc_begin, bc_piece)
        cmax = a_begin()
        sts = [begin(ijb, slot_b, hh) for hh in range(ATT_HB)]
        for r in range(nblk):
            for hh in range(ATT_HB):
                piece(ijb, slot_b, hh, r, sts[hh])
                if ija is not None:
                    a_piece(ija, slot_a, a_is_diag, hh, r, cmax)
        for hh in range(ATT_HB):
            if ija is not None:
                a_end(slot_a, hh, cmax)
            if b_is_diag:
                bcd_end(ijb, hh, sts[hh])
            else:
                bc_end(hh, sts[hh])

    def iteration(m):
        if m == 0:
            cmax0 = a_begin()
            for hh in range(ATT_HB):
                for r in range(nblk):
                    a_piece(steps[0], 0, True, hh, r, cmax0)
                a_end(0, hh, cmax0)
        elif m < nsteps:
            sa, sb = steps[m], steps[m - 1]
            run(m % 2, sa, sb, sa[0] == sa[1], sb[0] == sb[1])
        else:
            run(m % 2, None, steps[-1], False, True)

    def block(first):
        def emit():
            for m in range(first, min(first + ATT_GROUP, nsteps + 1)):
                iteration(m)
        return emit

    firsts = list(range(0, nsteps + 1, ATT_GROUP))

    def trip(p, carry):
        for q, first in enumerate(firsts):
            pl.when(p == q)(block(first))
        return carry

    lax.fori_loop(0, len(firsts), trip, 0)


def _attn(qt, rest, ke, vt, batch, seq):
    t = ke.shape[0]
    per_tile = BRANCH_WIDTH // (ATT_HB * HEAD_DIM)
    km_block = lambda h: jnp.where(h < per_tile, T_FQ * per_tile + h,
                                   T_FK * per_tile + h - per_tile)
    nq = seq // ATT_T
    nh = 2 * N_HEADS // ATT_HB
    row = pltpu.VMEM((ATT_HB, 1, ATT_T), F32)
    tile = pltpu.VMEM((ATT_HB, ATT_T, ATT_T), F32)
    return pl.pallas_call(
        _attn_kernel,
        grid=(batch, nh),
        in_specs=[
            pl.BlockSpec((nq, ATT_HB * FEAT, ATT_T), lambda b, h: (b, h, 0)),
            pl.BlockSpec((seq, ATT_HB * HEAD_DIM), lambda b, h: (b, km_block(h))),
            pl.BlockSpec((seq, ATT_HB * HEAD_DIM),
                         lambda b, h: (b, jnp.maximum(h - per_tile + 1, 0))),
            pl.BlockSpec((nq, ATT_HB * V_ROWS, ATT_T), lambda b, h: (b, h, 0)),
        ],
        out_specs=pl.BlockSpec((seq, ATT_HB * HEAD_DIM), lambda b, h: (b, h)),
        out_shape=jax.ShapeDtypeStruct((t, D_MIX), BF16),
        scratch_shapes=[
            tile, tile, row, row, row,
            pltpu.VMEM((ATT_HB, V_ROWS, ATT_T), F32),
        ],
        compiler_params=pltpu.CompilerParams(
            dimension_semantics=("arbitrary", "arbitrary"),
            vmem_limit_bytes=VMEM_LIMIT),
        name="attn",
    )(qt, rest, ke, vt)


def _silu(g):
    return g / (1.0 + jnp.exp(-g))


def _out_kernel(o_ref, gm_ref, gf_ref, x_ref, w_ref, g_ref, y_ref):
    o = o_ref[...].astype(F32)
    om = (o[:, :BRANCH_WIDTH] * _silu(gm_ref[...].astype(F32))).astype(BF16)
    of = (o[:, BRANCH_WIDTH:] * _silu(gf_ref[...].astype(F32))).astype(BF16)
    y = (jnp.dot(om, w_ref[:BRANCH_WIDTH, :], preferred_element_type=F32)
         + jnp.dot(of, w_ref[BRANCH_WIDTH:, :], preferred_element_type=F32))
    y_ref[...] = x_ref[...] + _rms(y, g_ref[...])


def _out(o, rest, x2, w_out, g_post):
    t = o.shape[0]
    return pl.pallas_call(
        _out_kernel,
        grid=(t // OUT_TM,),
        in_specs=[
            pl.BlockSpec((OUT_TM, D_MIX), lambda i: (i, 0)),
            pl.BlockSpec((OUT_TM, BRANCH_WIDTH), lambda i: (i, T_GM)),
            pl.BlockSpec((OUT_TM, BRANCH_WIDTH), lambda i: (i, T_GF)),
            pl.BlockSpec((OUT_TM, D_MODEL), lambda i: (i, 0)),
            pl.BlockSpec((D_MIX, D_MODEL), lambda i: (0, 0)),
            pl.BlockSpec((1, D_MODEL), lambda i: (0, 0)),
        ],
        out_specs=pl.BlockSpec((OUT_TM, D_MODEL), lambda i: (i, 0)),
        out_shape=jax.ShapeDtypeStruct((t, D_MODEL), F32),
        compiler_params=pltpu.CompilerParams(
            dimension_semantics=("arbitrary",),
            vmem_limit_bytes=VMEM_LIMIT),
        name="out",
    )(o, rest, rest, x2, w_out, g_post)


def _layer(x2, pos, inv_f, g_pre, w_in, g_q, w_uq, g_kv, w_ukv, b_forget,
           w_out, g_post, batch, seq):
    w_all = _pack_w_in(jnp.swapaxes(w_in, 0, 1))
    w_q = w_uq.T.reshape(N_HEADS, QK_DIM, Q_RANK)
    w_uq_t = jnp.concatenate([
        w_q[:, :NOPE_DIM].reshape(N_HEADS * NOPE_DIM, Q_RANK),
        w_q[:, NOPE_DIM:].reshape(N_HEADS * ROPE_DIM, Q_RANK)], axis=0).astype(BF16)
    w_kv = w_ukv.reshape(KV_RANK, N_HEADS, NOPE_DIM + HEAD_DIM)
    w_k = w_kv[:, :, :NOPE_DIM].reshape(KV_RANK, BRANCH_WIDTH).astype(BF16)
    w_v_t = w_kv[:, :, NOPE_DIM:].reshape(KV_RANK, BRANCH_WIDTH).T.astype(BF16)
    w_out_b = w_out.astype(BF16)
    b_f = jnp.pad(b_forget, (0, LANES - N_HEADS))[None, :]

    lat, small, rest = _proj(x2, g_pre[None, :], w_all)
    qt, rest, ke, vt = _prep(lat, small, rest, pos, g_q[None, :], g_kv[None, :],
                       w_uq_t, w_k, w_v_t, b_f, inv_f, batch, seq)
    o = _attn(qt, rest, ke, vt, batch, seq)
    return _out(o, rest, x2, w_out_b, g_post[None, :])


def kernel(x, positions, g_pre, w_in, g_q_latent, w_uq, g_kv_latent, w_ukv,
           b_forget, w_out, g_post):
    batch, seq, d = x.shape
    depth = g_pre.shape[0]
    x2 = x.reshape(batch * seq, d)
    pos = positions.reshape(batch * seq // PREP_TM, 1, PREP_TM)
    inv_f = (ROPE_THETA ** (-jnp.arange(0, ROPE_DIM, 2, dtype=F32) / ROPE_DIM))[:, None]
    for l in range(depth):
        x2 = _layer(x2, pos, inv_f, g_pre[l], w_in[l], g_q_latent[l], w_uq[l],
                    g_kv_latent[l], w_ukv[l], b_forget[l], w_out[l], g_post[l],
                    batch, seq)
    return x2.reshape(batch, seq, d)
```
